```python
import math
import jax, jax.numpy as jnp
from jax import lax
import numpy as np

D_MODEL = 2048
BATCH = 4
SEQ = 4096
DEPTH = 1

HEAD_DIM = 128
MOBA_HEADS = D_MODEL // (2 * HEAD_DIM)
FOX_HEADS = D_MODEL // (2 * HEAD_DIM)
WA = MOBA_HEADS * HEAD_DIM
WB = FOX_HEADS * HEAD_DIM
MOBA_BLOCK = 256
MOBA_TOPK = 3
MOBA_Q_CHUNK = 32
FOX_Q_CHUNK = 128
ROPE_THETA = 10000.0
PLE_DIM = 256
RMS_EPS = 1e-6

_SECTION_WIDTHS = (WA, WA, WA, WA, WB, WB, WB, FOX_HEADS, WB, D_MODEL, D_MODEL)
N_IN = int(sum(_SECTION_WIDTHS))
SPLIT_POINTS = tuple(int(s) for s in np.cumsum(_SECTION_WIDTHS)[:-1])

kernel_name = "hybrid_moba_fox_gated_parallel"


def rmsnorm(x, g):
    xf = x.astype(jnp.float32)
    y = xf * lax.rsqrt(jnp.mean(xf * xf, axis=-1, keepdims=True) + RMS_EPS)
    return (y * g.astype(jnp.float32)).astype(x.dtype)


def rope(x, positions):
    inv = ROPE_THETA ** (-jnp.arange(0, HEAD_DIM, 2, dtype=jnp.float32) / HEAD_DIM)
    ang = positions.astype(jnp.float32)[..., None] * inv
    cos = jnp.cos(ang)[:, :, None, :]
    sin = jnp.sin(ang)[:, :, None, :]
    xf = x.astype(jnp.float32)
    x1, x2 = xf[..., : HEAD_DIM // 2], xf[..., HEAD_DIM // 2:]
    out = jnp.concatenate([x1 * cos - x2 * sin, x2 * cos + x1 * sin], axis=-1)
    return out.astype(x.dtype)


def moba_attention(q, k, v):
    B, H, S, hd = q.shape
    nb = -(-S // MOBA_BLOCK)
    pad = nb * MOBA_BLOCK - S
    kp = jnp.pad(k, ((0, 0), (0, 0), (0, pad), (0, 0)))
    vp = jnp.pad(v, ((0, 0), (0, 0), (0, pad), (0, 0)))
    kb = kp.reshape(B, H, nb, MOBA_BLOCK, hd)
    vb = vp.reshape(B, H, nb, MOBA_BLOCK, hd)
    kbar = jnp.mean(kb.astype(jnp.float32), axis=3)

    t = jnp.arange(S)
    qblk = t // MOBA_BLOCK
    gscore = jnp.einsum('bhsd,bhnd->bhsn', q.astype(jnp.float32), kbar)
    past = jnp.arange(nb)[None, :] < qblk[:, None]
    gscore = jnp.where(past[None, None], gscore, -jnp.inf)
    n_sel = min(MOBA_TOPK, nb)
    _, top_idx = lax.top_k(gscore, n_sel)
    rank_ok = jnp.arange(n_sel)[None, :] < qblk[:, None]
    own = jnp.broadcast_to(qblk[None, None, :, None], (B, H, S, 1))
    idx = jnp.concatenate([top_idx, own.astype(top_idx.dtype)], axis=-1)
    blk_ok = jnp.concatenate([rank_ok, jnp.ones((S, 1), bool)], axis=-1)
    n = n_sel + 1

    nq = S // MOBA_Q_CHUNK
    qc = q.reshape(B, H, nq, MOBA_Q_CHUNK, hd).transpose(2, 0, 1, 3, 4)
    ic = idx.reshape(B, H, nq, MOBA_Q_CHUNK, n).transpose(2, 0, 1, 3, 4)
    tc = t.reshape(nq, MOBA_Q_CHUNK)
    vc = blk_ok.reshape(nq, MOBA_Q_CHUNK, n)
    bi = jnp.arange(B)[:, None, None, None]
    hi = jnp.arange(H)[None, :, None, None]
    scale = 1.0 / math.sqrt(hd)

    def step(args):
        q_c, i_c, t_c, v_c = args
        kg = kb[bi, hi, i_c]
        vg = vb[bi, hi, i_c]
        logits = jnp.einsum('bhqd,bhqnkd->bhqnk', q_c, kg).astype(jnp.float32) * scale
        kpos = i_c[..., None] * MOBA_BLOCK + jnp.arange(MOBA_BLOCK)
        mask = (kpos <= t_c[None, None, :, None, None]) & v_c[None, None, :, :, None]
        logits = jnp.where(mask, logits, -jnp.inf)
        w = jax.nn.softmax(logits.reshape(B, H, MOBA_Q_CHUNK, n * MOBA_BLOCK), axis=-1)
        w = w.reshape(B, H, MOBA_Q_CHUNK, n, MOBA_BLOCK).astype(v.dtype)
        return jnp.einsum('bhqnk,bhqnkd->bhqd', w, vg)

    out = lax.map(step, (qc, ic, tc, vc))
    return out.transpose(1, 2, 0, 3, 4).reshape(B, H, S, hd)


def fox_attention(q, k, v, logf):
    B, H, S, hd = q.shape
    c = lax.cumsum(logf, axis=2)
    nq = S // FOX_Q_CHUNK
    qc = q.reshape(B, H, nq, FOX_Q_CHUNK, hd).transpose(2, 0, 1, 3, 4)
    cq = c.reshape(B, H, nq, FOX_Q_CHUNK).transpose(2, 0, 1, 3)
    tc = jnp.arange(S).reshape(nq, FOX_Q_CHUNK)
    spos = jnp.arange(S)
    scale = 1.0 / math.sqrt(hd)

    def step(args):
        q_c, c_c, t_c = args
        logits = jnp.einsum('bhqd,bhsd->bhqs', q_c, k).astype(jnp.float32) * scale
        logits = logits + (c_c[..., None] - c[:, :, None, :])
        mask = spos[None, :] <= t_c[:, None]
        logits = jnp.where(mask[None, None], logits, -jnp.inf)
        w = jax.nn.softmax(logits, axis=-1).astype(v.dtype)
        return jnp.einsum('bhqs,bhsd->bhqd', w, v)

    out = lax.map(step, (qc, cq, tc))
    return out.transpose(1, 2, 0, 3, 4).reshape(B, H, S, hd)


def setup_inputs(seed: int = 0) -> dict:
    key = jax.random.key(seed)
    ks = jax.random.split(key, 14)
    f32 = jnp.float32
    x = jax.random.normal(ks[0], (BATCH, SEQ, D_MODEL), f32)
    p = jax.random.normal(ks[1], (DEPTH, BATCH, SEQ, PLE_DIM), f32)
    positions = jnp.broadcast_to(jnp.arange(SEQ, dtype=jnp.int32)[None, :], (BATCH, SEQ))
    g_norm = 1.0 + 0.02 * jax.random.normal(ks[2], (DEPTH, D_MODEL), f32)
    w_in = jax.random.normal(ks[3], (DEPTH, D_MODEL, N_IN), f32) * D_MODEL ** -0.5
    b_f = 1.0 + 0.1 * jax.random.normal(ks[4], (DEPTH, FOX_HEADS), f32)
    w_branch_a = jax.random.normal(ks[5], (DEPTH, WA, D_MODEL), f32) * WA ** -0.5
    w_branch_b = jax.random.normal(ks[6], (DEPTH, WB, D_MODEL), f32) * WB ** -0.5
    w_out = jax.random.normal(ks[7], (DEPTH, D_MODEL, D_MODEL), f32) * D_MODEL ** -0.5
    g_ple = 1.0 + 0.02 * jax.random.normal(ks[8], (DEPTH, D_MODEL), f32)
    w_ple_gate = jax.random.normal(ks[9], (DEPTH, D_MODEL, D_MODEL), f32) * D_MODEL ** -0.5
    w_ple_up = jax.random.normal(ks[10], (DEPTH, PLE_DIM, D_MODEL), f32) * PLE_DIM ** -0.5
    g_final = 1.0 + 0.02 * jax.random.normal(ks[11], (D_MODEL,), f32)
    return {"x": x, "p": p, "positions": positions, "g_norm": g_norm, "w_in": w_in,
            "b_f": b_f, "w_branch_a": w_branch_a, "w_branch_b": w_branch_b, "w_out": w_out,
            "g_ple": g_ple, "w_ple_gate": w_ple_gate, "w_ple_up": w_ple_up, "g_final": g_final}


def reference(x, p, positions, g_norm, w_in, b_f, w_branch_a, w_branch_b, w_out,
              g_ple, w_ple_gate, w_ple_up, g_final):
    B, S, _ = x.shape
    for i in range(DEPTH):
        h = rmsnorm(x, g_norm[i])
        proj = h @ w_in[i]
        qa, ka, va, za, qb, kb, vb, fb, zb, ga, gb = jnp.split(proj, SPLIT_POINTS, axis=-1)

        qa = rope(qa.reshape(B, S, MOBA_HEADS, HEAD_DIM), positions).transpose(0, 2, 1, 3)
        ka = rope(ka.reshape(B, S, MOBA_HEADS, HEAD_DIM), positions).transpose(0, 2, 1, 3)
        va = va.reshape(B, S, MOBA_HEADS, HEAD_DIM).transpose(0, 2, 1, 3)
        oa = moba_attention(qa, ka, va).transpose(0, 2, 1, 3).reshape(B, S, WA)
        ya = (oa * jax.nn.silu(za)) @ w_branch_a[i]

        qb = qb.reshape(B, S, FOX_HEADS, HEAD_DIM).transpose(0, 2, 1, 3)
        kb = kb.reshape(B, S, FOX_HEADS, HEAD_DIM).transpose(0, 2, 1, 3)
        vb = vb.reshape(B, S, FOX_HEADS, HEAD_DIM).transpose(0, 2, 1, 3)
        logf = jax.nn.log_sigmoid((fb + b_f[i]).astype(jnp.float32)).transpose(0, 2, 1)
        ob = fox_attention(qb, kb, vb, logf).transpose(0, 2, 1, 3).reshape(B, S, WB)
        yb = (ob * jax.nn.silu(zb)) @ w_branch_b[i]

        mixed = jax.nn.sigmoid(ga) * ya + jax.nn.sigmoid(gb) * yb
        x = x + mixed @ w_out[i]

        pg = jax.nn.sigmoid(rmsnorm(x, g_ple[i]) @ w_ple_gate[i])
        x = x + (p[i] @ w_ple_up[i]) * pg
    return rmsnorm(x, g_final)
```

```python
import functools
import math

import jax
import jax.numpy as jnp
from jax import lax
from jax.experimental import pallas as pl
from jax.experimental.pallas import tpu as pltpu

HEAD_DIM = 128
MOBA_BLOCK = 256
MOBA_TOPK = 3
ROPE_THETA = 10000.0
RMS_EPS = 1e-6

V7X_VMEM_BYTES = 64 * 1024 * 1024
VMEM_LIMIT_BYTES = V7X_VMEM_BYTES * 7 // 8
LANES = 128

F32 = jnp.float32
BF16 = jnp.bfloat16
NEG_INF = float("-inf")
NT_DIMS = (((1,), (1,)), ((), ()))


def _compiler_params(n_grid_dims):
    return pltpu.CompilerParams(
        dimension_semantics=("arbitrary",) * n_grid_dims,
        vmem_limit_bytes=VMEM_LIMIT_BYTES,
    )


def _resident(block_shape, index_map):
    return pl.BlockSpec(block_shape, index_map, pipeline_mode=pl.Buffered(1))


def _in_proj_kernel(x_ref, pos_ref, g_ref, inv_ref, w_ref, wf_ref, proj_ref, f_ref,
                    h_scr, cos_scr, sin_scr, *, qa_tile, ka_tile, qb_tile, q_scale):
    n = pl.program_id(1)

    @pl.when(n == 0)
    def _normalize_and_tables():
        x = x_ref[...]
        ms = jnp.mean(x * x, axis=-1, keepdims=True)
        h = (x * lax.rsqrt(ms + RMS_EPS) * g_ref[...]).astype(BF16)
        h_scr[...] = h
        f_ref[...] = jnp.dot(h, wf_ref[...], preferred_element_type=F32)
        ang = pos_ref[...].astype(F32) * inv_ref[...]
        lane = lax.broadcasted_iota(jnp.int32, ang.shape, 1)
        sin = jnp.sin(ang)
        cos_scr[...] = jnp.cos(ang)
        sin_scr[...] = jnp.where(lane < HEAD_DIM // 2, -sin, sin)

    acc = jnp.dot(h_scr[...], w_ref[...], preferred_element_type=F32)
    is_rope = jnp.logical_or(n == qa_tile, n == ka_tile)

    @pl.when(is_rope)
    def _rotary():
        scale = jnp.where(n == qa_tile, q_scale, 1.0).astype(F32)
        cos = cos_scr[...]
        sin = sin_scr[...]
        for j in range(acc.shape[1] // HEAD_DIM):
            sl = slice(j * HEAD_DIM, (j + 1) * HEAD_DIM)
            a = acc[:, sl]
            r = a * cos + pltpu.roll(a, HEAD_DIM // 2, 1) * sin
            proj_ref[:, sl] = (r * scale).astype(BF16)

    @pl.when(n == qb_tile)
    def _scaled():
        proj_ref[...] = (acc * q_scale).astype(BF16)

    @pl.when(jnp.logical_not(jnp.logical_or(is_rope, n == qb_tile)))
    def _plain():
        proj_ref[...] = acc.astype(BF16)


def _in_proj(x2, pos, g, inv_full, w_main, w_f, *, tm, tn, width_a):
    t, d = x2.shape
    n_main = w_main.shape[1]
    assert width_a == tn, "one column tile per attention section"
    kern = functools.partial(
        _in_proj_kernel, qa_tile=0, ka_tile=1, qb_tile=4, q_scale=1.0 / math.sqrt(HEAD_DIM))
    return pl.pallas_call(
        kern,
        grid=(t // tm, n_main // tn),
        in_specs=[
            pl.BlockSpec((tm, d), lambda i, n: (i, 0)),
            pl.BlockSpec((tm, 1), lambda i, n: (i, 0)),
            _resident((1, d), lambda i, n: (0, 0)),
            _resident((1, LANES), lambda i, n: (0, 0)),
            pl.BlockSpec((d, tn), lambda i, n: (0, n)),
            _resident((d, LANES), lambda i, n: (0, 0)),
        ],
        out_specs=[
            pl.BlockSpec((tm, tn), lambda i, n: (i, n)),
            pl.BlockSpec((tm, LANES), lambda i, n: (i, 0)),
        ],
        out_shape=[
            jax.ShapeDtypeStruct((t, n_main), BF16),
            jax.ShapeDtypeStruct((t, LANES), F32),
        ],
        scratch_shapes=[
            pltpu.VMEM((tm, d), BF16),
            pltpu.VMEM((tm, LANES), F32),
            pltpu.VMEM((tm, LANES), F32),
        ],
        compiler_params=_compiler_params(2),
        name="in_proj",
    )(x2, pos, g, inv_full, w_main, w_f)


def _fox_gate_kernel(f_ref, b_ref, c_ref, *, chunk):
    n_chunks = f_ref.shape[0] // chunk
    r = lax.broadcasted_iota(jnp.int32, (chunk, chunk), 0)
    c = lax.broadcasted_iota(jnp.int32, (chunk, chunk), 1)
    tri = (c <= r).astype(F32)

    def body(j, carry):
        off = pl.multiple_of(j * chunk, chunk)
        z = f_ref[pl.ds(off, chunk), :] + b_ref[...]
        logf = jnp.minimum(z, 0.0) - jnp.log1p(jnp.exp(-jnp.abs(z)))
        csum = jnp.dot(tri, logf, precision=lax.Precision.HIGHEST,
                       preferred_element_type=F32) + carry
        c_ref[pl.ds(off, chunk), :] = csum
        return csum[chunk - 1:chunk, :]

    lax.fori_loop(0, n_chunks, body, jnp.zeros((1, LANES), F32))


def _fox_gate(f, b_pad, *, batch, seq):
    return pl.pallas_call(
        functools.partial(_fox_gate_kernel, chunk=MOBA_BLOCK),
        grid=(batch,),
        in_specs=[
            pl.BlockSpec((seq, LANES), lambda b: (b, 0)),
            _resident((1, LANES), lambda b: (0, 0)),
        ],
        out_specs=pl.BlockSpec((seq, LANES), lambda b: (b, 0)),
        out_shape=jax.ShapeDtypeStruct(f.shape, F32),
        compiler_params=_compiler_params(1),
        name="fox_gate",
    )(f, b_pad)


def _attn_kernel(*refs, moba, n_blocks):
    if moba:
        q_ref, k_ref, v_ref, o_ref, vt_scr, kbar_scr, sel_scr = refs
    else:
        q_ref, k_ref, v_ref, c_ref, o_ref, vt_scr, crep_scr = refs
    blk = MOBA_BLOCK
    h = pl.program_id(1)
    i = pl.program_id(2)

    @pl.when(i == 0)
    def _per_head_setup():
        def body(j, carry):
            off = pl.multiple_of(j * blk, blk)
            vt_scr[j] = v_ref[pl.ds(off, blk), :].astype(F32).T.astype(BF16)
            if moba:
                kb = k_ref[pl.ds(off, blk), :].astype(F32)
                kbar_scr[pl.ds(j, 1), :] = jnp.mean(kb, axis=0, keepdims=True)
            else:
                c = c_ref[pl.ds(off, blk), :]
                lane = lax.broadcasted_iota(jnp.int32, c.shape, 1)
                col = jnp.sum(jnp.where(lane == h, c, 0.0), axis=1, keepdims=True)
                crep_scr[j] = jnp.broadcast_to(col, (blk, LANES))
            return carry
        lax.fori_loop(0, n_blocks, body, 0)

    q = q_ref[...]
    tq = q.shape[0]

    def scores(j):
        off = pl.multiple_of(j * blk, blk)
        kb = k_ref[pl.ds(off, blk), :]
        s = lax.dot_general(kb, q, NT_DIMS, preferred_element_type=F32)
        if not moba:
            c = crep_scr[j]
            s = s - jnp.concatenate([c] * (tq // LANES), axis=1)
        return s

    if moba:
        g = lax.dot_general(kbar_scr[...], q.astype(F32), NT_DIMS,
                            precision=lax.Precision.HIGHEST,
                            preferred_element_type=F32)
        blk_id = lax.broadcasted_iota(jnp.int32, g.shape, 0)
        g = jnp.where(blk_id < i, g, NEG_INF)
        sel_bias = jnp.full(g.shape, NEG_INF, F32)
        for _ in range(MOBA_TOPK):
            mx = jnp.max(g, axis=0, keepdims=True)
            first = jnp.min(jnp.where(g == mx, blk_id, n_blocks), axis=0, keepdims=True)
            pick = jnp.logical_and(blk_id == first, mx > NEG_INF)
            sel_bias = jnp.where(pick, 0.0, sel_bias)
            g = jnp.where(pick, NEG_INF, g)
        sel_scr[...] = sel_bias

    s = scores(i)
    key = lax.broadcasted_iota(jnp.int32, s.shape, 0)
    qry = lax.broadcasted_iota(jnp.int32, s.shape, 1)
    s = jnp.where(key <= qry, s, NEG_INF)
    m = jnp.max(s, axis=0, keepdims=True)
    p = jnp.exp(s - m)
    l = jnp.sum(p, axis=0, keepdims=True)
    acc = jnp.dot(vt_scr[i], p.astype(BF16), preferred_element_type=F32)

    def body(j, carry):
        m, l, acc = carry
        s = scores(j)
        if moba:
            s = s + sel_scr[pl.ds(j, 1), :]
        m_new = jnp.maximum(m, jnp.max(s, axis=0, keepdims=True))
        alpha = jnp.exp(m - m_new)
        p = jnp.exp(s - m_new)
        l = alpha * l + jnp.sum(p, axis=0, keepdims=True)
        acc = alpha * acc + jnp.dot(vt_scr[j], p.astype(BF16), preferred_element_type=F32)
        return m_new, l, acc

    m, l, acc = lax.fori_loop(0, i, body, (m, l, acc))
    o_ref[...] = (acc * (1.0 / l)).T.astype(o_ref.dtype)


def _attention(proj, c, *, moba, batch, seq, heads, q_col, k_col, v_col):
    tq = MOBA_BLOCK
    nq = seq // tq
    n_blocks = seq // MOBA_BLOCK
    in_specs = [
        pl.BlockSpec((tq, HEAD_DIM), lambda b, h, i: (b * nq + i, q_col + h)),
        pl.BlockSpec((seq, HEAD_DIM), lambda b, h, i: (b, k_col + h)),
        pl.BlockSpec((seq, HEAD_DIM), lambda b, h, i: (b, v_col + h)),
    ]
    args = [proj, proj, proj]
    scratch = [pltpu.VMEM((n_blocks, HEAD_DIM, MOBA_BLOCK), BF16)]
    if moba:
        scratch += [pltpu.VMEM((n_blocks, HEAD_DIM), F32), pltpu.VMEM((n_blocks, tq), F32)]
    else:
        in_specs.append(pl.BlockSpec((seq, LANES), lambda b, h, i: (b, 0)))
        args.append(c)
        scratch.append(pltpu.VMEM((n_blocks, MOBA_BLOCK, LANES), F32))
    return pl.pallas_call(
        functools.partial(_attn_kernel, moba=moba, n_blocks=n_blocks),
        grid=(batch, heads, nq),
        in_specs=in_specs,
        out_specs=pl.BlockSpec((tq, HEAD_DIM), lambda b, h, i: (b * nq + i, h)),
        out_shape=jax.ShapeDtypeStruct((batch * seq, heads * HEAD_DIM), BF16),
        scratch_shapes=scratch,
        compiler_params=_compiler_params(3),
        name="moba_attention" if moba else "fox_attention",
    )(*args)


def _rmsnorm(x, g):
    ms = jnp.mean(x * x, axis=-1, keepdims=True)
    return x * lax.rsqrt(ms + RMS_EPS) * g


def _out_merge_kernel(oa_ref, ob_ref, za_ref, zb_ref, ga_ref, gb_ref, x_ref, p_ref,
                      wa_ref, wb_ref, wo_ref, wg_ref, wu_ref, gple_ref, gfin_ref, out_ref,
                      *, final):
    def branch(o_ref, z_ref, w_ref):
        z = z_ref[...].astype(F32)
        a = o_ref[...].astype(F32) * (z * jax.nn.sigmoid(z))
        return jnp.dot(a.astype(BF16), w_ref[...], preferred_element_type=F32)

    ya = branch(oa_ref, za_ref, wa_ref)
    yb = branch(ob_ref, zb_ref, wb_ref)
    mixed = (jax.nn.sigmoid(ga_ref[...].astype(F32)) * ya
             + jax.nn.sigmoid(gb_ref[...].astype(F32)) * yb)
    x1 = x_ref[...] + jnp.dot(mixed.astype(BF16), wo_ref[...], preferred_element_type=F32)
    hn = _rmsnorm(x1, gple_ref[...]).astype(BF16)
    pg = jax.nn.sigmoid(jnp.dot(hn, wg_ref[...], preferred_element_type=F32))
    up = jnp.dot(p_ref[...].astype(BF16), wu_ref[...], preferred_element_type=F32)
    x2 = x1 + up * pg
    if final:
        x2 = _rmsnorm(x2, gfin_ref[...])
    out_ref[...] = x2


def _out_merge(oa, ob, proj, x2, p2, wa, wb, wo, wg, wu, g_ple, g_final, *, tm, final,
               za_col, zb_col, ga_col, gb_col):
    t, d = x2.shape
    wa_w = oa.shape[1]
    wb_w = ob.shape[1]
    ple = p2.shape[1]
    row = lambda i: (i, 0)
    const = lambda i: (0, 0)
    return pl.pallas_call(
        functools.partial(_out_merge_kernel, final=final),
        grid=(t // tm,),
        in_specs=[
            pl.BlockSpec((tm, wa_w), row),
            pl.BlockSpec((tm, wb_w), row),
            pl.BlockSpec((tm, wa_w), lambda i: (i, za_col)),
            pl.BlockSpec((tm, wb_w), lambda i: (i, zb_col)),
            pl.BlockSpec((tm, d), lambda i: (i, ga_col)),
            pl.BlockSpec((tm, d), lambda i: (i, gb_col)),
            pl.BlockSpec((tm, d), row),
            pl.BlockSpec((tm, ple), row),
            _resident((wa_w, d), const),
            _resident((wb_w, d), const),
            _resident((d, d), const),
            _resident((d, d), const),
            _resident((ple, d), const),
            _resident((1, d), const),
            _resident((1, d), const),
        ],
        out_specs=pl.BlockSpec((tm, d), row),
        out_shape=jax.ShapeDtypeStruct((t, d), F32),
        compiler_params=_compiler_params(1),
        name="out_merge",
    )(oa, ob, proj, proj, proj, proj, x2, p2, wa, wb, wo, wg, wu, g_ple, g_final)


def kernel(x, p, positions, g_norm, w_in, b_f, w_branch_a, w_branch_b, w_out,
           g_ple, w_ple_gate, w_ple_up, g_final):
    batch, seq, d = x.shape
    depth = w_in.shape[0]
    t = batch * seq
    wa_w = w_branch_a.shape[1]
    wb_w = w_branch_b.shape[1]
    heads_a = wa_w // HEAD_DIM
    heads_b = wb_w // HEAD_DIM
    n_f = b_f.shape[1]
    assert seq % MOBA_BLOCK == 0 and wa_w == wb_w and 2 * wa_w == d
    f_start = 4 * wa_w + 3 * wb_w
    assert w_in.shape[2] == f_start + n_f + wb_w + 2 * d

    inv = ROPE_THETA ** (-jnp.arange(0, HEAD_DIM, 2, dtype=F32) / HEAD_DIM)
    inv_full = jnp.concatenate([inv, inv]).reshape(1, HEAD_DIM)
    pos = positions.reshape(t, 1)
    x2 = x.reshape(t, d)
    hb = HEAD_DIM

    for layer in range(depth):
        w = w_in[layer]
        w_main = jnp.concatenate([w[:, :f_start], w[:, f_start + n_f:]], axis=1).astype(BF16)
        w_f = jnp.pad(w[:, f_start:f_start + n_f], ((0, 0), (0, LANES - n_f))).astype(BF16)
        b_pad = jnp.pad(b_f[layer], (0, LANES - n_f)).reshape(1, LANES)

        proj, f = _in_proj(x2, pos, g_norm[layer].reshape(1, d), inv_full, w_main, w_f,
                           tm=1024, tn=wa_w, width_a=wa_w)
        c = _fox_gate(f, b_pad, batch=batch, seq=seq)
        oa = _attention(proj, None, moba=True, batch=batch, seq=seq, heads=heads_a,
                        q_col=0, k_col=wa_w // hb, v_col=2 * wa_w // hb)
        ob = _attention(proj, c, moba=False, batch=batch, seq=seq, heads=heads_b,
                        q_col=4 * wa_w // hb, k_col=(4 * wa_w + wb_w) // hb,
                        v_col=(4 * wa_w + 2 * wb_w) // hb)
        x2 = _out_merge(
            oa, ob, proj, x2, p[layer].reshape(t, -1),
            w_branch_a[layer].astype(BF16), w_branch_b[layer].astype(BF16),
            w_out[layer].astype(BF16), w_ple_gate[layer].astype(BF16),
            w_ple_up[layer].astype(BF16), g_ple[layer].reshape(1, d), g_final.reshape(1, d),
            tm=256, final=(layer == depth - 1),
            za_col=3, zb_col=(4 * wa_w + 3 * wb_w) // wb_w,
            ga_col=(4 * wa_w + 4 * wb_w) // d, gb_col=(4 * wa_w + 4 * wb_w) // d + 1)
    return x2.reshape(batch, seq, d)
```

```python
import functools
import math

import jax
import jax.numpy as jnp
from jax import lax
from jax.experimental import pallas as pl
from jax.experimental.pallas import tpu as pltpu

HEAD_DIM = 128
MOBA_BLOCK = 256
MOBA_TOPK = 3
ROPE_THETA = 10000.0
RMS_EPS = 1e-6

V7X_VMEM_BYTES = 64 * 1024 * 1024
VMEM_LIMIT_BYTES = V7X_VMEM_BYTES * 7 // 8
LANES = 128

F32 = jnp.float32
BF16 = jnp.bfloat16
NEG_INF = float("-inf")
NT_DIMS = (((1,), (1,)), ((), ()))


def _compiler_params(n_grid_dims):
    return pltpu.CompilerParams(
        dimension_semantics=("arbitrary",) * n_grid_dims,
        vmem_limit_bytes=VMEM_LIMIT_BYTES,
    )


def _resident(block_shape, index_map):
    return pl.BlockSpec(block_shape, index_map, pipeline_mode=pl.Buffered(1))


def _in_proj_kernel(x_ref, pos_ref, g_ref, inv_ref, w_ref, wf_ref, proj_ref, f_ref,
                    h_scr, cos_scr, sin_scr, *, qa_tile, ka_tile, qb_tile, q_scale):
    n = pl.program_id(1)

    @pl.when(n == 0)
    def _normalize_and_tables():
        x = x_ref[...]
        ms = jnp.mean(x * x, axis=-1, keepdims=True)
        h = (x * lax.rsqrt(ms + RMS_EPS) * g_ref[...]).astype(BF16)
        h_scr[...] = h
        f_ref[...] = jnp.dot(h, wf_ref[...], preferred_element_type=F32)
        ang = pos_ref[...].astype(F32) * inv_ref[...]
        lane = lax.broadcasted_iota(jnp.int32, ang.shape, 1)
        sin = jnp.sin(ang)
        cos_scr[...] = jnp.cos(ang)
        sin_scr[...] = jnp.where(lane < HEAD_DIM // 2, -sin, sin)

    acc = jnp.dot(h_scr[...], w_ref[...], preferred_element_type=F32)
    is_rope = jnp.logical_or(n == qa_tile, n == ka_tile)

    @pl.when(is_rope)
    def _rotary():
        scale = jnp.where(n == qa_tile, q_scale, 1.0).astype(F32)
        cos = cos_scr[...]
        sin = sin_scr[...]
        for j in range(acc.shape[1] // HEAD_DIM):
            sl = slice(j * HEAD_DIM, (j + 1) * HEAD_DIM)
            a = acc[:, sl]
            r = a * cos + pltpu.roll(a, HEAD_DIM // 2, 1) * sin
            proj_ref[:, sl] = (r * scale).astype(BF16)

    @pl.when(n == qb_tile)
    def _scaled():
        proj_ref[...] = (acc * q_scale).astype(BF16)

    @pl.when(jnp.logical_not(jnp.logical_or(is_rope, n == qb_tile)))
    def _plain():
        proj_ref[...] = acc.astype(BF16)


def _in_proj(x2, pos, g, inv_full, w_main, w_f, *, tm, tn, width_a):
    t, d = x2.shape
    n_main = w_main.shape[1]
    assert width_a == tn, "one column tile per attention section"
    kern = functools.partial(
        _in_proj_kernel, qa_tile=0, ka_tile=1, qb_tile=4, q_scale=1.0 / math.sqrt(HEAD_DIM))
    return pl.pallas_call(
        kern,
        grid=(t // tm, n_main // tn),
        in_specs=[
            pl.BlockSpec((tm, d), lambda i, n: (i, 0)),
            pl.BlockSpec((tm, 1), lambda i, n: (i, 0)),
            _resident((1, d), lambda i, n: (0, 0)),
            _resident((1, LANES), lambda i, n: (0, 0)),
            pl.BlockSpec((d, tn), lambda i, n: (0, n)),
            _resident((d, LANES), lambda i, n: (0, 0)),
        ],
        out_specs=[
            pl.BlockSpec((tm, tn), lambda i, n: (i, n)),
            pl.BlockSpec((tm, LANES), lambda i, n: (i, 0)),
        ],
        out_shape=[
            jax.ShapeDtypeStruct((t, n_main), BF16),
            jax.ShapeDtypeStruct((t, LANES), F32),
        ],
        scratch_shapes=[
            pltpu.VMEM((tm, d), BF16),
            pltpu.VMEM((tm, LANES), F32),
            pltpu.VMEM((tm, LANES), F32),
        ],
        compiler_params=_compiler_params(2),
        name="in_proj",
    )(x2, pos, g, inv_full, w_main, w_f)


def _fox_gate_kernel(f_ref, b_ref, c_ref, *, chunk):
    n_chunks = f_ref.shape[0] // chunk
    r = lax.broadcasted_iota(jnp.int32, (chunk, chunk), 0)
    c = lax.broadcasted_iota(jnp.int32, (chunk, chunk), 1)
    tri = (c <= r).astype(F32)

    def body(j, carry):
        off = pl.multiple_of(j * chunk, chunk)
        z = f_ref[pl.ds(off, chunk), :] + b_ref[...]
        logf = jnp.minimum(z, 0.0) - jnp.log1p(jnp.exp(-jnp.abs(z)))
        csum = jnp.dot(tri, logf, precision=lax.Precision.HIGHEST,
                       preferred_element_type=F32) + carry
        c_ref[pl.ds(off, chunk), :] = csum
        return csum[chunk - 1:chunk, :]

    lax.fori_loop(0, n_chunks, body, jnp.zeros((1, LANES), F32))


def _fox_gate(f, b_pad, *, batch, seq):
    return pl.pallas_call(
        functools.partial(_fox_gate_kernel, chunk=MOBA_BLOCK),
        grid=(batch,),
        in_specs=[
            pl.BlockSpec((seq, LANES), lambda b: (b, 0)),
            _resident((1, LANES), lambda b: (0, 0)),
        ],
        out_specs=pl.BlockSpec((seq, LANES), lambda b: (b, 0)),
        out_shape=jax.ShapeDtypeStruct(f.shape, F32),
        compiler_params=_compiler_params(1),
        name="fox_gate",
    )(f, b_pad)


ATTN_GROUP = 4


def _attn_kernel(*refs, moba, n_blocks, group):
    if moba:
        q_ref, k_ref, v_ref, o_ref, vt_scr, kbar_scr, sel_scr = refs
    else:
        q_ref, k_ref, v_ref, c_ref, o_ref, vt_scr, crep_scr = refs
    blk = MOBA_BLOCK
    pair = 2 * blk
    hg = pl.program_id(1)
    i = pl.program_id(2)
    tq = q_ref.shape[0]
    head_cols = [slice(g * HEAD_DIM, (g + 1) * HEAD_DIM) for g in range(group)]

    @pl.when(i == 0)
    def _per_group_setup():
        def body(a, carry):
            off = pl.multiple_of(a * pair, pair)
            for g, cols in enumerate(head_cols):
                vt_scr[g, a] = v_ref[pl.ds(off, pair), cols].astype(F32).T.astype(BF16)
                if moba:
                    kb = k_ref[pl.ds(off, pair), cols].astype(F32)
                    kbar_scr[g, pl.ds(2 * a, 1), :] = jnp.mean(kb[:blk], axis=0, keepdims=True)
                    kbar_scr[g, pl.ds(2 * a + 1, 1), :] = jnp.mean(kb[blk:], axis=0,
                                                                   keepdims=True)
                else:
                    c = c_ref[pl.ds(off, pair), :]
                    lane = lax.broadcasted_iota(jnp.int32, c.shape, 1)
                    col = jnp.sum(jnp.where(lane == hg * group + g, c, 0.0), axis=1,
                                  keepdims=True)
                    crep_scr[g, a] = jnp.broadcast_to(col, (pair, LANES))
            return carry
        lax.fori_loop(0, n_blocks // 2, body, 0)

    qs = [q_ref[:, cols] for cols in head_cols]

    def scores(g, a):
        off = pl.multiple_of(a * pair, pair)
        kb = k_ref[pl.ds(off, pair), head_cols[g]]
        s = lax.dot_general(kb, qs[g], NT_DIMS, preferred_element_type=F32)
        if not moba:
            c = crep_scr[g, a]
            s = s - jnp.concatenate([c] * (tq // LANES), axis=1)
        return s

    def fold_all(a, ss, states):
        ms, ps = [], []
        for g, s in enumerate(ss):
            m_new = jnp.max(s, axis=0, keepdims=True)
            if states is not None:
                m_new = jnp.maximum(states[g][0], m_new)
            ms.append(m_new)
            ps.append(jnp.exp(s - m_new))
        out = []
        for g, p in enumerate(ps):
            l_new = jnp.sum(p, axis=0, keepdims=True)
            acc_new = jnp.dot(vt_scr[g, a], p.astype(BF16),
                              preferred_element_type=F32)
            if states is not None:
                m, l, acc = states[g]
                alpha = jnp.exp(m - ms[g])
                l_new = alpha * l + l_new
                acc_new = alpha * acc + acc_new
            out.append((ms[g], l_new, acc_new))
        return tuple(out)

    if moba:
        for g in range(group):
            gs = lax.dot_general(kbar_scr[g], qs[g].astype(F32), NT_DIMS,
                                 precision=lax.Precision.HIGHEST,
                                 preferred_element_type=F32)
            blk_id = lax.broadcasted_iota(jnp.int32, gs.shape, 0)
            gs = jnp.where(blk_id < i, gs, NEG_INF)
            sel_bias = jnp.full(gs.shape, NEG_INF, F32)
            for _ in range(MOBA_TOPK):
                mx = jnp.max(gs, axis=0, keepdims=True)
                first = jnp.min(jnp.where(gs == mx, blk_id, n_blocks), axis=0, keepdims=True)
                pick = jnp.logical_and(blk_id == first, mx > NEG_INF)
                sel_bias = jnp.where(pick, 0.0, sel_bias)
                gs = jnp.where(pick, NEG_INF, gs)
            sel_scr[g] = sel_bias

    own = lax.shift_right_logical(i, 1)
    key = own * pair + lax.broadcasted_iota(jnp.int32, (pair, tq), 0)
    qry = i * blk + lax.broadcasted_iota(jnp.int32, (pair, tq), 1)
    visible = key <= qry
    ss = []
    for g in range(group):
        s = scores(g, own)
        if moba:
            first_is_past = jnp.full((1, tq), 2 * own, jnp.int32) < i
            bias = jnp.where(first_is_past, sel_scr[g, pl.ds(2 * own, 1), :], 0.0)
            s = jnp.concatenate([s[:blk] + bias, s[blk:]], axis=0)
        ss.append(jnp.where(visible, s, NEG_INF))

    def past_scores(a):
        ss = []
        for g in range(group):
            s = scores(g, a)
            if moba:
                s = jnp.concatenate([s[:blk] + sel_scr[g, pl.ds(2 * a, 1), :],
                                     s[blk:] + sel_scr[g, pl.ds(2 * a + 1, 1), :]], axis=0)
            ss.append(s)
        return tuple(ss)

    states = fold_all(own, ss, None)
    states = lax.fori_loop(0, own, lambda a, st: fold_all(a, past_scores(a), st), states)
    for g, cols in enumerate(head_cols):
        _, l, acc = states[g]
        o_ref[:, cols] = (acc * (1.0 / l)).T.astype(o_ref.dtype)


def _attention(proj, c, *, moba, batch, seq, heads, q_col, k_col, v_col):
    group = ATTN_GROUP
    tq = MOBA_BLOCK
    nq = seq // tq
    n_blocks = seq // MOBA_BLOCK
    width = group * HEAD_DIM
    assert heads % group == 0 and n_blocks % 2 == 0
    assert q_col % group == 0 and k_col % group == 0 and v_col % group == 0
    in_specs = [
        pl.BlockSpec((tq, width), lambda b, h, i: (b * nq + i, q_col // group + h)),
        pl.BlockSpec((seq, width), lambda b, h, i: (b, k_col // group + h)),
        pl.BlockSpec((seq, width), lambda b, h, i: (b, v_col // group + h)),
    ]
    args = [proj, proj, proj]
    scratch = [pltpu.VMEM((group, n_blocks // 2, HEAD_DIM, 2 * MOBA_BLOCK), BF16)]
    if moba:
        scratch += [pltpu.VMEM((group, n_blocks, HEAD_DIM), F32),
                    pltpu.VMEM((group, n_blocks, tq), F32)]
    else:
        in_specs.append(pl.BlockSpec((seq, LANES), lambda b, h, i: (b, 0)))
        args.append(c)
        scratch.append(pltpu.VMEM((group, n_blocks // 2, 2 * MOBA_BLOCK, LANES), F32))
    return pl.pallas_call(
        functools.partial(_attn_kernel, moba=moba, n_blocks=n_blocks, group=group),
        grid=(batch, heads // group, nq),
        in_specs=in_specs,
        out_specs=pl.BlockSpec((tq, width), lambda b, h, i: (b * nq + i, h)),
        out_shape=jax.ShapeDtypeStruct((batch * seq, heads * HEAD_DIM), BF16),
        scratch_shapes=scratch,
        compiler_params=_compiler_params(3),
        name="moba_attention" if moba else "fox_attention",
    )(*args)


def _rmsnorm(x, g):
    ms = jnp.mean(x * x, axis=-1, keepdims=True)
    return x * lax.rsqrt(ms + RMS_EPS) * g


def _out_merge_kernel(oa_ref, ob_ref, za_ref, zb_ref, ga_ref, gb_ref, x_ref, p_ref,
                      wa_ref, wb_ref, wo_ref, wg_ref, wu_ref, gple_ref, gfin_ref, out_ref,
                      *, final):
    def branch(o_ref, z_ref, w_ref):
        z = z_ref[...].astype(F32)
        a = o_ref[...].astype(F32) * (z * jax.nn.sigmoid(z))
        return jnp.dot(a.astype(BF16), w_ref[...], preferred_element_type=F32)

    ya = branch(oa_ref, za_ref, wa_ref)
    yb = branch(ob_ref, zb_ref, wb_ref)
    mixed = (jax.nn.sigmoid(ga_ref[...].astype(F32)) * ya
             + jax.nn.sigmoid(gb_ref[...].astype(F32)) * yb)
    x1 = x_ref[...] + jnp.dot(mixed.astype(BF16), wo_ref[...], preferred_element_type=F32)
    hn = _rmsnorm(x1, gple_ref[...]).astype(BF16)
    pg = jax.nn.sigmoid(jnp.dot(hn, wg_ref[...], preferred_element_type=F32))
    up = jnp.dot(p_ref[...].astype(BF16), wu_ref[...], preferred_element_type=F32)
    x2 = x1 + up * pg
    if final:
        x2 = _rmsnorm(x2, gfin_ref[...])
    out_ref[...] = x2


def _out_merge(oa, ob, proj, x2, p2, wa, wb, wo, wg, wu, g_ple, g_final, *, tm, final,
               za_col, zb_col, ga_col, gb_col):
    t, d = x2.shape
    wa_w = oa.shape[1]
    wb_w = ob.shape[1]
    ple = p2.shape[1]
    row = lambda i: (i, 0)
    const = lambda i: (0, 0)
    return pl.pallas_call(
        functools.partial(_out_merge_kernel, final=final),
        grid=(t // tm,),
        in_specs=[
            pl.BlockSpec((tm, wa_w), row),
            pl.BlockSpec((tm, wb_w), row),
            pl.BlockSpec((tm, wa_w), lambda i: (i, za_col)),
            pl.BlockSpec((tm, wb_w), lambda i: (i, zb_col)),
            pl.BlockSpec((tm, d), lambda i: (i, ga_col)),
            pl.BlockSpec((tm, d), lambda i: (i, gb_col)),
            pl.BlockSpec((tm, d), row),
            pl.BlockSpec((tm, ple), row),
            _resident((wa_w, d), const),
            _resident((wb_w, d), const),
            _resident((d, d), const),
            _resident((d, d), const),
            _resident((ple, d), const),
            _resident((1, d), const),
            _resident((1, d), const),
        ],
        out_specs=pl.BlockSpec((tm, d), row),
        out_shape=jax.ShapeDtypeStruct((t, d), F32),
        compiler_params=_compiler_params(1),
        name="out_merge",
    )(oa, ob, proj, proj, proj, proj, x2, p2, wa, wb, wo, wg, wu, g_ple, g_final)


def kernel(x, p, positions, g_norm, w_in, b_f, w_branch_a, w_branch_b, w_out,
           g_ple, w_ple_gate, w_ple_up, g_final):
    batch, seq, d = x.shape
    depth = w_in.shape[0]
    t = batch * seq
    wa_w = w_branch_a.shape[1]
    wb_w = w_branch_b.shape[1]
    heads_a = wa_w // HEAD_DIM
    heads_b = wb_w // HEAD_DIM
    n_f = b_f.shape[1]
    assert seq % MOBA_BLOCK == 0 and wa_w == wb_w and 2 * wa_w == d
    f_start = 4 * wa_w + 3 * wb_w
    assert w_in.shape[2] == f_start + n_f + wb_w + 2 * d

    inv = ROPE_THETA ** (-jnp.arange(0, HEAD_DIM, 2, dtype=F32) / HEAD_DIM)
    inv_full = jnp.concatenate([inv, inv]).reshape(1, HEAD_DIM)
    pos = positions.reshape(t, 1)
    x2 = x.reshape(t, d)
    hb = HEAD_DIM

    for layer in range(depth):
        w = w_in[layer]
        w_main = jnp.concatenate([w[:, :f_start], w[:, f_start + n_f:]], axis=1).astype(BF16)
        w_f = jnp.pad(w[:, f_start:f_start + n_f], ((0, 0), (0, LANES - n_f))).astype(BF16)
        b_pad = jnp.pad(b_f[layer], (0, LANES - n_f)).reshape(1, LANES)

        proj, f = _in_proj(x2, pos, g_norm[layer].reshape(1, d), inv_full, w_main, w_f,
                           tm=1024, tn=wa_w, width_a=wa_w)
        c = _fox_gate(f, b_pad, batch=batch, seq=seq)
        oa = _attention(proj, None, moba=True, batch=batch, seq=seq, heads=heads_a,
                        q_col=0, k_col=wa_w // hb, v_col=2 * wa_w // hb)
        ob = _attention(proj, c, moba=False, batch=batch, seq=seq, heads=heads_b,
                        q_col=4 * wa_w // hb, k_col=(4 * wa_w + wb_w) // hb,
                        v_col=(4 * wa_w + 2 * wb_w) // hb)
        x2 = _out_merge(
            oa, ob, proj, x2, p[layer].reshape(t, -1),
            w_branch_a[layer].astype(BF16), w_branch_b[layer].astype(BF16),
            w_out[layer].astype(BF16), w_ple_gate[layer].astype(BF16),
            w_ple_up[layer].astype(BF16), g_ple[layer].reshape(1, d), g_final.reshape(1, d),
            tm=256, final=(layer == depth - 1),
            za_col=3, zb_col=(4 * wa_w + 3 * wb_w) // wb_w,
            ga_col=(4 * wa_w + 4 * wb_w) // d, gb_col=(4 * wa_w + 4 * wb_w) // d + 1)
    return x2.reshape(batch, seq, d)
```

```python
import functools
import math

import jax
import jax.numpy as jnp
from jax import lax
from jax.experimental import pallas as pl
from jax.experimental.pallas import tpu as pltpu

HEAD_DIM = 128
MOBA_BLOCK = 256
MOBA_TOPK = 3
ROPE_THETA = 10000.0
RMS_EPS = 1e-6

V7X_VMEM_BYTES = 64 * 1024 * 1024
VMEM_LIMIT_BYTES = V7X_VMEM_BYTES * 7 // 8
LANES = 128
BF16_SUBLANES = 16

F32 = jnp.float32
BF16 = jnp.bfloat16
NEG_INF = float("-inf")
MASKED = -1e30
LOG2E = math.log2(math.e)
NT_DIMS = (((1,), (1,)), ((), ()))


def _compiler_params(n_grid_dims):
    return pltpu.CompilerParams(
        dimension_semantics=("arbitrary",) * n_grid_dims,
        vmem_limit_bytes=VMEM_LIMIT_BYTES,
    )


def _resident(block_shape, index_map):
    return pl.BlockSpec(block_shape, index_map, pipeline_mode=pl.Buffered(1))


def _in_proj_kernel(x_ref, pos_ref, g_ref, inv_ref, w_ref, wf_ref, proj_ref, f_ref,
                    h_scr, cos_scr, sin_scr, *, qa_tile, ka_tile, qb_tile, q_scale):
    n = pl.program_id(1)

    @pl.when(n == 0)
    def _normalize_and_tables():
        x = x_ref[...]
        ms = jnp.mean(x * x, axis=-1, keepdims=True)
        h = (x * lax.rsqrt(ms + RMS_EPS) * g_ref[...]).astype(BF16)
        h_scr[...] = h
        f_ref[...] = jnp.dot(h, wf_ref[...], preferred_element_type=F32)
        ang = pos_ref[...].astype(F32) * inv_ref[...]
        lane = lax.broadcasted_iota(jnp.int32, ang.shape, 1)
        sin = jnp.sin(ang)
        cos_scr[...] = jnp.cos(ang)
        sin_scr[...] = jnp.where(lane < HEAD_DIM // 2, -sin, sin)

    acc = jnp.dot(h_scr[...], w_ref[...], preferred_element_type=F32)
    is_rope = jnp.logical_or(n == qa_tile, n == ka_tile)

    @pl.when(is_rope)
    def _rotary():
        scale = jnp.where(n == qa_tile, q_scale, 1.0).astype(F32)
        cos = cos_scr[...]
        sin = sin_scr[...]
        for j in range(acc.shape[1] // HEAD_DIM):
            sl = slice(j * HEAD_DIM, (j + 1) * HEAD_DIM)
            a = acc[:, sl]
            r = a * cos + pltpu.roll(a, HEAD_DIM // 2, 1) * sin
            proj_ref[:, sl] = (r * scale).astype(BF16)

    @pl.when(n == qb_tile)
    def _scaled():
        proj_ref[...] = (acc * q_scale).astype(BF16)

    @pl.when(jnp.logical_not(jnp.logical_or(is_rope, n == qb_tile)))
    def _plain():
        proj_ref[...] = acc.astype(BF16)


def _in_proj(x2, pos, g, inv_full, w_main, w_f, *, tm, tn, width_a):
    t, d = x2.shape
    n_main = w_main.shape[1]
    assert width_a == tn, "one column tile per attention section"
    kern = functools.partial(
        _in_proj_kernel, qa_tile=0, ka_tile=1, qb_tile=4, q_scale=LOG2E / math.sqrt(HEAD_DIM))
    return pl.pallas_call(
        kern,
        grid=(t // tm, n_main // tn),
        in_specs=[
            pl.BlockSpec((tm, d), lambda i, n: (i, 0)),
            pl.BlockSpec((tm, 1), lambda i, n: (i, 0)),
            _resident((1, d), lambda i, n: (0, 0)),
            _resident((1, LANES), lambda i, n: (0, 0)),
            pl.BlockSpec((d, tn), lambda i, n: (0, n)),
            _resident((d, LANES), lambda i, n: (0, 0)),
        ],
        out_specs=[
            pl.BlockSpec((tm, tn), lambda i, n: (i, n)),
            pl.BlockSpec((tm, LANES), lambda i, n: (i, 0)),
        ],
        out_shape=[
            jax.ShapeDtypeStruct((t, n_main), BF16),
            jax.ShapeDtypeStruct((t, LANES), F32),
        ],
        scratch_shapes=[
            pltpu.VMEM((tm, d), BF16),
            pltpu.VMEM((tm, LANES), F32),
            pltpu.VMEM((tm, LANES), F32),
        ],
        compiler_params=_compiler_params(2),
        name="in_proj",
    )(x2, pos, g, inv_full, w_main, w_f)


def _fox_gate_kernel(f_ref, b_ref, c_ref, *, chunk):
    n_chunks = f_ref.shape[0] // chunk
    r = lax.broadcasted_iota(jnp.int32, (chunk, chunk), 0)
    c = lax.broadcasted_iota(jnp.int32, (chunk, chunk), 1)
    tri = (c <= r).astype(F32)

    def body(j, carry):
        off = pl.multiple_of(j * chunk, chunk)
        z = f_ref[pl.ds(off, chunk), :] + b_ref[...]
        logf = (jnp.minimum(z, 0.0) - jnp.log1p(jnp.exp(-jnp.abs(z)))) * LOG2E
        csum = jnp.dot(tri, logf, precision=lax.Precision.HIGHEST,
                       preferred_element_type=F32) + carry
        c_ref[pl.ds(off, chunk), :] = csum
        return csum[chunk - 1:chunk, :]

    lax.fori_loop(0, n_chunks, body, jnp.zeros((1, LANES), F32))


def _fox_gate(f, b_pad, *, batch, seq):
    return pl.pallas_call(
        functools.partial(_fox_gate_kernel, chunk=MOBA_BLOCK),
        grid=(batch,),
        in_specs=[
            pl.BlockSpec((seq, LANES), lambda b: (b, 0)),
            _resident((1, LANES), lambda b: (0, 0)),
        ],
        out_specs=pl.BlockSpec((seq, LANES), lambda b: (b, 0)),
        out_shape=jax.ShapeDtypeStruct(f.shape, F32),
        compiler_params=_compiler_params(1),
        name="fox_gate",
    )(f, b_pad)


ATTN_GROUP = 4
VT_ROWS = HEAD_DIM + BF16_SUBLANES


def _attn_kernel(*refs, moba, n_blocks, group):
    if moba:
        q_ref, k_ref, v_ref, o_ref, vt_scr, kaug_scr, qaug_scr, s_scr, m_scr, acc_scr, kbar_scr = refs
    else:
        q_ref, k_ref, v_ref, c_ref, o_ref, vt_scr, kaug_scr, qaug_scr, s_scr, m_scr, acc_scr = refs
    blk = MOBA_BLOCK
    pair = 2 * blk
    hg = pl.program_id(1)
    i = pl.program_id(2)
    tq = q_ref.shape[0]
    head_cols = [slice(g * HEAD_DIM, (g + 1) * HEAD_DIM) for g in range(group)]
    extra = slice(HEAD_DIM, 2 * HEAD_DIM)

    @pl.when(i == 0)
    def _per_group_setup():
        ones_row = (lax.broadcasted_iota(jnp.int32, (BF16_SUBLANES, pair), 0) == 0).astype(BF16)
        lane = lax.broadcasted_iota(jnp.int32, (pair, LANES), 1)
        row = lax.broadcasted_iota(jnp.int32, (pair, LANES), 0)

        def body(a, carry):
            off = pl.multiple_of(a * pair, pair)
            for g, cols in enumerate(head_cols):
                vt_scr[g, a, :HEAD_DIM, :] = (
                    v_ref[pl.ds(off, pair), cols].astype(F32).T.astype(BF16))
                vt_scr[g, a, HEAD_DIM:, :] = ones_row
                kb = k_ref[pl.ds(off, pair), cols]
                kaug_scr[g, pl.ds(off, pair), :HEAD_DIM] = kb
                if moba:
                    kf = kb.astype(F32)
                    kbar_scr[g, pl.ds(2 * a, 1), :] = jnp.mean(kf[:blk], axis=0, keepdims=True)
                    kbar_scr[g, pl.ds(2 * a + 1, 1), :] = jnp.mean(kf[blk:], axis=0,
                                                                   keepdims=True)
                    block_of_row = 2 * a + (row >= blk).astype(jnp.int32)
                    feats = jnp.where(lane == block_of_row, 1.0, 0.0)
                else:
                    c = c_ref[pl.ds(off, pair), :]
                    col = jnp.sum(jnp.where(lane == hg * group + g, c, 0.0), axis=1,
                                  keepdims=True)
                    hi = col.astype(BF16).astype(F32)
                    mid = (col - hi).astype(BF16).astype(F32)
                    lo = col - hi - mid
                    feats = jnp.where(lane == 0, hi,
                                      jnp.where(lane == 1, mid, jnp.where(lane == 2, lo, 0.0)))
                kaug_scr[g, pl.ds(off, pair), extra] = feats.astype(BF16)
            return carry
        lax.fori_loop(0, n_blocks // 2, body, 0)

    for g, cols in enumerate(head_cols):
        q = q_ref[:, cols]
        qaug_scr[g, :, :HEAD_DIM] = q
        if moba:
            gs = lax.dot_general(kbar_scr[g], q.astype(F32), NT_DIMS,
                                 precision=lax.Precision.HIGHEST,
                                 preferred_element_type=F32)
            blk_id = lax.broadcasted_iota(jnp.int32, gs.shape, 0)
            gs = jnp.where(blk_id < i, gs, NEG_INF)
            bias = jnp.where(blk_id == i, 0.0, MASKED)
            for _ in range(MOBA_TOPK):
                mx = jnp.max(gs, axis=0, keepdims=True)
                first = jnp.min(jnp.where(gs == mx, blk_id, n_blocks), axis=0, keepdims=True)
                pick = jnp.logical_and(blk_id == first, mx > NEG_INF)
                bias = jnp.where(pick, 0.0, bias)
                gs = jnp.where(pick, NEG_INF, gs)
            bias = jnp.concatenate([bias, jnp.zeros((LANES - n_blocks, tq), F32)], axis=0)
            qaug_scr[g, :, extra] = bias.T.astype(BF16)
        else:
            lane = lax.broadcasted_iota(jnp.int32, (tq, LANES), 1)
            qaug_scr[g, :, extra] = jnp.where(lane < 3, -1.0, 0.0).astype(BF16)

    def scores(g, a):
        off = pl.multiple_of(a * pair, pair)
        return lax.dot_general(kaug_scr[g, pl.ds(off, pair), :], qaug_scr[g], NT_DIMS,
                               preferred_element_type=F32)

    def store_scores(slot, a):
        for g in range(group):
            s_scr[slot, g] = scores(g, a)

    def fold(a, ss, first):
        ms, ps = [], []
        for g, s in enumerate(ss):
            m_new = jnp.max(s, axis=0, keepdims=True)
            if not first:
                m_new = jnp.maximum(m_scr[g], m_new)
            ms.append(m_new)
            ps.append(jnp.exp2(s - m_new).astype(BF16))
        for g, p in enumerate(ps):
            pv = jnp.dot(vt_scr[g, a], p, preferred_element_type=F32)
            if not first:
                pv = jnp.exp2(m_scr[g] - ms[g]) * acc_scr[g] + pv
            acc_scr[g] = pv
            m_scr[g] = ms[g]

    own = lax.shift_right_logical(i, 1)
    key = own * pair + lax.broadcasted_iota(jnp.int32, (pair, tq), 0)
    qry = i * blk + lax.broadcasted_iota(jnp.int32, (pair, tq), 1)
    visible = key <= qry
    ss_own = [jnp.where(visible, scores(g, own), NEG_INF) for g in range(group)]

    store_scores(0, 0)
    fold(own, ss_own, first=True)

    def two_pairs(t, carry):
        a0 = 2 * t
        store_scores(1, a0 + 1)
        fold(a0, [s_scr[0, g] for g in range(group)], first=False)
        store_scores(0, jnp.minimum(a0 + 2, own - 1))
        fold(a0 + 1, [s_scr[1, g] for g in range(group)], first=False)
        return carry

    lax.fori_loop(0, lax.shift_right_logical(own, 1), two_pairs, 0)

    @pl.when(jnp.bitwise_and(own, 1) == 1)
    def _last_odd_pair():
        fold(own - 1, [s_scr[0, g] for g in range(group)], first=False)

    for g, cols in enumerate(head_cols):
        acc = acc_scr[g]
        o = acc[:HEAD_DIM] * (1.0 / acc[HEAD_DIM:HEAD_DIM + 1])
        o_ref[:, cols] = o.T.astype(o_ref.dtype)


def _attention(proj, c, *, moba, batch, seq, heads, q_col, k_col, v_col):
    group = ATTN_GROUP
    tq = MOBA_BLOCK
    nq = seq // tq
    n_blocks = seq // MOBA_BLOCK
    width = group * HEAD_DIM
    pair = 2 * MOBA_BLOCK
    assert heads % group == 0 and n_blocks % 2 == 0 and n_blocks <= LANES
    assert q_col % group == 0 and k_col % group == 0 and v_col % group == 0
    in_specs = [
        pl.BlockSpec((tq, width), lambda b, h, i: (b * nq + i, q_col // group + h)),
        pl.BlockSpec((seq, width), lambda b, h, i: (b, k_col // group + h)),
        pl.BlockSpec((seq, width), lambda b, h, i: (b, v_col // group + h)),
    ]
    args = [proj, proj, proj]
    if not moba:
        in_specs.append(pl.BlockSpec((seq, LANES), lambda b, h, i: (b, 0)))
        args.append(c)
    scratch = [
        pltpu.VMEM((group, n_blocks // 2, VT_ROWS, pair), BF16),
        pltpu.VMEM((group, seq, 2 * HEAD_DIM), BF16),
        pltpu.VMEM((group, tq, 2 * HEAD_DIM), BF16),
        pltpu.VMEM((2, group, pair, tq), F32),
        pltpu.VMEM((group, 1, tq), F32),
        pltpu.VMEM((group, VT_ROWS, tq), F32),
    ]
    if moba:
        scratch.append(pltpu.VMEM((group, n_blocks, HEAD_DIM), F32))
    return pl.pallas_call(
        functools.partial(_attn_kernel, moba=moba, n_blocks=n_blocks, group=group),
        grid=(batch, heads // group, nq),
        in_specs=in_specs,
        out_specs=pl.BlockSpec((tq, width), lambda b, h, i: (b * nq + i, h)),
        out_shape=jax.ShapeDtypeStruct((batch * seq, heads * HEAD_DIM), BF16),
        scratch_shapes=scratch,
        compiler_params=_compiler_params(3),
        name="moba_attention" if moba else "fox_attention",
    )(*args)


def _rmsnorm(x, g):
    ms = jnp.mean(x * x, axis=-1, keepdims=True)
    return x * lax.rsqrt(ms + RMS_EPS) * g


def _out_merge_kernel(oa_ref, ob_ref, za_ref, zb_ref, ga_ref, gb_ref, x_ref, p_ref,
                      wa_ref, wb_ref, wo_ref, wg_ref, wu_ref, gple_ref, gfin_ref, out_ref,
                      *, final):
    def branch(o_ref, z_ref, w_ref):
        z = z_ref[...].astype(F32)
        a = o_ref[...].astype(F32) * (z * jax.nn.sigmoid(z))
        return jnp.dot(a.astype(BF16), w_ref[...], preferred_element_type=F32)

    ya = branch(oa_ref, za_ref, wa_ref)
    yb = branch(ob_ref, zb_ref, wb_ref)
    mixed = (jax.nn.sigmoid(ga_ref[...].astype(F32)) * ya
             + jax.nn.sigmoid(gb_ref[...].astype(F32)) * yb)
    x1 = x_ref[...] + jnp.dot(mixed.astype(BF16), wo_ref[...], preferred_element_type=F32)
    hn = _rmsnorm(x1, gple_ref[...]).astype(BF16)
    pg = jax.nn.sigmoid(jnp.dot(hn, wg_ref[...], preferred_element_type=F32))
    up = jnp.dot(p_ref[...].astype(BF16), wu_ref[...], preferred_element_type=F32)
    x2 = x1 + up * pg
    if final:
        x2 = _rmsnorm(x2, gfin_ref[...])
    out_ref[...] = x2


def _out_merge(oa, ob, proj, x2, p2, wa, wb, wo, wg, wu, g_ple, g_final, *, tm, final,
               za_col, zb_col, ga_col, gb_col):
    t, d = x2.shape
    wa_w = oa.shape[1]
    wb_w = ob.shape[1]
    ple = p2.shape[1]
    row = lambda i: (i, 0)
    const = lambda i: (0, 0)
    return pl.pallas_call(
        functools.partial(_out_merge_kernel, final=final),
        grid=(t // tm,),
        in_specs=[
            pl.BlockSpec((tm, wa_w), row),
            pl.BlockSpec((tm, wb_w), row),
            pl.BlockSpec((tm, wa_w), lambda i: (i, za_col)),
            pl.BlockSpec((tm, wb_w), lambda i: (i, zb_col)),
            pl.BlockSpec((tm, d), lambda i: (i, ga_col)),
            pl.BlockSpec((tm, d), lambda i: (i, gb_col)),
            pl.BlockSpec((tm, d), row),
            pl.BlockSpec((tm, ple), row),
            _resident((wa_w, d), const),
            _resident((wb_w, d), const),
            _resident((d, d), const),
            _resident((d, d), const),
            _resident((ple, d), const),
            _resident((1, d), const),
            _resident((1, d), const),
        ],
        out_specs=pl.BlockSpec((tm, d), row),
        out_shape=jax.ShapeDtypeStruct((t, d), F32),
        compiler_params=_compiler_params(1),
        name="out_merge",
    )(oa, ob, proj, proj, proj, proj, x2, p2, wa, wb, wo, wg, wu, g_ple, g_final)


def kernel(x, p, positions, g_norm, w_in, b_f, w_branch_a, w_branch_b, w_out,
           g_ple, w_ple_gate, w_ple_up, g_final):
    batch, seq, d = x.shape
    depth = w_in.shape[0]
    t = batch * seq
    wa_w = w_branch_a.shape[1]
    wb_w = w_branch_b.shape[1]
    heads_a = wa_w // HEAD_DIM
    heads_b = wb_w // HEAD_DIM
    n_f = b_f.shape[1]
    assert seq % MOBA_BLOCK == 0 and wa_w == wb_w and 2 * wa_w == d
    f_start = 4 * wa_w + 3 * wb_w
    assert w_in.shape[2] == f_start + n_f + wb_w + 2 * d

    inv = ROPE_THETA ** (-jnp.arange(0, HEAD_DIM, 2, dtype=F32) / HEAD_DIM)
    inv_full = jnp.concatenate([inv, inv]).reshape(1, HEAD_DIM)
    pos = positions.reshape(t, 1)
    x2 = x.reshape(t, d)
    hb = HEAD_DIM

    for layer in range(depth):
        w = w_in[layer]
        w_main = jnp.concatenate([w[:, :f_start], w[:, f_start + n_f:]], axis=1).astype(BF16)
        w_f = jnp.pad(w[:, f_start:f_start + n_f], ((0, 0), (0, LANES - n_f))).astype(BF16)
        b_pad = jnp.pad(b_f[layer], (0, LANES - n_f)).reshape(1, LANES)

        proj, f = _in_proj(x2, pos, g_norm[layer].reshape(1, d), inv_full, w_main, w_f,
                           tm=1024, tn=wa_w, width_a=wa_w)
        c = _fox_gate(f, b_pad, batch=batch, seq=seq)
        oa = _attention(proj, None, moba=True, batch=batch, seq=seq, heads=heads_a,
                        q_col=0, k_col=wa_w // hb, v_col=2 * wa_w // hb)
        ob = _attention(proj, c, moba=False, batch=batch, seq=seq, heads=heads_b,
                        q_col=4 * wa_w // hb, k_col=(4 * wa_w + wb_w) // hb,
                        v_col=(4 * wa_w + 2 * wb_w) // hb)
        x2 = _out_merge(
            oa, ob, proj, x2, p[layer].reshape(t, -1),
            w_branch_a[layer].astype(BF16), w_branch_b[layer].astype(BF16),
            w_out[layer].astype(BF16), w_ple_gate[layer].astype(BF16),
            w_ple_up[layer].astype(BF16), g_ple[layer].reshape(1, d), g_final.reshape(1, d),
            tm=256, final=(layer == depth - 1),
            za_col=3, zb_col=(4 * wa_w + 3 * wb_w) // wb_w,
            ga_col=(4 * wa_w + 4 * wb_w) // d, gb_col=(4 * wa_w + 4 * wb_w) // d + 1)
    return x2.reshape(batch, seq, d)
```

```python
import functools
import math

import jax
import jax.numpy as jnp
from jax import lax
from jax.experimental import pallas as pl
from jax.experimental.pallas import tpu as pltpu

HEAD_DIM = 128
MOBA_BLOCK = 256
MOBA_TOPK = 3
ROPE_THETA = 10000.0
RMS_EPS = 1e-6

V7X_VMEM_BYTES = 64 * 1024 * 1024
VMEM_LIMIT_BYTES = V7X_VMEM_BYTES * 7 // 8
LANES = 128
BF16_SUBLANES = 16

F32 = jnp.float32
BF16 = jnp.bfloat16
NEG_INF = float("-inf")
MASKED = -1e30
LOG2E = math.log2(math.e)
NT_DIMS = (((1,), (1,)), ((), ()))


def _compiler_params(n_grid_dims):
    return pltpu.CompilerParams(
        dimension_semantics=("arbitrary",) * n_grid_dims,
        vmem_limit_bytes=VMEM_LIMIT_BYTES,
    )


def _resident(block_shape, index_map):
    return pl.BlockSpec(block_shape, index_map, pipeline_mode=pl.Buffered(1))


MXU_COLS = 256


def _in_proj_kernel(x_ref, pos_ref, g_ref, inv_ref, wlo_ref, whi_ref, wf_ref, proj_ref, f_ref,
                    h_scr, cos_scr, sin_scr, *, n_lo, qa_tile, ka_tile, qb_tile, q_scale,
                    row_tile):
    n = pl.program_id(1)
    half = HEAD_DIM // 2

    @pl.when(n == 0)
    def _normalize_and_tables():
        x = x_ref[...]
        ms = jnp.mean(x * x, axis=-1, keepdims=True)
        h = (x * lax.rsqrt(ms + RMS_EPS) * g_ref[...]).astype(BF16)
        h_scr[...] = h
        f_ref[...] = jnp.dot(h, wf_ref[...], preferred_element_type=F32)
        lane = lax.broadcasted_iota(jnp.int32, (x.shape[0], HEAD_DIM), 1)
        first = lane < half
        ang = pos_ref[...].astype(F32) * inv_ref[...]
        cs = jnp.cos(jnp.where(first, ang, ang - 0.5 * math.pi))
        sc = pltpu.roll(cs, half, 1)
        cos_scr[...] = jnp.where(first, cs, sc)
        sin_scr[...] = jnp.where(first, -sc, cs)

    is_rope = jnp.logical_or(n == qa_tile, n == ka_tile)
    scale = jnp.where(jnp.logical_or(n == qa_tile, n == qb_tile), q_scale, 1.0).astype(F32)
    coef_c = jnp.where(is_rope, cos_scr[...], 1.0) * scale
    coef_s = jnp.where(is_rope, sin_scr[...], 0.0) * scale
    from_lo = n < n_lo
    tm = h_scr.shape[0]
    for r0 in range(0, tm, row_tile):
        rows = slice(r0, r0 + row_tile)
        h = h_scr[rows, :]
        for c0 in range(0, proj_ref.shape[1], MXU_COLS):
            cols = slice(c0, c0 + MXU_COLS)
            w = jnp.where(from_lo, wlo_ref[:, cols], whi_ref[:, cols])
            acc = jnp.dot(h, w, preferred_element_type=F32)
            for j in range(0, MXU_COLS, HEAD_DIM):
                a = acc[:, j:j + HEAD_DIM]
                r = a * coef_c[rows] + pltpu.roll(a, half, 1) * coef_s[rows]
                proj_ref[rows, c0 + j:c0 + j + HEAD_DIM] = r.astype(BF16)


def _in_proj(x2, pos, g, inv_full, w_lo, w_hi, w_f, *, tm, tn, width_a):
    t, d = x2.shape
    n_lo = w_lo.shape[1] // tn
    n_hi = w_hi.shape[1] // tn
    assert width_a == tn, "one column tile per attention section"
    assert w_lo.shape[1] == n_lo * tn and w_hi.shape[1] == n_hi * tn
    kern = functools.partial(
        _in_proj_kernel, n_lo=n_lo, qa_tile=0, ka_tile=1, qb_tile=4,
        q_scale=LOG2E / math.sqrt(HEAD_DIM), row_tile=512)
    return pl.pallas_call(
        kern,
        grid=(t // tm, n_lo + n_hi),
        in_specs=[
            pl.BlockSpec((tm, d), lambda i, n: (i, 0)),
            pl.BlockSpec((tm, 1), lambda i, n: (i, 0)),
            _resident((1, d), lambda i, n: (0, 0)),
            _resident((1, LANES), lambda i, n: (0, 0)),
            pl.BlockSpec((d, tn), lambda i, n: (0, jnp.minimum(n, n_lo - 1))),
            pl.BlockSpec((d, tn), lambda i, n: (0, jnp.maximum(n - n_lo, 0))),
            _resident((d, LANES), lambda i, n: (0, 0)),
        ],
        out_specs=[
            pl.BlockSpec((tm, tn), lambda i, n: (i, n)),
            pl.BlockSpec((tm, LANES), lambda i, n: (i, 0)),
        ],
        out_shape=[
            jax.ShapeDtypeStruct((t, (n_lo + n_hi) * tn), BF16),
            jax.ShapeDtypeStruct((t, LANES), F32),
        ],
        scratch_shapes=[
            pltpu.VMEM((tm, d), BF16),
            pltpu.VMEM((tm, LANES), F32),
            pltpu.VMEM((tm, LANES), F32),
        ],
        compiler_params=_compiler_params(2),
        name="in_proj",
    )(x2, pos, g, inv_full, w_lo, w_hi, w_f)


def _fox_gate_kernel(f_ref, b_ref, c_ref, *, chunk):
    n_chunks = f_ref.shape[0] // chunk
    r = lax.broadcasted_iota(jnp.int32, (chunk, chunk), 0)
    c = lax.broadcasted_iota(jnp.int32, (chunk, chunk), 1)
    tri = (c <= r).astype(F32)

    def body(j, carry):
        off = pl.multiple_of(j * chunk, chunk)
        z = f_ref[pl.ds(off, chunk), :] + b_ref[...]
        logf = (jnp.minimum(z, 0.0) - jnp.log1p(jnp.exp(-jnp.abs(z)))) * LOG2E
        csum = jnp.dot(tri, logf, precision=lax.Precision.HIGHEST,
                       preferred_element_type=F32) + carry
        c_ref[pl.ds(off, chunk), :] = csum
        return csum[chunk - 1:chunk, :]

    lax.fori_loop(0, n_chunks, body, jnp.zeros((1, LANES), F32))


def _fox_gate(f, b_pad, *, batch, seq):
    return pl.pallas_call(
        functools.partial(_fox_gate_kernel, chunk=MOBA_BLOCK),
        grid=(batch,),
        in_specs=[
            pl.BlockSpec((seq, LANES), lambda b: (b, 0)),
            _resident((1, LANES), lambda b: (0, 0)),
        ],
        out_specs=pl.BlockSpec((seq, LANES), lambda b: (b, 0)),
        out_shape=jax.ShapeDtypeStruct(f.shape, F32),
        compiler_params=_compiler_params(1),
        name="fox_gate",
    )(f, b_pad)


ATTN_GROUP = 4
VT_ROWS = HEAD_DIM + BF16_SUBLANES


def _attn_kernel(*refs, moba, n_blocks, group):
    if moba:
        q_ref, k_ref, v_ref, o_ref, vt_scr, kaug_scr, qaug_scr, s_scr, m_scr, acc_scr, kbar_scr = refs
    else:
        q_ref, k_ref, v_ref, c_ref, o_ref, vt_scr, kaug_scr, qaug_scr, s_scr, m_scr, acc_scr = refs
    blk = MOBA_BLOCK
    pair = 2 * blk
    hg = pl.program_id(1)
    i = pl.program_id(2)
    tq = q_ref.shape[0]
    head_cols = [slice(g * HEAD_DIM, (g + 1) * HEAD_DIM) for g in range(group)]
    extra = slice(HEAD_DIM, 2 * HEAD_DIM)

    @pl.when(i == 0)
    def _per_group_setup():
        ones_row = (lax.broadcasted_iota(jnp.int32, (BF16_SUBLANES, pair), 0) == 0).astype(BF16)
        lane = lax.broadcasted_iota(jnp.int32, (pair, LANES), 1)
        row = lax.broadcasted_iota(jnp.int32, (pair, LANES), 0)

        def body(a, carry):
            off = pl.multiple_of(a * pair, pair)
            for g, cols in enumerate(head_cols):
                vt_scr[g, a, :HEAD_DIM, :] = (
                    v_ref[pl.ds(off, pair), cols].astype(F32).T.astype(BF16))
                vt_scr[g, a, HEAD_DIM:, :] = ones_row
                kb = k_ref[pl.ds(off, pair), cols]
                kaug_scr[g, pl.ds(off, pair), :HEAD_DIM] = kb
                if moba:
                    kf = kb.astype(F32)
                    kbar_scr[g, pl.ds(2 * a, 1), :] = jnp.mean(kf[:blk], axis=0, keepdims=True)
                    kbar_scr[g, pl.ds(2 * a + 1, 1), :] = jnp.mean(kf[blk:], axis=0,
                                                                   keepdims=True)
                    block_of_row = 2 * a + (row >= blk).astype(jnp.int32)
                    feats = jnp.where(lane == block_of_row, 1.0, 0.0)
                else:
                    c = c_ref[pl.ds(off, pair), :]
                    col = jnp.sum(jnp.where(lane == hg * group + g, c, 0.0), axis=1,
                                  keepdims=True)
                    hi = col.astype(BF16).astype(F32)
                    mid = (col - hi).astype(BF16).astype(F32)
                    lo = col - hi - mid
                    feats = jnp.where(lane == 0, hi,
                                      jnp.where(lane == 1, mid, jnp.where(lane == 2, lo, 0.0)))
                kaug_scr[g, pl.ds(off, pair), extra] = feats.astype(BF16)
            return carry
        lax.fori_loop(0, n_blocks // 2, body, 0)

    for g, cols in enumerate(head_cols):
        q = q_ref[:, cols]
        qaug_scr[g, :, :HEAD_DIM] = q
        if moba:
            gs = lax.dot_general(kbar_scr[g], q.astype(F32), NT_DIMS,
                                 precision=lax.Precision.HIGHEST,
                                 preferred_element_type=F32)
            blk_id = lax.broadcasted_iota(jnp.int32, gs.shape, 0)
            gs = jnp.where(blk_id < i, gs, NEG_INF)
            bias = jnp.where(blk_id == i, 0.0, MASKED)
            for _ in range(MOBA_TOPK):
                mx = jnp.max(gs, axis=0, keepdims=True)
                first = jnp.min(jnp.where(gs == mx, blk_id, n_blocks), axis=0, keepdims=True)
                pick = jnp.logical_and(blk_id == first, mx > NEG_INF)
                bias = jnp.where(pick, 0.0, bias)
                gs = jnp.where(pick, NEG_INF, gs)
            bias = jnp.concatenate([bias, jnp.zeros((LANES - n_blocks, tq), F32)], axis=0)
            qaug_scr[g, :, extra] = bias.T.astype(BF16)
        else:
            lane = lax.broadcasted_iota(jnp.int32, (tq, LANES), 1)
            qaug_scr[g, :, extra] = jnp.where(lane < 3, -1.0, 0.0).astype(BF16)

    def scores(g, a):
        off = pl.multiple_of(a * pair, pair)
        return lax.dot_general(kaug_scr[g, pl.ds(off, pair), :], qaug_scr[g], NT_DIMS,
                               preferred_element_type=F32)

    def store_scores(slot, a):
        for g in range(group):
            s_scr[slot, g] = scores(g, a)

    def fold(a, ss, first):
        ms, ps = [], []
        for g, s in enumerate(ss):
            m_new = jnp.max(s, axis=0, keepdims=True)
            if not first:
                m_new = jnp.maximum(m_scr[g], m_new)
            ms.append(m_new)
            ps.append(jnp.exp2(s - m_new).astype(BF16))
        for g, p in enumerate(ps):
            pv = jnp.dot(vt_scr[g, a], p, preferred_element_type=F32)
            if not first:
                pv = jnp.exp2(m_scr[g] - ms[g]) * acc_scr[g] + pv
            acc_scr[g] = pv
            m_scr[g] = ms[g]

    own = lax.shift_right_logical(i, 1)
    key = own * pair + lax.broadcasted_iota(jnp.int32, (pair, tq), 0)
    qry = i * blk + lax.broadcasted_iota(jnp.int32, (pair, tq), 1)
    visible = key <= qry
    ss_own = [jnp.where(visible, scores(g, own), NEG_INF) for g in range(group)]

    store_scores(0, 0)
    fold(own, ss_own, first=True)

    def two_pairs(t, carry):
        a0 = 2 * t
        store_scores(1, a0 + 1)
        fold(a0, [s_scr[0, g] for g in range(group)], first=False)
        store_scores(0, jnp.minimum(a0 + 2, own - 1))
        fold(a0 + 1, [s_scr[1, g] for g in range(group)], first=False)
        return carry

    lax.fori_loop(0, lax.shift_right_logical(own, 1), two_pairs, 0)

    @pl.when(jnp.bitwise_and(own, 1) == 1)
    def _last_odd_pair():
        fold(own - 1, [s_scr[0, g] for g in range(group)], first=False)

    for g, cols in enumerate(head_cols):
        acc = acc_scr[g]
        o = acc[:HEAD_DIM] * (1.0 / acc[HEAD_DIM:HEAD_DIM + 1])
        o_ref[:, cols] = o.T.astype(o_ref.dtype)


def _attention(proj, c, *, moba, batch, seq, heads, q_col, k_col, v_col):
    group = ATTN_GROUP
    tq = MOBA_BLOCK
    nq = seq // tq
    n_blocks = seq // MOBA_BLOCK
    width = group * HEAD_DIM
    pair = 2 * MOBA_BLOCK
    assert heads % group == 0 and n_blocks % 2 == 0 and n_blocks <= LANES
    assert q_col % group == 0 and k_col % group == 0 and v_col % group == 0
    in_specs = [
        pl.BlockSpec((tq, width), lambda b, h, i: (b * nq + i, q_col // group + h)),
        pl.BlockSpec((seq, width), lambda b, h, i: (b, k_col // group + h)),
        pl.BlockSpec((seq, width), lambda b, h, i: (b, v_col // group + h)),
    ]
    args = [proj, proj, proj]
    if not moba:
        in_specs.append(pl.BlockSpec((seq, LANES), lambda b, h, i: (b, 0)))
        args.append(c)
    scratch = [
        pltpu.VMEM((group, n_blocks // 2, VT_ROWS, pair), BF16),
        pltpu.VMEM((group, seq, 2 * HEAD_DIM), BF16),
        pltpu.VMEM((group, tq, 2 * HEAD_DIM), BF16),
        pltpu.VMEM((2, group, pair, tq), F32),
        pltpu.VMEM((group, 1, tq), F32),
        pltpu.VMEM((group, VT_ROWS, tq), F32),
    ]
    if moba:
        scratch.append(pltpu.VMEM((group, n_blocks, HEAD_DIM), F32))
    return pl.pallas_call(
        functools.partial(_attn_kernel, moba=moba, n_blocks=n_blocks, group=group),
        grid=(batch, heads // group, nq),
        in_specs=in_specs,
        out_specs=pl.BlockSpec((tq, width), lambda b, h, i: (b * nq + i, h)),
        out_shape=jax.ShapeDtypeStruct((batch * seq, heads * HEAD_DIM), BF16),
        scratch_shapes=scratch,
        compiler_params=_compiler_params(3),
        name="moba_attention" if moba else "fox_attention",
    )(*args)


def _rmsnorm(x, g):
    ms = jnp.mean(x * x, axis=-1, keepdims=True)
    return x * lax.rsqrt(ms + RMS_EPS) * g


def _out_merge_kernel(oa_ref, ob_ref, za_ref, zb_ref, ga_ref, gb_ref, x_ref, p_ref,
                      wa_ref, wb_ref, wo_ref, wg_ref, wu_ref, gple_ref, gfin_ref, out_ref,
                      *, final):
    def branch(o_ref, z_ref, w_ref):
        z = z_ref[...].astype(F32)
        a = o_ref[...].astype(F32) * (z * jax.nn.sigmoid(z))
        return jnp.dot(a.astype(BF16), w_ref[...], preferred_element_type=F32)

    ya = branch(oa_ref, za_ref, wa_ref)
    yb = branch(ob_ref, zb_ref, wb_ref)
    mixed = (jax.nn.sigmoid(ga_ref[...].astype(F32)) * ya
             + jax.nn.sigmoid(gb_ref[...].astype(F32)) * yb)
    x1 = x_ref[...] + jnp.dot(mixed.astype(BF16), wo_ref[...], preferred_element_type=F32)
    hn = _rmsnorm(x1, gple_ref[...]).astype(BF16)
    pg = jax.nn.sigmoid(jnp.dot(hn, wg_ref[...], preferred_element_type=F32))
    up = jnp.dot(p_ref[...].astype(BF16), wu_ref[...], preferred_element_type=F32)
    x2 = x1 + up * pg
    if final:
        x2 = _rmsnorm(x2, gfin_ref[...])
    out_ref[...] = x2


def _out_merge(oa, ob, proj, x2, p2, wa, wb, wo, wg, wu, g_ple, g_final, *, tm, final,
               za_col, zb_col, ga_col, gb_col):
    t, d = x2.shape
    wa_w = oa.shape[1]
    wb_w = ob.shape[1]
    ple = p2.shape[1]
    row = lambda i: (i, 0)
    const = lambda i: (0, 0)
    return pl.pallas_call(
        functools.partial(_out_merge_kernel, final=final),
        grid=(t // tm,),
        in_specs=[
            pl.BlockSpec((tm, wa_w), row),
            pl.BlockSpec((tm, wb_w), row),
            pl.BlockSpec((tm, wa_w), lambda i: (i, za_col)),
            pl.BlockSpec((tm, wb_w), lambda i: (i, zb_col)),
            pl.BlockSpec((tm, d), lambda i: (i, ga_col)),
            pl.BlockSpec((tm, d), lambda i: (i, gb_col)),
            pl.BlockSpec((tm, d), row),
            pl.BlockSpec((tm, ple), row),
            _resident((wa_w, d), const),
            _resident((wb_w, d), const),
            _resident((d, d), const),
            _resident((d, d), const),
            _resident((ple, d), const),
            _resident((1, d), const),
            _resident((1, d), const),
        ],
        out_specs=pl.BlockSpec((tm, d), row),
        out_shape=jax.ShapeDtypeStruct((t, d), F32),
        compiler_params=_compiler_params(1),
        name="out_merge",
    )(oa, ob, proj, proj, proj, proj, x2, p2, wa, wb, wo, wg, wu, g_ple, g_final)


def kernel(x, p, positions, g_norm, w_in, b_f, w_branch_a, w_branch_b, w_out,
           g_ple, w_ple_gate, w_ple_up, g_final):
    batch, seq, d = x.shape
    depth = w_in.shape[0]
    t = batch * seq
    wa_w = w_branch_a.shape[1]
    wb_w = w_branch_b.shape[1]
    heads_a = wa_w // HEAD_DIM
    heads_b = wb_w // HEAD_DIM
    n_f = b_f.shape[1]
    assert seq % MOBA_BLOCK == 0 and wa_w == wb_w and 2 * wa_w == d
    f_start = 4 * wa_w + 3 * wb_w
    assert w_in.shape[2] == f_start + n_f + wb_w + 2 * d

    inv = ROPE_THETA ** (-jnp.arange(0, HEAD_DIM, 2, dtype=F32) / HEAD_DIM)
    inv_full = jnp.concatenate([inv, inv]).reshape(1, HEAD_DIM)
    pos = positions.reshape(t, 1)
    x2 = x.reshape(t, d)
    hb = HEAD_DIM

    for layer in range(depth):
        w = w_in[layer]
        w_lo = w[:, :f_start].astype(BF16)
        w_hi = w[:, f_start + n_f:].astype(BF16)
        w_f = jnp.pad(w[:, f_start:f_start + n_f], ((0, 0), (0, LANES - n_f))).astype(BF16)
        b_pad = jnp.pad(b_f[layer], (0, LANES - n_f)).reshape(1, LANES)

        proj, f = _in_proj(x2, pos, g_norm[layer].reshape(1, d), inv_full, w_lo, w_hi, w_f,
                           tm=1024, tn=wa_w, width_a=wa_w)
        c = _fox_gate(f, b_pad, batch=batch, seq=seq)
        oa = _attention(proj, None, moba=True, batch=batch, seq=seq, heads=heads_a,
                        q_col=0, k_col=wa_w // hb, v_col=2 * wa_w // hb)
        ob = _attention(proj, c, moba=False, batch=batch, seq=seq, heads=heads_b,
                        q_col=4 * wa_w // hb, k_col=(4 * wa_w + wb_w) // hb,
                        v_col=(4 * wa_w + 2 * wb_w) // hb)
        x2 = _out_merge(
            oa, ob, proj, x2, p[layer].reshape(t, -1),
            w_branch_a[layer].astype(BF16), w_branch_b[layer].astype(BF16),
            w_out[layer].astype(BF16), w_ple_gate[layer].astype(BF16),
            w_ple_up[layer].astype(BF16), g_ple[layer].reshape(1, d), g_final.reshape(1, d),
            tm=256, final=(layer == depth - 1),
            za_col=3, zb_col=(4 * wa_w + 3 * wb_w) // wb_w,
            ga_col=(4 * wa_w + 4 * wb_w) // d, gb_col=(4 * wa_w + 4 * wb_w) // d + 1)
    return x2.reshape(batch, seq, d)
```

```python
import functools
import math

import jax
import jax.numpy as jnp
from jax import lax
from jax.experimental import pallas as pl
from jax.experimental.pallas import tpu as pltpu

HEAD_DIM = 128
MOBA_BLOCK = 256
MOBA_TOPK = 3
ROPE_THETA = 10000.0
RMS_EPS = 1e-6

V7X_VMEM_BYTES = 64 * 1024 * 1024
VMEM_LIMIT_BYTES = V7X_VMEM_BYTES * 7 // 8
LANES = 128
BF16_SUBLANES = 16

F32 = jnp.float32
BF16 = jnp.bfloat16
NEG_INF = float("-inf")
MASKED = -1e30
LOG2E = math.log2(math.e)
NT_DIMS = (((1,), (1,)), ((), ()))


def _compiler_params(n_grid_dims):
    return pltpu.CompilerParams(
        dimension_semantics=("arbitrary",) * n_grid_dims,
        vmem_limit_bytes=VMEM_LIMIT_BYTES,
    )


def _resident(block_shape, index_map):
    return pl.BlockSpec(block_shape, index_map, pipeline_mode=pl.Buffered(1))


MXU_COLS = 256


def _in_proj_kernel(x_ref, pos_ref, g_ref, inv_ref, wlo_ref, whi_ref, wf_ref, proj_ref, f_ref,
                    h_scr, cos_scr, sin_scr, *, n_lo, qa_tile, ka_tile, qb_tile, q_scale,
                    row_tile):
    n = pl.program_id(1)
    half = HEAD_DIM // 2

    @pl.when(n == 0)
    def _normalize_and_tables():
        x = x_ref[...]
        ms = jnp.mean(x * x, axis=-1, keepdims=True)
        h = (x * lax.rsqrt(ms + RMS_EPS) * g_ref[...]).astype(BF16)
        h_scr[...] = h
        f_ref[...] = jnp.dot(h, wf_ref[...], preferred_element_type=F32)
        lane = lax.broadcasted_iota(jnp.int32, (x.shape[0], HEAD_DIM), 1)
        first = lane < half
        ang = pos_ref[...].astype(F32) * inv_ref[...]
        cs = jnp.cos(jnp.where(first, ang, ang - 0.5 * math.pi))
        sc = pltpu.roll(cs, half, 1)
        cos_scr[...] = jnp.where(first, cs, sc)
        sin_scr[...] = jnp.where(first, -sc, cs)

    is_rope = jnp.logical_or(n == qa_tile, n == ka_tile)
    scale = jnp.where(jnp.logical_or(n == qa_tile, n == qb_tile), q_scale, 1.0).astype(F32)
    coef_c = jnp.where(is_rope, cos_scr[...], 1.0) * scale
    coef_s = jnp.where(is_rope, sin_scr[...], 0.0) * scale
    from_lo = n < n_lo
    tm = h_scr.shape[0]
    for r0 in range(0, tm, row_tile):
        rows = slice(r0, r0 + row_tile)
        h = h_scr[rows, :]
        for c0 in range(0, proj_ref.shape[1], MXU_COLS):
            cols = slice(c0, c0 + MXU_COLS)
            w = jnp.where(from_lo, wlo_ref[:, cols], whi_ref[:, cols])
            acc = jnp.dot(h, w, preferred_element_type=F32)
            for j in range(0, MXU_COLS, HEAD_DIM):
                a = acc[:, j:j + HEAD_DIM]
                r = a * coef_c[rows] + pltpu.roll(a, half, 1) * coef_s[rows]
                proj_ref[rows, c0 + j:c0 + j + HEAD_DIM] = r.astype(BF16)


def _in_proj(x2, pos, g, inv_full, w_lo, w_hi, w_f, *, tm, tn, width_a):
    t, d = x2.shape
    n_lo = w_lo.shape[1] // tn
    n_hi = w_hi.shape[1] // tn
    assert width_a == tn, "one column tile per attention section"
    assert w_lo.shape[1] == n_lo * tn and w_hi.shape[1] == n_hi * tn
    kern = functools.partial(
        _in_proj_kernel, n_lo=n_lo, qa_tile=0, ka_tile=1, qb_tile=4,
        q_scale=LOG2E / math.sqrt(HEAD_DIM), row_tile=512)
    return pl.pallas_call(
        kern,
        grid=(t // tm, n_lo + n_hi),
        in_specs=[
            pl.BlockSpec((tm, d), lambda i, n: (i, 0)),
            pl.BlockSpec((tm, 1), lambda i, n: (i, 0)),
            _resident((1, d), lambda i, n: (0, 0)),
            _resident((1, LANES), lambda i, n: (0, 0)),
            pl.BlockSpec((d, tn), lambda i, n: (0, jnp.minimum(n, n_lo - 1))),
            pl.BlockSpec((d, tn), lambda i, n: (0, jnp.maximum(n - n_lo, 0))),
            _resident((d, LANES), lambda i, n: (0, 0)),
        ],
        out_specs=[
            pl.BlockSpec((tm, tn), lambda i, n: (i, n)),
            pl.BlockSpec((tm, LANES), lambda i, n: (i, 0)),
        ],
        out_shape=[
            jax.ShapeDtypeStruct((t, (n_lo + n_hi) * tn), BF16),
            jax.ShapeDtypeStruct((t, LANES), F32),
        ],
        scratch_shapes=[
            pltpu.VMEM((tm, d), BF16),
            pltpu.VMEM((tm, LANES), F32),
            pltpu.VMEM((tm, LANES), F32),
        ],
        compiler_params=_compiler_params(2),
        name="in_proj",
    )(x2, pos, g, inv_full, w_lo, w_hi, w_f)


def _fox_gate_kernel(f_ref, b_ref, c_ref, *, chunk):
    n_chunks = f_ref.shape[0] // chunk
    r = lax.broadcasted_iota(jnp.int32, (chunk, chunk), 0)
    c = lax.broadcasted_iota(jnp.int32, (chunk, chunk), 1)
    tri = (c <= r).astype(F32)

    def body(j, carry):
        off = pl.multiple_of(j * chunk, chunk)
        z = f_ref[pl.ds(off, chunk), :] + b_ref[...]
        logf = (jnp.minimum(z, 0.0) - jnp.log1p(jnp.exp(-jnp.abs(z)))) * LOG2E
        csum = jnp.dot(tri, logf, precision=lax.Precision.HIGHEST,
                       preferred_element_type=F32) + carry
        c_ref[pl.ds(off, chunk), :] = csum
        return csum[chunk - 1:chunk, :]

    lax.fori_loop(0, n_chunks, body, jnp.zeros((1, LANES), F32))


def _fox_gate(f, b_pad, *, batch, seq):
    return pl.pallas_call(
        functools.partial(_fox_gate_kernel, chunk=MOBA_BLOCK),
        grid=(batch,),
        in_specs=[
            pl.BlockSpec((seq, LANES), lambda b: (b, 0)),
            _resident((1, LANES), lambda b: (0, 0)),
        ],
        out_specs=pl.BlockSpec((seq, LANES), lambda b: (b, 0)),
        out_shape=jax.ShapeDtypeStruct(f.shape, F32),
        compiler_params=_compiler_params(1),
        name="fox_gate",
    )(f, b_pad)


ATTN_GROUP = 4
VT_ROWS = HEAD_DIM + BF16_SUBLANES
KEY_PAIR = 2 * MOBA_BLOCK


def _attn_kernel(*refs, moba, n_blocks, group):
    if moba:
        q_ref, k_ref, v_ref, o_ref, vt_scr, kaug_scr, qaug_scr, s_scr, m_scr, acc_scr, kbar_scr = refs
    else:
        q_ref, k_ref, v_ref, c_ref, o_ref, vt_scr, kaug_scr, qaug_scr, s_scr, m_scr, acc_scr = refs
    blk = MOBA_BLOCK
    pair = KEY_PAIR
    hg = pl.program_id(1)
    own = pl.program_id(2)
    head_cols = [slice(g * HEAD_DIM, (g + 1) * HEAD_DIM) for g in range(group)]
    extra = slice(HEAD_DIM, 2 * HEAD_DIM)

    @pl.when(own == 0)
    def _per_group_setup():
        ones_row = (lax.broadcasted_iota(jnp.int32, (BF16_SUBLANES, pair), 0) == 0).astype(BF16)
        lane = lax.broadcasted_iota(jnp.int32, (pair, LANES), 1)
        row = lax.broadcasted_iota(jnp.int32, (pair, LANES), 0)

        def body(a, carry):
            off = pl.multiple_of(a * pair, pair)
            for g, cols in enumerate(head_cols):
                vt_scr[g, a, :HEAD_DIM, :] = (
                    v_ref[pl.ds(off, pair), cols].astype(F32).T.astype(BF16))
                vt_scr[g, a, HEAD_DIM:, :] = ones_row
                kb = k_ref[pl.ds(off, pair), cols]
                kaug_scr[g, pl.ds(off, pair), :HEAD_DIM] = kb
                if moba:
                    kf = kb.astype(F32)
                    kbar_scr[g, pl.ds(2 * a, 1), :] = jnp.mean(kf[:blk], axis=0, keepdims=True)
                    kbar_scr[g, pl.ds(2 * a + 1, 1), :] = jnp.mean(kf[blk:], axis=0,
                                                                   keepdims=True)
                    block_of_row = 2 * a + (row >= blk).astype(jnp.int32)
                    feats = jnp.where(lane == block_of_row, 1.0, 0.0)
                else:
                    c = c_ref[pl.ds(off, pair), :]
                    col = jnp.sum(jnp.where(lane == hg * group + g, c, 0.0), axis=1,
                                  keepdims=True)
                    hi = col.astype(BF16).astype(F32)
                    mid = (col - hi).astype(BF16).astype(F32)
                    lo = col - hi - mid
                    feats = jnp.where(lane == 0, hi,
                                      jnp.where(lane == 1, mid, jnp.where(lane == 2, lo, 0.0)))
                kaug_scr[g, pl.ds(off, pair), extra] = feats.astype(BF16)
            return carry
        lax.fori_loop(0, n_blocks // 2, body, 0)

    for g, cols in enumerate(head_cols):
        q = q_ref[:, cols]
        qaug_scr[g, :, :HEAD_DIM] = q
        if moba:
            gs = lax.dot_general(kbar_scr[g], q.astype(F32), NT_DIMS,
                                 precision=lax.Precision.HIGHEST,
                                 preferred_element_type=F32)
            blk_id = lax.broadcasted_iota(jnp.int32, gs.shape, 0)
            q_blk = 2 * own + (lax.broadcasted_iota(jnp.int32, gs.shape, 1) >= blk).astype(
                jnp.int32)
            gs = jnp.where(blk_id < q_blk, gs, NEG_INF)
            bias = jnp.where(blk_id == q_blk, 0.0, MASKED)
            for _ in range(MOBA_TOPK):
                mx = jnp.max(gs, axis=0, keepdims=True)
                first = jnp.min(jnp.where(gs == mx, blk_id, n_blocks), axis=0, keepdims=True)
                pick = jnp.logical_and(blk_id == first, mx > NEG_INF)
                bias = jnp.where(pick, 0.0, bias)
                gs = jnp.where(pick, NEG_INF, gs)
            bias = jnp.concatenate([bias, jnp.zeros((LANES - n_blocks, pair), F32)], axis=0)
            qaug_scr[g, :, extra] = bias.T.astype(BF16)
        else:
            lane = lax.broadcasted_iota(jnp.int32, (pair, LANES), 1)
            qaug_scr[g, :, extra] = jnp.where(lane < 3, -1.0, 0.0).astype(BF16)

    def scores(g, a):
        off = pl.multiple_of(a * pair, pair)
        return lax.dot_general(kaug_scr[g, pl.ds(off, pair), :], qaug_scr[g], NT_DIMS,
                               preferred_element_type=F32)

    def fold_head(g, a, s, first=False):
        m_new = jnp.max(s, axis=0, keepdims=True)
        if not first:
            m_old = m_scr[g]
            m_new = jnp.maximum(m_old, m_new)
        p = jnp.exp2(s - m_new).astype(BF16)
        pv = jnp.dot(vt_scr[g, a], p, preferred_element_type=F32)
        if not first:
            pv = jnp.exp2(m_old - m_new) * acc_scr[g] + pv
        acc_scr[g] = pv
        m_scr[g] = m_new

    visible = (lax.broadcasted_iota(jnp.int32, (pair, pair), 0)
               <= lax.broadcasted_iota(jnp.int32, (pair, pair), 1))
    for g in range(group):
        s_scr[0, g] = scores(g, 0)
        fold_head(g, own, jnp.where(visible, scores(g, own), NEG_INF), first=True)

    def pipelined(slot_next, a_next, slot_cur, a_cur):
        for g in range(group):
            s_scr[slot_next, g] = scores(g, a_next)
            fold_head(g, a_cur, s_scr[slot_cur, g])

    def two_pairs(t, carry):
        a0 = 2 * t
        pipelined(1, a0 + 1, 0, a0)
        pipelined(0, jnp.minimum(a0 + 2, own - 1), 1, a0 + 1)
        return carry

    lax.fori_loop(0, lax.shift_right_logical(own, 1), two_pairs, 0)

    @pl.when(jnp.bitwise_and(own, 1) == 1)
    def _last_odd_pair():
        for g in range(group):
            fold_head(g, own - 1, s_scr[0, g])

    for g, cols in enumerate(head_cols):
        acc = acc_scr[g]
        o = acc[:HEAD_DIM] * (1.0 / acc[HEAD_DIM:HEAD_DIM + 1])
        o_ref[:, cols] = o.T.astype(o_ref.dtype)


def _attention(proj, c, *, moba, batch, seq, heads, q_col, k_col, v_col):
    group = ATTN_GROUP
    tq = KEY_PAIR
    nq = seq // tq
    n_blocks = seq // MOBA_BLOCK
    width = group * HEAD_DIM
    assert heads % group == 0 and seq % KEY_PAIR == 0 and n_blocks <= LANES
    assert q_col % group == 0 and k_col % group == 0 and v_col % group == 0
    in_specs = [
        pl.BlockSpec((tq, width), lambda b, h, i: (b * nq + i, q_col // group + h)),
        pl.BlockSpec((seq, width), lambda b, h, i: (b, k_col // group + h)),
        pl.BlockSpec((seq, width), lambda b, h, i: (b, v_col // group + h)),
    ]
    args = [proj, proj, proj]
    if not moba:
        in_specs.append(pl.BlockSpec((seq, LANES), lambda b, h, i: (b, 0)))
        args.append(c)
    scratch = [
        pltpu.VMEM((group, seq // KEY_PAIR, VT_ROWS, KEY_PAIR), BF16),
        pltpu.VMEM((group, seq, 2 * HEAD_DIM), BF16),
        pltpu.VMEM((group, tq, 2 * HEAD_DIM), BF16),
        pltpu.VMEM((2, group, KEY_PAIR, tq), F32),
        pltpu.VMEM((group, 1, tq), F32),
        pltpu.VMEM((group, VT_ROWS, tq), F32),
    ]
    if moba:
        scratch.append(pltpu.VMEM((group, n_blocks, HEAD_DIM), F32))
    return pl.pallas_call(
        functools.partial(_attn_kernel, moba=moba, n_blocks=n_blocks, group=group),
        grid=(batch, heads // group, nq),
        in_specs=in_specs,
        out_specs=pl.BlockSpec((tq, width), lambda b, h, i: (b * nq + i, h)),
        out_shape=jax.ShapeDtypeStruct((batch * seq, heads * HEAD_DIM), BF16),
        scratch_shapes=scratch,
        compiler_params=_compiler_params(3),
        name="moba_attention" if moba else "fox_attention",
    )(*args)


def _rmsnorm(x, g):
    ms = jnp.mean(x * x, axis=-1, keepdims=True)
    return x * lax.rsqrt(ms + RMS_EPS) * g


def _out_merge_kernel(oa_ref, ob_ref, za_ref, zb_ref, ga_ref, gb_ref, x_ref, p_ref,
                      wa_ref, wb_ref, wo_ref, wg_ref, wu_ref, gple_ref, gfin_ref, out_ref,
                      *, final):
    def branch(o_ref, z_ref, w_ref):
        z = z_ref[...].astype(F32)
        a = o_ref[...].astype(F32) * (z * jax.nn.sigmoid(z))
        return jnp.dot(a.astype(BF16), w_ref[...], preferred_element_type=F32)

    ya = branch(oa_ref, za_ref, wa_ref)
    yb = branch(ob_ref, zb_ref, wb_ref)
    mixed = (jax.nn.sigmoid(ga_ref[...].astype(F32)) * ya
             + jax.nn.sigmoid(gb_ref[...].astype(F32)) * yb)
    x1 = x_ref[...] + jnp.dot(mixed.astype(BF16), wo_ref[...], preferred_element_type=F32)
    hn = _rmsnorm(x1, gple_ref[...]).astype(BF16)
    pg = jax.nn.sigmoid(jnp.dot(hn, wg_ref[...], preferred_element_type=F32))
    up = jnp.dot(p_ref[...].astype(BF16), wu_ref[...], preferred_element_type=F32)
    x2 = x1 + up * pg
    if final:
        x2 = _rmsnorm(x2, gfin_ref[...])
    out_ref[...] = x2


def _out_merge(oa, ob, proj, x2, p2, wa, wb, wo, wg, wu, g_ple, g_final, *, tm, final,
               za_col, zb_col, ga_col, gb_col):
    t, d = x2.shape
    wa_w = oa.shape[1]
    wb_w = ob.shape[1]
    ple = p2.shape[1]
    row = lambda i: (i, 0)
    const = lambda i: (0, 0)
    return pl.pallas_call(
        functools.partial(_out_merge_kernel, final=final),
        grid=(t // tm,),
        in_specs=[
            pl.BlockSpec((tm, wa_w), row),
            pl.BlockSpec((tm, wb_w), row),
            pl.BlockSpec((tm, wa_w), lambda i: (i, za_col)),
            pl.BlockSpec((tm, wb_w), lambda i: (i, zb_col)),
            pl.BlockSpec((tm, d), lambda i: (i, ga_col)),
            pl.BlockSpec((tm, d), lambda i: (i, gb_col)),
            pl.BlockSpec((tm, d), row),
            pl.BlockSpec((tm, ple), row),
            _resident((wa_w, d), const),
            _resident((wb_w, d), const),
            _resident((d, d), const),
            _resident((d, d), const),
            _resident((ple, d), const),
            _resident((1, d), const),
            _resident((1, d), const),
        ],
        out_specs=pl.BlockSpec((tm, d), row),
        out_shape=jax.ShapeDtypeStruct((t, d), F32),
        compiler_params=_compiler_params(1),
        name="out_merge",
    )(oa, ob, proj, proj, proj, proj, x2, p2, wa, wb, wo, wg, wu, g_ple, g_final)


def kernel(x, p, positions, g_norm, w_in, b_f, w_branch_a, w_branch_b, w_out,
           g_ple, w_ple_gate, w_ple_up, g_final):
    batch, seq, d = x.shape
    depth = w_in.shape[0]
    t = batch * seq
    wa_w = w_branch_a.shape[1]
    wb_w = w_branch_b.shape[1]
    heads_a = wa_w // HEAD_DIM
    heads_b = wb_w // HEAD_DIM
    n_f = b_f.shape[1]
    assert seq % MOBA_BLOCK == 0 and wa_w == wb_w and 2 * wa_w == d
    f_start = 4 * wa_w + 3 * wb_w
    assert w_in.shape[2] == f_start + n_f + wb_w + 2 * d

    inv = ROPE_THETA ** (-jnp.arange(0, HEAD_DIM, 2, dtype=F32) / HEAD_DIM)
    inv_full = jnp.concatenate([inv, inv]).reshape(1, HEAD_DIM)
    pos = positions.reshape(t, 1)
    x2 = x.reshape(t, d)
    hb = HEAD_DIM

    for layer in range(depth):
        w = w_in[layer]
        w_lo = w[:, :f_start].astype(BF16)
        w_hi = w[:, f_start + n_f:].astype(BF16)
        w_f = jnp.pad(w[:, f_start:f_start + n_f], ((0, 0), (0, LANES - n_f))).astype(BF16)
        b_pad = jnp.pad(b_f[layer], (0, LANES - n_f)).reshape(1, LANES)

        proj, f = _in_proj(x2, pos, g_norm[layer].reshape(1, d), inv_full, w_lo, w_hi, w_f,
                           tm=1024, tn=wa_w, width_a=wa_w)
        c = _fox_gate(f, b_pad, batch=batch, seq=seq)
        oa = _attention(proj, None, moba=True, batch=batch, seq=seq, heads=heads_a,
                        q_col=0, k_col=wa_w // hb, v_col=2 * wa_w // hb)
        ob = _attention(proj, c, moba=False, batch=batch, seq=seq, heads=heads_b,
                        q_col=4 * wa_w // hb, k_col=(4 * wa_w + wb_w) // hb,
                        v_col=(4 * wa_w + 2 * wb_w) // hb)
        x2 = _out_merge(
            oa, ob, proj, x2, p[layer].reshape(t, -1),
            w_branch_a[layer].astype(BF16), w_branch_b[layer].astype(BF16),
            w_out[layer].astype(BF16), w_ple_gate[layer].astype(BF16),
            w_ple_up[layer].astype(BF16), g_ple[layer].reshape(1, d), g_final.reshape(1, d),
            tm=256, final=(layer == depth - 1),
            za_col=3, zb_col=(4 * wa_w + 3 * wb_w) // wb_w,
            ga_col=(4 * wa_w + 4 * wb_w) // d, gb_col=(4 * wa_w + 4 * wb_w) // d + 1)
    return x2.reshape(batch, seq, d)
```

```python
import functools
import math

import jax
import jax.numpy as jnp
from jax import lax
from jax.experimental import pallas as pl
from jax.experimental.pallas import tpu as pltpu

HEAD_DIM = 128
MOBA_BLOCK = 256
MOBA_TOPK = 3
ROPE_THETA = 10000.0
RMS_EPS = 1e-6

V7X_VMEM_BYTES = 64 * 1024 * 1024
VMEM_LIMIT_BYTES = V7X_VMEM_BYTES * 7 // 8
LANES = 128
BF16_SUBLANES = 16

F32 = jnp.float32
BF16 = jnp.bfloat16
NEG_INF = float("-inf")
MASKED = -1e30
LOG2E = math.log2(math.e)
NT_DIMS = (((1,), (1,)), ((), ()))


def _compiler_params(n_grid_dims):
    return pltpu.CompilerParams(
        dimension_semantics=("arbitrary",) * n_grid_dims,
        vmem_limit_bytes=VMEM_LIMIT_BYTES,
    )


def _resident(block_shape, index_map):
    return pl.BlockSpec(block_shape, index_map, pipeline_mode=pl.Buffered(1))


MXU_COLS = 256


def _in_proj_kernel(x_ref, pos_ref, g_ref, inv_ref, wlo_ref, whi_ref, wf_ref, proj_ref, f_ref,
                    h_scr, cos_scr, sin_scr, *, n_lo, qa_tile, ka_tile, qb_tile, q_scale,
                    row_tile):
    n = pl.program_id(1)
    half = HEAD_DIM // 2

    @pl.when(n == 0)
    def _normalize_and_tables():
        x = x_ref[...]
        ms = jnp.mean(x * x, axis=-1, keepdims=True)
        h = (x * lax.rsqrt(ms + RMS_EPS) * g_ref[...]).astype(BF16)
        h_scr[...] = h
        f_ref[...] = jnp.dot(h, wf_ref[...], preferred_element_type=F32)
        lane = lax.broadcasted_iota(jnp.int32, (x.shape[0], HEAD_DIM), 1)
        first = lane < half
        ang = pos_ref[...].astype(F32) * inv_ref[...]
        cs = jnp.cos(jnp.where(first, ang, ang - 0.5 * math.pi))
        sc = pltpu.roll(cs, half, 1)
        cos_scr[...] = jnp.where(first, cs, sc)
        sin_scr[...] = jnp.where(first, -sc, cs)

    is_rope = jnp.logical_or(n == qa_tile, n == ka_tile)
    scale = jnp.where(jnp.logical_or(n == qa_tile, n == qb_tile), q_scale, 1.0).astype(F32)
    coef_c = jnp.where(is_rope, cos_scr[...], 1.0) * scale
    coef_s = jnp.where(is_rope, sin_scr[...], 0.0) * scale
    from_lo = n < n_lo
    tm = h_scr.shape[0]
    for r0 in range(0, tm, row_tile):
        rows = slice(r0, r0 + row_tile)
        h = h_scr[rows, :]
        for c0 in range(0, proj_ref.shape[1], MXU_COLS):
            cols = slice(c0, c0 + MXU_COLS)
            w = jnp.where(from_lo, wlo_ref[:, cols], whi_ref[:, cols])
            acc = jnp.dot(h, w, preferred_element_type=F32)
            for j in range(0, MXU_COLS, HEAD_DIM):
                a = acc[:, j:j + HEAD_DIM]
                r = a * coef_c[rows] + pltpu.roll(a, half, 1) * coef_s[rows]
                proj_ref[rows, c0 + j:c0 + j + HEAD_DIM] = r.astype(BF16)


def _in_proj(x2, pos, g, inv_full, w_lo, w_hi, w_f, *, tm, tn, width_a):
    t, d = x2.shape
    n_lo = w_lo.shape[1] // tn
    n_hi = w_hi.shape[1] // tn
    assert width_a == tn, "one column tile per attention section"
    assert w_lo.shape[1] == n_lo * tn and w_hi.shape[1] == n_hi * tn
    kern = functools.partial(
        _in_proj_kernel, n_lo=n_lo, qa_tile=0, ka_tile=1, qb_tile=4,
        q_scale=LOG2E / math.sqrt(HEAD_DIM), row_tile=512)
    return pl.pallas_call(
        kern,
        grid=(t // tm, n_lo + n_hi),
        in_specs=[
            pl.BlockSpec((tm, d), lambda i, n: (i, 0)),
            pl.BlockSpec((tm, 1), lambda i, n: (i, 0)),
            _resident((1, d), lambda i, n: (0, 0)),
            _resident((1, LANES), lambda i, n: (0, 0)),
            pl.BlockSpec((d, tn), lambda i, n: (0, jnp.minimum(n, n_lo - 1))),
            pl.BlockSpec((d, tn), lambda i, n: (0, jnp.maximum(n - n_lo, 0))),
            _resident((d, LANES), lambda i, n: (0, 0)),
        ],
        out_specs=[
            pl.BlockSpec((tm, tn), lambda i, n: (i, n)),
            pl.BlockSpec((tm, LANES), lambda i, n: (i, 0)),
        ],
        out_shape=[
            jax.ShapeDtypeStruct((t, (n_lo + n_hi) * tn), BF16),
            jax.ShapeDtypeStruct((t, LANES), F32),
        ],
        scratch_shapes=[
            pltpu.VMEM((tm, d), BF16),
            pltpu.VMEM((tm, LANES), F32),
            pltpu.VMEM((tm, LANES), F32),
        ],
        compiler_params=_compiler_params(2),
        name="in_proj",
    )(x2, pos, g, inv_full, w_lo, w_hi, w_f)


def _fox_gate_kernel(f_ref, b_ref, c_ref, *, chunk):
    n_chunks = f_ref.shape[0] // chunk
    r = lax.broadcasted_iota(jnp.int32, (chunk, chunk), 0)
    c = lax.broadcasted_iota(jnp.int32, (chunk, chunk), 1)
    tri = (c <= r).astype(F32)

    def body(j, carry):
        off = pl.multiple_of(j * chunk, chunk)
        z = f_ref[pl.ds(off, chunk), :] + b_ref[...]
        logf = (jnp.minimum(z, 0.0) - jnp.log1p(jnp.exp(-jnp.abs(z)))) * LOG2E
        csum = jnp.dot(tri, logf, precision=lax.Precision.HIGHEST,
                       preferred_element_type=F32) + carry
        c_ref[pl.ds(off, chunk), :] = csum
        return csum[chunk - 1:chunk, :]

    lax.fori_loop(0, n_chunks, body, jnp.zeros((1, LANES), F32))


def _fox_gate(f, b_pad, *, batch, seq):
    return pl.pallas_call(
        functools.partial(_fox_gate_kernel, chunk=MOBA_BLOCK),
        grid=(batch,),
        in_specs=[
            pl.BlockSpec((seq, LANES), lambda b: (b, 0)),
            _resident((1, LANES), lambda b: (0, 0)),
        ],
        out_specs=pl.BlockSpec((seq, LANES), lambda b: (b, 0)),
        out_shape=jax.ShapeDtypeStruct(f.shape, F32),
        compiler_params=_compiler_params(1),
        name="fox_gate",
    )(f, b_pad)


ATTN_GROUP = 4
VT_ROWS = HEAD_DIM + BF16_SUBLANES
KEY_PAIR = 2 * MOBA_BLOCK


def _attn_kernel(*refs, moba, n_blocks, group):
    if moba:
        (q_ref, k_ref, v_ref, o_ref,
         vt_scr, kaug_scr, qaug_scr, s_scr, smax_scr, m_scr, acc_scr, kbar_scr) = refs
    else:
        (q_ref, k_ref, v_ref, c_ref, o_ref,
         vt_scr, kaug_scr, qaug_scr, s_scr, smax_scr, m_scr, acc_scr) = refs
    blk = MOBA_BLOCK
    pair = KEY_PAIR
    hg = pl.program_id(1)
    own = pl.program_id(2)
    head_cols = [slice(g * HEAD_DIM, (g + 1) * HEAD_DIM) for g in range(group)]
    extra = slice(HEAD_DIM, 2 * HEAD_DIM)

    @pl.when(own == 0)
    def _per_group_setup():
        ones_row = (lax.broadcasted_iota(jnp.int32, (BF16_SUBLANES, pair), 0) == 0).astype(BF16)
        lane = lax.broadcasted_iota(jnp.int32, (pair, LANES), 1)
        row = lax.broadcasted_iota(jnp.int32, (pair, LANES), 0)

        def body(a, carry):
            off = pl.multiple_of(a * pair, pair)
            for g, cols in enumerate(head_cols):
                vt_scr[g, a, :HEAD_DIM, :] = (
                    v_ref[pl.ds(off, pair), cols].astype(F32).T.astype(BF16))
                vt_scr[g, a, HEAD_DIM:, :] = ones_row
                kb = k_ref[pl.ds(off, pair), cols]
                kaug_scr[g, pl.ds(off, pair), :HEAD_DIM] = kb
                if moba:
                    kf = kb.astype(F32)
                    kbar_scr[g, pl.ds(2 * a, 1), :] = jnp.mean(kf[:blk], axis=0, keepdims=True)
                    kbar_scr[g, pl.ds(2 * a + 1, 1), :] = jnp.mean(kf[blk:], axis=0,
                                                                   keepdims=True)
                    block_of_row = 2 * a + (row >= blk).astype(jnp.int32)
                    feats = jnp.where(lane == block_of_row, 1.0, 0.0)
                else:
                    c = c_ref[pl.ds(off, pair), :]
                    col = jnp.sum(jnp.where(lane == hg * group + g, c, 0.0), axis=1,
                                  keepdims=True)
                    hi = col.astype(BF16).astype(F32)
                    mid = (col - hi).astype(BF16).astype(F32)
                    lo = col - hi - mid
                    feats = jnp.where(lane == 0, hi,
                                      jnp.where(lane == 1, mid, jnp.where(lane == 2, lo, 0.0)))
                kaug_scr[g, pl.ds(off, pair), extra] = feats.astype(BF16)
            return carry
        lax.fori_loop(0, n_blocks // 2, body, 0)

    for g, cols in enumerate(head_cols):
        q = q_ref[:, cols]
        qaug_scr[g, :, :HEAD_DIM] = q
        if moba:
            gs = lax.dot_general(kbar_scr[g], q.astype(F32), NT_DIMS,
                                 precision=lax.Precision.HIGHEST,
                                 preferred_element_type=F32)
            blk_id = lax.broadcasted_iota(jnp.int32, gs.shape, 0)
            q_blk = 2 * own + (lax.broadcasted_iota(jnp.int32, gs.shape, 1) >= blk).astype(
                jnp.int32)
            gs = jnp.where(blk_id < q_blk, gs, NEG_INF)
            bias = jnp.where(blk_id == q_blk, 0.0, MASKED)
            for _ in range(MOBA_TOPK):
                mx = jnp.max(gs, axis=0, keepdims=True)
                first = jnp.min(jnp.where(gs == mx, blk_id, n_blocks), axis=0, keepdims=True)
                pick = jnp.logical_and(blk_id == first, mx > NEG_INF)
                bias = jnp.where(pick, 0.0, bias)
                gs = jnp.where(pick, NEG_INF, gs)
            bias = jnp.concatenate([bias, jnp.zeros((LANES - n_blocks, pair), F32)], axis=0)
            qaug_scr[g, :, extra] = bias.T.astype(BF16)
        else:
            lane = lax.broadcasted_iota(jnp.int32, (pair, LANES), 1)
            qaug_scr[g, :, extra] = jnp.where(lane < 3, -1.0, 0.0).astype(BF16)

    def scores(g, a):
        off = pl.multiple_of(a * pair, pair)
        return lax.dot_general(kaug_scr[g, pl.ds(off, pair), :], qaug_scr[g], NT_DIMS,
                               preferred_element_type=F32)

    def stage_scores(slot, g, s):
        s_scr[slot, g] = s
        smax_scr[slot, g] = jnp.max(s, axis=0, keepdims=True)

    def fold_head(slot, g, a, first=False):
        m_new = smax_scr[slot, g]
        if not first:
            m_old = m_scr[g]
            m_new = jnp.maximum(m_old, m_new)
        p = jnp.exp2(s_scr[slot, g] - m_new).astype(BF16)
        pv = jnp.dot(vt_scr[g, a], p, preferred_element_type=F32)
        if not first:
            pv = jnp.exp2(m_old - m_new) * acc_scr[g] + pv
        acc_scr[g] = pv
        m_scr[g] = m_new

    visible = (lax.broadcasted_iota(jnp.int32, (pair, pair), 0)
               <= lax.broadcasted_iota(jnp.int32, (pair, pair), 1))
    for g in range(group):
        stage_scores(1, g, jnp.where(visible, scores(g, own), NEG_INF))
    for g in range(group):
        stage_scores(0, g, scores(g, 0))
        fold_head(1, g, own, first=True)

    def pipelined(slot_next, a_next, slot_cur, a_cur):
        for g in range(group):
            stage_scores(slot_next, g, scores(g, a_next))
            fold_head(slot_cur, g, a_cur)

    def two_pairs(t, carry):
        a0 = 2 * t
        pipelined(1, a0 + 1, 0, a0)
        pipelined(0, jnp.minimum(a0 + 2, own - 1), 1, a0 + 1)
        return carry

    lax.fori_loop(0, lax.shift_right_logical(own, 1), two_pairs, 0)

    @pl.when(jnp.bitwise_and(own, 1) == 1)
    def _last_odd_pair():
        for g in range(group):
            fold_head(0, g, own - 1)

    for g, cols in enumerate(head_cols):
        acc = acc_scr[g]
        o = acc[:HEAD_DIM] * (1.0 / acc[HEAD_DIM:HEAD_DIM + 1])
        o_ref[:, cols] = o.T.astype(o_ref.dtype)


def _attention(proj, c, *, moba, batch, seq, heads, q_col, k_col, v_col):
    group = ATTN_GROUP
    tq = KEY_PAIR
    nq = seq // tq
    n_blocks = seq // MOBA_BLOCK
    width = group * HEAD_DIM
    assert heads % group == 0 and seq % KEY_PAIR == 0 and n_blocks <= LANES
    assert q_col % group == 0 and k_col % group == 0 and v_col % group == 0
    in_specs = [
        pl.BlockSpec((tq, width), lambda b, h, i: (b * nq + i, q_col // group + h)),
        pl.BlockSpec((seq, width), lambda b, h, i: (b, k_col // group + h)),
        pl.BlockSpec((seq, width), lambda b, h, i: (b, v_col // group + h)),
    ]
    args = [proj, proj, proj]
    if not moba:
        in_specs.append(pl.BlockSpec((seq, LANES), lambda b, h, i: (b, 0)))
        args.append(c)
    scratch = [
        pltpu.VMEM((group, seq // KEY_PAIR, VT_ROWS, KEY_PAIR), BF16),
        pltpu.VMEM((group, seq, 2 * HEAD_DIM), BF16),
        pltpu.VMEM((group, tq, 2 * HEAD_DIM), BF16),
        pltpu.VMEM((2, group, KEY_PAIR, tq), F32),
        pltpu.VMEM((2, group, 1, tq), F32),
        pltpu.VMEM((group, 1, tq), F32),
        pltpu.VMEM((group, VT_ROWS, tq), F32),
    ]
    if moba:
        scratch.append(pltpu.VMEM((group, n_blocks, HEAD_DIM), F32))
    return pl.pallas_call(
        functools.partial(_attn_kernel, moba=moba, n_blocks=n_blocks, group=group),
        grid=(batch, heads // group, nq),
        in_specs=in_specs,
        out_specs=pl.BlockSpec((tq, width), lambda b, h, i: (b * nq + i, h)),
        out_shape=jax.ShapeDtypeStruct((batch * seq, heads * HEAD_DIM), BF16),
        scratch_shapes=scratch,
        compiler_params=_compiler_params(3),
        name="moba_attention" if moba else "fox_attention",
    )(*args)


def _rmsnorm(x, g):
    ms = jnp.mean(x * x, axis=-1, keepdims=True)
    return x * lax.rsqrt(ms + RMS_EPS) * g


def _out_merge_kernel(oa_ref, ob_ref, za_ref, zb_ref, ga_ref, gb_ref, x_ref, p_ref,
                      wa_ref, wb_ref, wo_ref, wg_ref, wu_ref, gple_ref, gfin_ref, out_ref,
                      *, final):
    def branch(o_ref, z_ref, w_ref):
        z = z_ref[...].astype(F32)
        a = o_ref[...].astype(F32) * (z * jax.nn.sigmoid(z))
        return jnp.dot(a.astype(BF16), w_ref[...], preferred_element_type=F32)

    ya = branch(oa_ref, za_ref, wa_ref)
    yb = branch(ob_ref, zb_ref, wb_ref)
    mixed = (jax.nn.sigmoid(ga_ref[...].astype(F32)) * ya
             + jax.nn.sigmoid(gb_ref[...].astype(F32)) * yb)
    x1 = x_ref[...] + jnp.dot(mixed.astype(BF16), wo_ref[...], preferred_element_type=F32)
    hn = _rmsnorm(x1, gple_ref[...]).astype(BF16)
    pg = jax.nn.sigmoid(jnp.dot(hn, wg_ref[...], preferred_element_type=F32))
    up = jnp.dot(p_ref[...].astype(BF16), wu_ref[...], preferred_element_type=F32)
    x2 = x1 + up * pg
    if final:
        x2 = _rmsnorm(x2, gfin_ref[...])
    out_ref[...] = x2


def _out_merge(oa, ob, proj, x2, p2, wa, wb, wo, wg, wu, g_ple, g_final, *, tm, final,
               za_col, zb_col, ga_col, gb_col):
    t, d = x2.shape
    wa_w = oa.shape[1]
    wb_w = ob.shape[1]
    ple = p2.shape[1]
    row = lambda i: (i, 0)
    const = lambda i: (0, 0)
    return pl.pallas_call(
        functools.partial(_out_merge_kernel, final=final),
        grid=(t // tm,),
        in_specs=[
            pl.BlockSpec((tm, wa_w), row),
            pl.BlockSpec((tm, wb_w), row),
            pl.BlockSpec((tm, wa_w), lambda i: (i, za_col)),
            pl.BlockSpec((tm, wb_w), lambda i: (i, zb_col)),
            pl.BlockSpec((tm, d), lambda i: (i, ga_col)),
            pl.BlockSpec((tm, d), lambda i: (i, gb_col)),
            pl.BlockSpec((tm, d), row),
            pl.BlockSpec((tm, ple), row),
            _resident((wa_w, d), const),
            _resident((wb_w, d), const),
            _resident((d, d), const),
            _resident((d, d), const),
            _resident((ple, d), const),
            _resident((1, d), const),
            _resident((1, d), const),
        ],
        out_specs=pl.BlockSpec((tm, d), row),
        out_shape=jax.ShapeDtypeStruct((t, d), F32),
        compiler_params=_compiler_params(1),
        name="out_merge",
    )(oa, ob, proj, proj, proj, proj, x2, p2, wa, wb, wo, wg, wu, g_ple, g_final)


def kernel(x, p, positions, g_norm, w_in, b_f, w_branch_a, w_branch_b, w_out,
           g_ple, w_ple_gate, w_ple_up, g_final):
    batch, seq, d = x.shape
    depth = w_in.shape[0]
    t = batch * seq
    wa_w = w_branch_a.shape[1]
    wb_w = w_branch_b.shape[1]
    heads_a = wa_w // HEAD_DIM
    heads_b = wb_w // HEAD_DIM
    n_f = b_f.shape[1]
    assert seq % MOBA_BLOCK == 0 and wa_w == wb_w and 2 * wa_w == d
    f_start = 4 * wa_w + 3 * wb_w
    assert w_in.shape[2] == f_start + n_f + wb_w + 2 * d

    inv = ROPE_THETA ** (-jnp.arange(0, HEAD_DIM, 2, dtype=F32) / HEAD_DIM)
    inv_full = jnp.concatenate([inv, inv]).reshape(1, HEAD_DIM)
    pos = positions.reshape(t, 1)
    x2 = x.reshape(t, d)
    hb = HEAD_DIM

    for layer in range(depth):
        w = w_in[layer]
        w_lo = w[:, :f_start].astype(BF16)
        w_hi = w[:, f_start + n_f:].astype(BF16)
        w_f = jnp.pad(w[:, f_start:f_start + n_f], ((0, 0), (0, LANES - n_f))).astype(BF16)
        b_pad = jnp.pad(b_f[layer], (0, LANES - n_f)).reshape(1, LANES)

        proj, f = _in_proj(x2, pos, g_norm[layer].reshape(1, d), inv_full, w_lo, w_hi, w_f,
                           tm=1024, tn=wa_w, width_a=wa_w)
        c = _fox_gate(f, b_pad, batch=batch, seq=seq)
        oa = _attention(proj, None, moba=True, batch=batch, seq=seq, heads=heads_a,
                        q_col=0, k_col=wa_w // hb, v_col=2 * wa_w // hb)
        ob = _attention(proj, c, moba=False, batch=batch, seq=seq, heads=heads_b,
                        q_col=4 * wa_w // hb, k_col=(4 * wa_w + wb_w) // hb,
                        v_col=(4 * wa_w + 2 * wb_w) // hb)
        x2 = _out_merge(
            oa, ob, proj, x2, p[layer].reshape(t, -1),
            w_branch_a[layer].astype(BF16), w_branch_b[layer].astype(BF16),
            w_out[layer].astype(BF16), w_ple_gate[layer].astype(BF16),
            w_ple_up[layer].astype(BF16), g_ple[layer].reshape(1, d), g_final.reshape(1, d),
            tm=256, final=(layer == depth - 1),
            za_col=3, zb_col=(4 * wa_w + 3 * wb_w) // wb_w,
            ga_col=(4 * wa_w + 4 * wb_w) // d, gb_col=(4 * wa_w + 4 * wb_w) // d + 1)
    return x2.reshape(batch, seq, d)
```

```python
import functools
import math

import jax
import jax.numpy as jnp
from jax import lax
from jax.experimental import pallas as pl
from jax.experimental.pallas import tpu as pltpu

HEAD_DIM = 128
MOBA_BLOCK = 256
MOBA_TOPK = 3
ROPE_THETA = 10000.0
RMS_EPS = 1e-6

V7X_VMEM_BYTES = 64 * 1024 * 1024
VMEM_LIMIT_BYTES = V7X_VMEM_BYTES * 7 // 8
LANES = 128
BF16_SUBLANES = 16

F32 = jnp.float32
BF16 = jnp.bfloat16
NEG_INF = float("-inf")
MASKED = -1e30
LOG2E = math.log2(math.e)
NT_DIMS = (((1,), (1,)), ((), ()))


def _compiler_params(n_grid_dims):
    return pltpu.CompilerParams(
        dimension_semantics=("arbitrary",) * n_grid_dims,
        vmem_limit_bytes=VMEM_LIMIT_BYTES,
    )


def _resident(block_shape, index_map):
    return pl.BlockSpec(block_shape, index_map, pipeline_mode=pl.Buffered(1))


MXU_COLS = 256


def _in_proj_kernel(x_ref, pos_ref, g_ref, inv_ref, wlo_ref, whi_ref, wf_ref, proj_ref, f_ref,
                    h_scr, cos_scr, sin_scr, *, n_lo, qa_tile, ka_tile, qb_tile, q_scale,
                    row_tile):
    n = pl.program_id(1)
    half = HEAD_DIM // 2

    @pl.when(n == 0)
    def _normalize_and_tables():
        x = x_ref[...]
        ms = jnp.mean(x * x, axis=-1, keepdims=True)
        h = (x * lax.rsqrt(ms + RMS_EPS) * g_ref[...]).astype(BF16)
        h_scr[...] = h
        f_ref[...] = lax.dot_general(h, wf_ref[...], NT_DIMS, preferred_element_type=F32)
        lane = lax.broadcasted_iota(jnp.int32, (x.shape[0], HEAD_DIM), 1)
        first = lane < half
        ang = pos_ref[...].astype(F32) * inv_ref[...]
        cs = jnp.cos(jnp.where(first, ang, ang - 0.5 * math.pi))
        sc = pltpu.roll(cs, half, 1)
        cos_scr[...] = jnp.where(first, cs, sc)
        sin_scr[...] = jnp.where(first, -sc, cs)

    is_rope = jnp.logical_or(n == qa_tile, n == ka_tile)
    scale = jnp.where(jnp.logical_or(n == qa_tile, n == qb_tile), q_scale, 1.0).astype(F32)
    coef_c = jnp.where(is_rope, cos_scr[...], 1.0) * scale
    coef_s = jnp.where(is_rope, sin_scr[...], 0.0) * scale
    from_lo = n < n_lo
    tm = h_scr.shape[0]
    for r0 in range(0, tm, row_tile):
        rows = slice(r0, r0 + row_tile)
        h = h_scr[rows, :]
        for c0 in range(0, proj_ref.shape[1], MXU_COLS):
            w = jnp.where(from_lo, wlo_ref[c0:c0 + MXU_COLS, :], whi_ref[c0:c0 + MXU_COLS, :])
            acc = lax.dot_general(h, w, NT_DIMS, preferred_element_type=F32)
            for j in range(0, MXU_COLS, HEAD_DIM):
                a = acc[:, j:j + HEAD_DIM]
                r = a * coef_c[rows] + pltpu.roll(a, half, 1) * coef_s[rows]
                proj_ref[rows, c0 + j:c0 + j + HEAD_DIM] = r.astype(BF16)


def _in_proj(x2, pos, g, inv_full, w_lo_t, w_hi_t, w_f_t, *, tm, tn, width_a):
    t, d = x2.shape
    n_lo = w_lo_t.shape[0] // tn
    n_hi = w_hi_t.shape[0] // tn
    n_main = (n_lo + n_hi) * tn
    assert width_a == tn, "one column tile per attention section"
    assert w_lo_t.shape[0] == n_lo * tn and w_hi_t.shape[0] == n_hi * tn
    kern = functools.partial(
        _in_proj_kernel, n_lo=n_lo, qa_tile=0, ka_tile=1, qb_tile=4,
        q_scale=LOG2E / math.sqrt(HEAD_DIM), row_tile=512)
    return pl.pallas_call(
        kern,
        grid=(t // tm, n_main // tn),
        in_specs=[
            pl.BlockSpec((tm, d), lambda i, n: (i, 0)),
            pl.BlockSpec((tm, 1), lambda i, n: (i, 0)),
            _resident((1, d), lambda i, n: (0, 0)),
            _resident((1, LANES), lambda i, n: (0, 0)),
            pl.BlockSpec((tn, d), lambda i, n: (jnp.minimum(n, n_lo - 1), 0)),
            pl.BlockSpec((tn, d), lambda i, n: (jnp.maximum(n - n_lo, 0), 0)),
            _resident((LANES, d), lambda i, n: (0, 0)),
        ],
        out_specs=[
            pl.BlockSpec((tm, tn), lambda i, n: (i, n)),
            pl.BlockSpec((tm, LANES), lambda i, n: (i, 0)),
        ],
        out_shape=[
            jax.ShapeDtypeStruct((t, n_main), BF16),
            jax.ShapeDtypeStruct((t, LANES), F32),
        ],
        scratch_shapes=[
            pltpu.VMEM((tm, d), BF16),
            pltpu.VMEM((tm, LANES), F32),
            pltpu.VMEM((tm, LANES), F32),
        ],
        compiler_params=_compiler_params(2),
        name="in_proj",
    )(x2, pos, g, inv_full, w_lo_t, w_hi_t, w_f_t)


FOX_TERMS = 3


def _fox_gate_kernel(f_ref, b_ref, e_ref, *, chunk, n_heads):
    n_chunks = f_ref.shape[0] // chunk
    r = lax.broadcasted_iota(jnp.int32, (chunk, chunk), 0)
    c = lax.broadcasted_iota(jnp.int32, (chunk, chunk), 1)
    tri = (c <= r).astype(F32)
    z = f_ref[...] + b_ref[...]
    logf = (jnp.minimum(z, 0.0) - jnp.log1p(jnp.exp(-jnp.abs(z)))) * LOG2E
    wide = jnp.concatenate([logf[j * chunk:(j + 1) * chunk] for j in range(n_chunks)], axis=1)
    local = jnp.dot(tri, wide, precision=lax.Precision.HIGHEST, preferred_element_type=F32)
    lane = lax.broadcasted_iota(jnp.int32, (chunk, LANES), 1)
    offset = jnp.zeros((1, LANES), F32)
    for j in range(n_chunks):
        csum = local[:, j * LANES:(j + 1) * LANES] + offset
        offset = csum[chunk - 1:chunk, :]
        hi = csum.astype(BF16).astype(F32)
        mid = (csum - hi).astype(BF16).astype(F32)
        lo = csum - hi - mid
        feats = jnp.where(
            lane < n_heads, hi,
            jnp.where(lane < 2 * n_heads, pltpu.roll(mid, n_heads, 1),
                      jnp.where(lane < 3 * n_heads, pltpu.roll(lo, 2 * n_heads, 1), 0.0)))
        e_ref[j * chunk:(j + 1) * chunk, :] = feats.astype(BF16)


def _fox_gate(f, b_pad, *, batch, seq, n_heads):
    assert FOX_TERMS * n_heads <= LANES
    return pl.pallas_call(
        functools.partial(_fox_gate_kernel, chunk=MOBA_BLOCK, n_heads=n_heads),
        grid=(batch,),
        in_specs=[
            pl.BlockSpec((seq, LANES), lambda b: (b, 0)),
            _resident((1, LANES), lambda b: (0, 0)),
        ],
        out_specs=pl.BlockSpec((seq, LANES), lambda b: (b, 0)),
        out_shape=jax.ShapeDtypeStruct(f.shape, BF16),
        compiler_params=_compiler_params(1),
        name="fox_gate",
    )(f, b_pad)


ATTN_GROUP = 4
VT_ROWS = HEAD_DIM + BF16_SUBLANES
KEY_PAIR = 2 * MOBA_BLOCK


def _attn_kernel(*refs, moba, n_blocks, group, n_heads):
    if moba:
        (q_ref, k_ref, v_ref, o_ref,
         vt_scr, kaug_scr, qaug_scr, s_scr, smax_scr, m_scr, acc_scr, kbar_scr) = refs
    else:
        (q_ref, k_ref, v_ref, e_ref, o_ref,
         vt_scr, kaug_scr, qaug_scr, s_scr, smax_scr, m_scr, acc_scr) = refs
    blk = MOBA_BLOCK
    pair = KEY_PAIR
    hg = pl.program_id(1)
    own = pl.program_id(2)
    head_cols = [slice(g * HEAD_DIM, (g + 1) * HEAD_DIM) for g in range(group)]
    extra = slice(HEAD_DIM, 2 * HEAD_DIM)

    @pl.when(own == 0)
    def _per_group_setup():
        ones_row = (lax.broadcasted_iota(jnp.int32, (BF16_SUBLANES, pair), 0) == 0).astype(BF16)
        lane = lax.broadcasted_iota(jnp.int32, (pair, LANES), 1)
        row = lax.broadcasted_iota(jnp.int32, (pair, LANES), 0)

        def body(a, carry):
            off = pl.multiple_of(a * pair, pair)
            if moba:
                block_of_row = 2 * a + (row >= blk).astype(jnp.int32)
                feats = jnp.where(lane == block_of_row, 1.0, 0.0).astype(BF16)
            else:
                feats = e_ref[pl.ds(off, pair), :]
            for g, cols in enumerate(head_cols):
                vt_scr[g, a, :HEAD_DIM, :] = (
                    v_ref[pl.ds(off, pair), cols].astype(F32).T.astype(BF16))
                vt_scr[g, a, HEAD_DIM:, :] = ones_row
                kb = k_ref[pl.ds(off, pair), cols]
                kaug_scr[g, pl.ds(off, pair), :HEAD_DIM] = kb
                kaug_scr[g, pl.ds(off, pair), extra] = feats
                if moba:
                    kf = kb.astype(F32)
                    kbar_scr[g, pl.ds(2 * a, 1), :] = jnp.mean(kf[:blk], axis=0, keepdims=True)
                    kbar_scr[g, pl.ds(2 * a + 1, 1), :] = jnp.mean(kf[blk:], axis=0,
                                                                   keepdims=True)
            return carry
        lax.fori_loop(0, n_blocks // 2, body, 0)

    for g, cols in enumerate(head_cols):
        q = q_ref[:, cols]
        qaug_scr[g, :, :HEAD_DIM] = q
        if moba:
            gs = lax.dot_general(kbar_scr[g], q.astype(F32), NT_DIMS,
                                 precision=lax.Precision.HIGHEST,
                                 preferred_element_type=F32)
            blk_id = lax.broadcasted_iota(jnp.int32, gs.shape, 0)
            q_blk = 2 * own + (lax.broadcasted_iota(jnp.int32, gs.shape, 1) >= blk).astype(
                jnp.int32)
            gs = jnp.where(blk_id < q_blk, gs, NEG_INF)
            bias = jnp.where(blk_id == q_blk, 0.0, MASKED)
            for _ in range(MOBA_TOPK):
                mx = jnp.max(gs, axis=0, keepdims=True)
                first = jnp.min(jnp.where(gs == mx, blk_id, n_blocks), axis=0, keepdims=True)
                pick = jnp.logical_and(blk_id == first, mx > NEG_INF)
                bias = jnp.where(pick, 0.0, bias)
                gs = jnp.where(pick, NEG_INF, gs)
            bias = jnp.concatenate([bias, jnp.zeros((LANES - n_blocks, pair), F32)], axis=0)
            qaug_scr[g, :, extra] = bias.T.astype(BF16)
        else:
            lane = lax.broadcasted_iota(jnp.int32, (pair, LANES), 1)
            head = hg * group + g
            mine = functools.reduce(
                jnp.logical_or, [lane == head + j * n_heads for j in range(FOX_TERMS)])
            qaug_scr[g, :, extra] = jnp.where(mine, -1.0, 0.0).astype(BF16)

    def scores(g, a):
        off = pl.multiple_of(a * pair, pair)
        return lax.dot_general(kaug_scr[g, pl.ds(off, pair), :], qaug_scr[g], NT_DIMS,
                               preferred_element_type=F32)

    def stage_scores(slot, g, s):
        s_scr[slot, g] = s
        smax_scr[slot, g] = jnp.max(s, axis=0, keepdims=True)

    def fold_head(slot, g, a, first=False):
        m_new = smax_scr[slot, g]
        if not first:
            m_old = m_scr[g]
            m_new = jnp.maximum(m_old, m_new)
        p = jnp.exp2(s_scr[slot, g] - m_new).astype(BF16)
        pv = jnp.dot(vt_scr[g, a], p, preferred_element_type=F32)
        if not first:
            pv = jnp.exp2(m_old - m_new) * acc_scr[g] + pv
        acc_scr[g] = pv
        m_scr[g] = m_new

    visible = (lax.broadcasted_iota(jnp.int32, (pair, pair), 0)
               <= lax.broadcasted_iota(jnp.int32, (pair, pair), 1))
    for g in range(group):
        stage_scores(1, g, jnp.where(visible, scores(g, own), NEG_INF))
    for g in range(group):
        stage_scores(0, g, scores(g, 0))
        fold_head(1, g, own, first=True)

    def pipelined(slot_next, a_next, slot_cur, a_cur):
        for g in range(group):
            stage_scores(slot_next, g, scores(g, a_next))
            fold_head(slot_cur, g, a_cur)

    def two_pairs(t, carry):
        a0 = 2 * t
        pipelined(1, a0 + 1, 0, a0)
        pipelined(0, jnp.minimum(a0 + 2, own - 1), 1, a0 + 1)
        return carry

    lax.fori_loop(0, lax.shift_right_logical(own, 1), two_pairs, 0)

    @pl.when(jnp.bitwise_and(own, 1) == 1)
    def _last_odd_pair():
        for g in range(group):
            fold_head(0, g, own - 1)

    for g, cols in enumerate(head_cols):
        acc = acc_scr[g]
        o = acc[:HEAD_DIM] * (1.0 / acc[HEAD_DIM:HEAD_DIM + 1])
        o_ref[:, cols] = o.T.astype(o_ref.dtype)


def _attention(proj, e, *, moba, batch, seq, heads, q_col, k_col, v_col):
    group = ATTN_GROUP
    tq = KEY_PAIR
    nq = seq // tq
    n_blocks = seq // MOBA_BLOCK
    width = group * HEAD_DIM
    assert heads % group == 0 and seq % KEY_PAIR == 0 and n_blocks <= LANES
    assert q_col % group == 0 and k_col % group == 0 and v_col % group == 0
    in_specs = [
        pl.BlockSpec((tq, width), lambda b, h, i: (b * nq + i, q_col // group + h)),
        pl.BlockSpec((seq, width), lambda b, h, i: (b, k_col // group + h)),
        pl.BlockSpec((seq, width), lambda b, h, i: (b, v_col // group + h)),
    ]
    args = [proj, proj, proj]
    if not moba:
        in_specs.append(pl.BlockSpec((seq, LANES), lambda b, h, i: (b, 0)))
        args.append(e)
    scratch = [
        pltpu.VMEM((group, seq // KEY_PAIR, VT_ROWS, KEY_PAIR), BF16),
        pltpu.VMEM((group, seq, 2 * HEAD_DIM), BF16),
        pltpu.VMEM((group, tq, 2 * HEAD_DIM), BF16),
        pltpu.VMEM((2, group, KEY_PAIR, tq), F32),
        pltpu.VMEM((2, group, 1, tq), F32),
        pltpu.VMEM((group, 1, tq), F32),
        pltpu.VMEM((group, VT_ROWS, tq), F32),
    ]
    if moba:
        scratch.append(pltpu.VMEM((group, n_blocks, HEAD_DIM), F32))
    return pl.pallas_call(
        functools.partial(_attn_kernel, moba=moba, n_blocks=n_blocks, group=group,
                          n_heads=heads),
        grid=(batch, heads // group, nq),
        in_specs=in_specs,
        out_specs=pl.BlockSpec((tq, width), lambda b, h, i: (b * nq + i, h)),
        out_shape=jax.ShapeDtypeStruct((batch * seq, heads * HEAD_DIM), BF16),
        scratch_shapes=scratch,
        compiler_params=_compiler_params(3),
        name="moba_attention" if moba else "fox_attention",
    )(*args)


def _rmsnorm(x, g):
    ms = jnp.mean(x * x, axis=-1, keepdims=True)
    return x * lax.rsqrt(ms + RMS_EPS) * g


def _out_merge_kernel(oa_ref, ob_ref, za_ref, zb_ref, ga_ref, gb_ref, x_ref, p_ref,
                      wa_ref, wb_ref, wo_ref, wg_ref, wu_ref, gple_ref, gfin_ref, out_ref,
                      *, final):
    def branch(o_ref, z_ref, w_ref):
        z = z_ref[...].astype(F32)
        a = o_ref[...].astype(F32) * (z * jax.nn.sigmoid(z))
        return jnp.dot(a.astype(BF16), w_ref[...], preferred_element_type=F32)

    ya = branch(oa_ref, za_ref, wa_ref)
    yb = branch(ob_ref, zb_ref, wb_ref)
    mixed = (jax.nn.sigmoid(ga_ref[...].astype(F32)) * ya
             + jax.nn.sigmoid(gb_ref[...].astype(F32)) * yb)
    x1 = x_ref[...] + jnp.dot(mixed.astype(BF16), wo_ref[...], preferred_element_type=F32)
    hn = _rmsnorm(x1, gple_ref[...]).astype(BF16)
    pg = jax.nn.sigmoid(jnp.dot(hn, wg_ref[...], preferred_element_type=F32))
    up = jnp.dot(p_ref[...].astype(BF16), wu_ref[...], preferred_element_type=F32)
    x2 = x1 + up * pg
    if final:
        x2 = _rmsnorm(x2, gfin_ref[...])
    out_ref[...] = x2


def _out_merge(oa, ob, proj, x2, p2, wa, wb, wo, wg, wu, g_ple, g_final, *, tm, final,
               za_col, zb_col, ga_col, gb_col):
    t, d = x2.shape
    wa_w = oa.shape[1]
    wb_w = ob.shape[1]
    ple = p2.shape[1]
    row = lambda i: (i, 0)
    const = lambda i: (0, 0)
    return pl.pallas_call(
        functools.partial(_out_merge_kernel, final=final),
        grid=(t // tm,),
        in_specs=[
            pl.BlockSpec((tm, wa_w), row),
            pl.BlockSpec((tm, wb_w), row),
            pl.BlockSpec((tm, wa_w), lambda i: (i, za_col)),
            pl.BlockSpec((tm, wb_w), lambda i: (i, zb_col)),
            pl.BlockSpec((tm, d), lambda i: (i, ga_col)),
            pl.BlockSpec((tm, d), lambda i: (i, gb_col)),
            pl.BlockSpec((tm, d), row),
            pl.BlockSpec((tm, ple), row),
            _resident((wa_w, d), const),
            _resident((wb_w, d), const),
            _resident((d, d), const),
            _resident((d, d), const),
            _resident((ple, d), const),
            _resident((1, d), const),
            _resident((1, d), const),
        ],
        out_specs=pl.BlockSpec((tm, d), row),
        out_shape=jax.ShapeDtypeStruct((t, d), F32),
        compiler_params=_compiler_params(1),
        name="out_merge",
    )(oa, ob, proj, proj, proj, proj, x2, p2, wa, wb, wo, wg, wu, g_ple, g_final)


def kernel(x, p, positions, g_norm, w_in, b_f, w_branch_a, w_branch_b, w_out,
           g_ple, w_ple_gate, w_ple_up, g_final):
    batch, seq, d = x.shape
    depth = w_in.shape[0]
    t = batch * seq
    wa_w = w_branch_a.shape[1]
    wb_w = w_branch_b.shape[1]
    heads_a = wa_w // HEAD_DIM
    heads_b = wb_w // HEAD_DIM
    n_f = b_f.shape[1]
    assert seq % MOBA_BLOCK == 0 and wa_w == wb_w and 2 * wa_w == d
    f_start = 4 * wa_w + 3 * wb_w
    assert w_in.shape[2] == f_start + n_f + wb_w + 2 * d

    inv = ROPE_THETA ** (-jnp.arange(0, HEAD_DIM, 2, dtype=F32) / HEAD_DIM)
    inv_full = jnp.concatenate([inv, inv]).reshape(1, HEAD_DIM)
    pos = positions.reshape(t, 1)
    x2 = x.reshape(t, d)
    hb = HEAD_DIM

    for layer in range(depth):
        w_t = jnp.swapaxes(w_in[layer], 0, 1)
        w_lo_t = w_t[:f_start].astype(BF16)
        w_hi_t = w_t[f_start + n_f:].astype(BF16)
        w_f_t = jnp.pad(w_t[f_start:f_start + n_f], ((0, LANES - n_f), (0, 0))).astype(BF16)
        b_pad = jnp.pad(b_f[layer], (0, LANES - n_f)).reshape(1, LANES)

        proj, f = _in_proj(x2, pos, g_norm[layer].reshape(1, d), inv_full, w_lo_t, w_hi_t,
                           w_f_t, tm=1024, tn=wa_w, width_a=wa_w)
        e = _fox_gate(f, b_pad, batch=batch, seq=seq, n_heads=heads_b)
        oa = _attention(proj, None, moba=True, batch=batch, seq=seq, heads=heads_a,
                        q_col=0, k_col=wa_w // hb, v_col=2 * wa_w // hb)
        ob = _attention(proj, e, moba=False, batch=batch, seq=seq, heads=heads_b,
                        q_col=4 * wa_w // hb, k_col=(4 * wa_w + wb_w) // hb,
                        v_col=(4 * wa_w + 2 * wb_w) // hb)
        x2 = _out_merge(
            oa, ob, proj, x2, p[layer].reshape(t, -1),
            w_branch_a[layer].astype(BF16), w_branch_b[layer].astype(BF16),
            w_out[layer].astype(BF16), w_ple_gate[layer].astype(BF16),
            w_ple_up[layer].astype(BF16), g_ple[layer].reshape(1, d), g_final.reshape(1, d),
            tm=256, final=(layer == depth - 1),
            za_col=3, zb_col=(4 * wa_w + 3 * wb_w) // wb_w,
            ga_col=(4 * wa_w + 4 * wb_w) // d, gb_col=(4 * wa_w + 4 * wb_w) // d + 1)
    return x2.reshape(batch, seq, d)
```

```python
import functools
import math

import jax
import jax.numpy as jnp
from jax import lax
from jax.experimental import pallas as pl
from jax.experimental.pallas import tpu as pltpu

HEAD_DIM = 128
MOBA_BLOCK = 256
MOBA_TOPK = 3
ROPE_THETA = 10000.0
RMS_EPS = 1e-6

V7X_VMEM_BYTES = 64 * 1024 * 1024
VMEM_LIMIT_BYTES = V7X_VMEM_BYTES * 7 // 8
LANES = 128
BF16_SUBLANES = 16

F32 = jnp.float32
BF16 = jnp.bfloat16
NEG_INF = float("-inf")
MASKED = -1e30
LOG2E = math.log2(math.e)
NT_DIMS = (((1,), (1,)), ((), ()))


def _compiler_params(n_grid_dims):
    return pltpu.CompilerParams(
        dimension_semantics=("arbitrary",) * n_grid_dims,
        vmem_limit_bytes=VMEM_LIMIT_BYTES,
    )


def _resident(block_shape, index_map):
    return pl.BlockSpec(block_shape, index_map, pipeline_mode=pl.Buffered(1))


F32_SUBLANES = 8
PREP_COLS = 512


def _w_in_prep_kernel(wt_hbm, out_ref, wf_ref, buf, fbuf, sem, fsem, *, n_lo, n_f):
    n = pl.program_id(0)
    tn = out_ref.shape[1]

    def fetch(j, slot):
        first_row = pl.multiple_of(j * tn + jnp.where(j >= n_lo, n_f, 0), F32_SUBLANES)
        return pltpu.make_async_copy(wt_hbm.at[pl.ds(first_row, tn), :], buf.at[slot],
                                     sem.at[slot])

    slot = jnp.bitwise_and(n, 1)

    @pl.when(n == 0)
    def _first():
        fetch(0, 0).start()
        gate_rows = pltpu.make_async_copy(wt_hbm.at[pl.ds(n_lo * tn, n_f), :], fbuf, fsem)
        gate_rows.start()
        gate_rows.wait()
        padded = jnp.concatenate(
            [fbuf[...], jnp.zeros((LANES - n_f, fbuf.shape[1]), F32)], axis=0)
        wf_ref[...] = padded.T.astype(BF16)

    @pl.when(n + 1 < pl.num_programs(0))
    def _ahead():
        fetch(n + 1, 1 - slot).start()

    fetch(n, slot).wait()
    for c0 in range(0, buf.shape[2], PREP_COLS):
        out_ref[c0:c0 + PREP_COLS, :] = buf[slot, :, c0:c0 + PREP_COLS].T.astype(BF16)


def _w_in_prep(w_t, *, f_start, n_f, tn):
    n_in, d = w_t.shape
    n_lo = f_start // tn
    n_tiles = (n_in - n_f) // tn
    assert f_start == n_lo * tn and n_in == n_tiles * tn + n_f
    assert n_f % F32_SUBLANES == 0 and n_f <= LANES and d % PREP_COLS == 0
    return pl.pallas_call(
        functools.partial(_w_in_prep_kernel, n_lo=n_lo, n_f=n_f),
        grid=(n_tiles,),
        in_specs=[pl.BlockSpec(memory_space=pl.ANY)],
        out_specs=[pl.BlockSpec((d, tn), lambda n: (0, n)),
                   pl.BlockSpec((d, LANES), lambda n: (0, 0))],
        out_shape=[jax.ShapeDtypeStruct((d, n_tiles * tn), BF16),
                   jax.ShapeDtypeStruct((d, LANES), BF16)],
        scratch_shapes=[pltpu.VMEM((2, tn, d), F32), pltpu.VMEM((n_f, d), F32),
                        pltpu.SemaphoreType.DMA((2,)), pltpu.SemaphoreType.DMA(())],
        compiler_params=_compiler_params(1),
        name="w_in_prep",
    )(w_t)


MXU_COLS = 256


def _in_proj_kernel(x_ref, pos_ref, g_ref, inv_ref, w_ref, wf_ref, proj_ref, f_ref,
                    h_scr, cos_scr, sin_scr, *, qa_tile, ka_tile, qb_tile, q_scale, row_tile):
    n = pl.program_id(1)
    half = HEAD_DIM // 2

    @pl.when(n == 0)
    def _normalize_and_tables():
        x = x_ref[...]
        ms = jnp.mean(x * x, axis=-1, keepdims=True)
        h = (x * lax.rsqrt(ms + RMS_EPS) * g_ref[...]).astype(BF16)
        h_scr[...] = h
        f_ref[...] = jnp.dot(h, wf_ref[...], preferred_element_type=F32)
        lane = lax.broadcasted_iota(jnp.int32, (x.shape[0], HEAD_DIM), 1)
        first = lane < half
        ang = pos_ref[...].astype(F32) * inv_ref[...]
        cs = jnp.cos(jnp.where(first, ang, ang - 0.5 * math.pi))
        sc = pltpu.roll(cs, half, 1)
        cos_scr[...] = jnp.where(first, cs, sc)
        sin_scr[...] = jnp.where(first, -sc, cs)

    is_rope = jnp.logical_or(n == qa_tile, n == ka_tile)
    scale = jnp.where(jnp.logical_or(n == qa_tile, n == qb_tile), q_scale, 1.0).astype(F32)
    coef_c = jnp.where(is_rope, cos_scr[...], 1.0) * scale
    coef_s = jnp.where(is_rope, sin_scr[...], 0.0) * scale
    tm = h_scr.shape[0]
    for r0 in range(0, tm, row_tile):
        rows = slice(r0, r0 + row_tile)
        h = h_scr[rows, :]
        for c0 in range(0, proj_ref.shape[1], MXU_COLS):
            acc = jnp.dot(h, w_ref[:, c0:c0 + MXU_COLS],
                          preferred_element_type=F32)
            for j in range(0, MXU_COLS, HEAD_DIM):
                a = acc[:, j:j + HEAD_DIM]
                r = a * coef_c[rows] + pltpu.roll(a, half, 1) * coef_s[rows]
                proj_ref[rows, c0 + j:c0 + j + HEAD_DIM] = r.astype(BF16)


def _in_proj(x2, pos, g, inv_full, w_main, w_f, *, tm, tn, width_a):
    t, d = x2.shape
    n_main = w_main.shape[1]
    assert width_a == tn, "one column tile per attention section"
    kern = functools.partial(
        _in_proj_kernel, qa_tile=0, ka_tile=1, qb_tile=4,
        q_scale=LOG2E / math.sqrt(HEAD_DIM), row_tile=512)
    return pl.pallas_call(
        kern,
        grid=(t // tm, n_main // tn),
        in_specs=[
            pl.BlockSpec((tm, d), lambda i, n: (i, 0)),
            pl.BlockSpec((tm, 1), lambda i, n: (i, 0)),
            _resident((1, d), lambda i, n: (0, 0)),
            _resident((1, LANES), lambda i, n: (0, 0)),
            pl.BlockSpec((d, tn), lambda i, n: (0, n)),
            _resident((d, LANES), lambda i, n: (0, 0)),
        ],
        out_specs=[
            pl.BlockSpec((tm, tn), lambda i, n: (i, n)),
            pl.BlockSpec((tm, LANES), lambda i, n: (i, 0)),
        ],
        out_shape=[
            jax.ShapeDtypeStruct((t, n_main), BF16),
            jax.ShapeDtypeStruct((t, LANES), F32),
        ],
        scratch_shapes=[
            pltpu.VMEM((tm, d), BF16),
            pltpu.VMEM((tm, LANES), F32),
            pltpu.VMEM((tm, LANES), F32),
        ],
        compiler_params=_compiler_params(2),
        name="in_proj",
    )(x2, pos, g, inv_full, w_main, w_f)


FOX_TERMS = 3


def _fox_gate_kernel(f_ref, b_ref, e_ref, *, chunk, n_heads):
    n_chunks = f_ref.shape[0] // chunk
    r = lax.broadcasted_iota(jnp.int32, (chunk, chunk), 0)
    c = lax.broadcasted_iota(jnp.int32, (chunk, chunk), 1)
    tri = (c <= r).astype(F32)
    z = f_ref[...] + b_ref[...]
    logf = (jnp.minimum(z, 0.0) - jnp.log1p(jnp.exp(-jnp.abs(z)))) * LOG2E
    wide = jnp.concatenate([logf[j * chunk:(j + 1) * chunk] for j in range(n_chunks)], axis=1)
    local = jnp.dot(tri, wide, precision=lax.Precision.HIGHEST, preferred_element_type=F32)
    lane = lax.broadcasted_iota(jnp.int32, (chunk, LANES), 1)
    offset = jnp.zeros((1, LANES), F32)
    for j in range(n_chunks):
        csum = local[:, j * LANES:(j + 1) * LANES] + offset
        offset = csum[chunk - 1:chunk, :]
        hi = csum.astype(BF16).astype(F32)
        mid = (csum - hi).astype(BF16).astype(F32)
        lo = csum - hi - mid
        feats = jnp.where(
            lane < n_heads, hi,
            jnp.where(lane < 2 * n_heads, pltpu.roll(mid, n_heads, 1),
                      jnp.where(lane < 3 * n_heads, pltpu.roll(lo, 2 * n_heads, 1), 0.0)))
        e_ref[j * chunk:(j + 1) * chunk, :] = feats.astype(BF16)


def _fox_gate(f, b_pad, *, batch, seq, n_heads):
    assert FOX_TERMS * n_heads <= LANES
    return pl.pallas_call(
        functools.partial(_fox_gate_kernel, chunk=MOBA_BLOCK, n_heads=n_heads),
        grid=(batch,),
        in_specs=[
            pl.BlockSpec((seq, LANES), lambda b: (b, 0)),
            _resident((1, LANES), lambda b: (0, 0)),
        ],
        out_specs=pl.BlockSpec((seq, LANES), lambda b: (b, 0)),
        out_shape=jax.ShapeDtypeStruct(f.shape, BF16),
        compiler_params=_compiler_params(1),
        name="fox_gate",
    )(f, b_pad)


ATTN_GROUP = 4
VT_ROWS = HEAD_DIM + BF16_SUBLANES
KEY_PAIR = 2 * MOBA_BLOCK


def _attn_kernel(*refs, moba, n_blocks, group, n_heads):
    if moba:
        (q_ref, k_ref, v_ref, o_ref,
         vt_scr, kaug_scr, qaug_scr, s_scr, smax_scr, m_scr, acc_scr, kbar_scr) = refs
    else:
        (q_ref, k_ref, v_ref, e_ref, o_ref,
         vt_scr, kaug_scr, qaug_scr, s_scr, smax_scr, m_scr, acc_scr) = refs
    blk = MOBA_BLOCK
    pair = KEY_PAIR
    hg = pl.program_id(1)
    own = pl.program_id(2)
    head_cols = [slice(g * HEAD_DIM, (g + 1) * HEAD_DIM) for g in range(group)]
    extra = slice(HEAD_DIM, 2 * HEAD_DIM)

    @pl.when(own == 0)
    def _per_group_setup():
        ones_row = (lax.broadcasted_iota(jnp.int32, (BF16_SUBLANES, pair), 0) == 0).astype(BF16)
        lane = lax.broadcasted_iota(jnp.int32, (pair, LANES), 1)
        row = lax.broadcasted_iota(jnp.int32, (pair, LANES), 0)

        def body(a, carry):
            off = pl.multiple_of(a * pair, pair)
            if moba:
                block_of_row = 2 * a + (row >= blk).astype(jnp.int32)
                feats = jnp.where(lane == block_of_row, 1.0, 0.0).astype(BF16)
            else:
                feats = e_ref[pl.ds(off, pair), :]
            for g, cols in enumerate(head_cols):
                vt_scr[g, a, :HEAD_DIM, :] = (
                    v_ref[pl.ds(off, pair), cols].astype(F32).T.astype(BF16))
                vt_scr[g, a, HEAD_DIM:, :] = ones_row
                kb = k_ref[pl.ds(off, pair), cols]
                kaug_scr[g, pl.ds(off, pair), :HEAD_DIM] = kb
                kaug_scr[g, pl.ds(off, pair), extra] = feats
                if moba:
                    kf = kb.astype(F32)
                    kbar_scr[g, pl.ds(2 * a, 1), :] = jnp.mean(kf[:blk], axis=0, keepdims=True)
                    kbar_scr[g, pl.ds(2 * a + 1, 1), :] = jnp.mean(kf[blk:], axis=0,
                                                                   keepdims=True)
            return carry
        lax.fori_loop(0, n_blocks // 2, body, 0)

    for g, cols in enumerate(head_cols):
        q = q_ref[:, cols]
        qaug_scr[g, :, :HEAD_DIM] = q
        if moba:
            gs = lax.dot_general(kbar_scr[g], q.astype(F32), NT_DIMS,
                                 precision=lax.Precision.HIGHEST,
                                 preferred_element_type=F32)
            blk_id = lax.broadcasted_iota(jnp.int32, gs.shape, 0)
            q_blk = 2 * own + (lax.broadcasted_iota(jnp.int32, gs.shape, 1) >= blk).astype(
                jnp.int32)
            gs = jnp.where(blk_id < q_blk, gs, NEG_INF)
            bias = jnp.where(blk_id == q_blk, 0.0, MASKED)
            for _ in range(MOBA_TOPK):
                mx = jnp.max(gs, axis=0, keepdims=True)
                first = jnp.min(jnp.where(gs == mx, blk_id, n_blocks), axis=0, keepdims=True)
                pick = jnp.logical_and(blk_id == first, mx > NEG_INF)
                bias = jnp.where(pick, 0.0, bias)
                gs = jnp.where(pick, NEG_INF, gs)
            bias = jnp.concatenate([bias, jnp.zeros((LANES - n_blocks, pair), F32)], axis=0)
            qaug_scr[g, :, extra] = bias.T.astype(BF16)
        else:
            lane = lax.broadcasted_iota(jnp.int32, (pair, LANES), 1)
            head = hg * group + g
            mine = functools.reduce(
                jnp.logical_or, [lane == head + j * n_heads for j in range(FOX_TERMS)])
            qaug_scr[g, :, extra] = jnp.where(mine, -1.0, 0.0).astype(BF16)

    def scores(g, a):
        off = pl.multiple_of(a * pair, pair)
        return lax.dot_general(kaug_scr[g, pl.ds(off, pair), :], qaug_scr[g], NT_DIMS,
                               preferred_element_type=F32)

    def stage_scores(slot, g, s):
        s_scr[slot, g] = s
        smax_scr[slot, g] = jnp.max(s, axis=0, keepdims=True)

    def fold_head(slot, g, a, first=False):
        m_new = smax_scr[slot, g]
        if not first:
            m_old = m_scr[g]
            m_new = jnp.maximum(m_old, m_new)
        p = jnp.exp2(s_scr[slot, g] - m_new).astype(BF16)
        pv = jnp.dot(vt_scr[g, a], p, preferred_element_type=F32)
        if not first:
            pv = jnp.exp2(m_old - m_new) * acc_scr[g] + pv
        acc_scr[g] = pv
        m_scr[g] = m_new

    visible = (lax.broadcasted_iota(jnp.int32, (pair, pair), 0)
               <= lax.broadcasted_iota(jnp.int32, (pair, pair), 1))
    for g in range(group):
        stage_scores(1, g, jnp.where(visible, scores(g, own), NEG_INF))
    for g in range(group):
        stage_scores(0, g, scores(g, 0))
        fold_head(1, g, own, first=True)

    def pipelined(slot_next, a_next, slot_cur, a_cur):
        for g in range(group):
            stage_scores(slot_next, g, scores(g, a_next))
            fold_head(slot_cur, g, a_cur)

    def two_pairs(t, carry):
        a0 = 2 * t
        pipelined(1, a0 + 1, 0, a0)
        pipelined(0, jnp.minimum(a0 + 2, own - 1), 1, a0 + 1)
        return carry

    lax.fori_loop(0, lax.shift_right_logical(own, 1), two_pairs, 0)

    @pl.when(jnp.bitwise_and(own, 1) == 1)
    def _last_odd_pair():
        for g in range(group):
            fold_head(0, g, own - 1)

    for g, cols in enumerate(head_cols):
        acc = acc_scr[g]
        o = acc[:HEAD_DIM] * (1.0 / acc[HEAD_DIM:HEAD_DIM + 1])
        o_ref[:, cols] = o.T.astype(o_ref.dtype)


def _attention(proj, e, *, moba, batch, seq, heads, q_col, k_col, v_col):
    group = ATTN_GROUP
    tq = KEY_PAIR
    nq = seq // tq
    n_blocks = seq // MOBA_BLOCK
    width = group * HEAD_DIM
    assert heads % group == 0 and seq % KEY_PAIR == 0 and n_blocks <= LANES
    assert q_col % group == 0 and k_col % group == 0 and v_col % group == 0
    in_specs = [
        pl.BlockSpec((tq, width), lambda b, h, i: (b * nq + i, q_col // group + h)),
        pl.BlockSpec((seq, width), lambda b, h, i: (b, k_col // group + h)),
        pl.BlockSpec((seq, width), lambda b, h, i: (b, v_col // group + h)),
    ]
    args = [proj, proj, proj]
    if not moba:
        in_specs.append(pl.BlockSpec((seq, LANES), lambda b, h, i: (b, 0)))
        args.append(e)
    scratch = [
        pltpu.VMEM((group, seq // KEY_PAIR, VT_ROWS, KEY_PAIR), BF16),
        pltpu.VMEM((group, seq, 2 * HEAD_DIM), BF16),
        pltpu.VMEM((group, tq, 2 * HEAD_DIM), BF16),
        pltpu.VMEM((2, group, KEY_PAIR, tq), F32),
        pltpu.VMEM((2, group, 1, tq), F32),
        pltpu.VMEM((group, 1, tq), F32),
        pltpu.VMEM((group, VT_ROWS, tq), F32),
    ]
    if moba:
        scratch.append(pltpu.VMEM((group, n_blocks, HEAD_DIM), F32))
    return pl.pallas_call(
        functools.partial(_attn_kernel, moba=moba, n_blocks=n_blocks, group=group,
                          n_heads=heads),
        grid=(batch, heads // group, nq),
        in_specs=in_specs,
        out_specs=pl.BlockSpec((tq, width), lambda b, h, i: (b * nq + i, h)),
        out_shape=jax.ShapeDtypeStruct((batch * seq, heads * HEAD_DIM), BF16),
        scratch_shapes=scratch,
        compiler_params=_compiler_params(3),
        name="moba_attention" if moba else "fox_attention",
    )(*args)


def _rmsnorm(x, g):
    ms = jnp.mean(x * x, axis=-1, keepdims=True)
    return x * lax.rsqrt(ms + RMS_EPS) * g


def _out_merge_kernel(oa_ref, ob_ref, za_ref, zb_ref, ga_ref, gb_ref, x_ref, p_ref,
                      wa_ref, wb_ref, wo_ref, wg_ref, wu_ref, gple_ref, gfin_ref, out_ref,
                      *, final):
    def branch(o_ref, z_ref, w_ref):
        z = z_ref[...].astype(F32)
        a = o_ref[...].astype(F32) * (z * jax.nn.sigmoid(z))
        return jnp.dot(a.astype(BF16), w_ref[...], preferred_element_type=F32)

    ya = branch(oa_ref, za_ref, wa_ref)
    yb = branch(ob_ref, zb_ref, wb_ref)
    mixed = (jax.nn.sigmoid(ga_ref[...].astype(F32)) * ya
             + jax.nn.sigmoid(gb_ref[...].astype(F32)) * yb)
    x1 = x_ref[...] + jnp.dot(mixed.astype(BF16), wo_ref[...], preferred_element_type=F32)
    hn = _rmsnorm(x1, gple_ref[...]).astype(BF16)
    pg = jax.nn.sigmoid(jnp.dot(hn, wg_ref[...], preferred_element_type=F32))
    up = jnp.dot(p_ref[...].astype(BF16), wu_ref[...], preferred_element_type=F32)
    x2 = x1 + up * pg
    if final:
        x2 = _rmsnorm(x2, gfin_ref[...])
    out_ref[...] = x2


def _out_merge(oa, ob, proj, x2, p2, wa, wb, wo, wg, wu, g_ple, g_final, *, tm, final,
               za_col, zb_col, ga_col, gb_col):
    t, d = x2.shape
    wa_w = oa.shape[1]
    wb_w = ob.shape[1]
    ple = p2.shape[1]
    row = lambda i: (i, 0)
    const = lambda i: (0, 0)
    return pl.pallas_call(
        functools.partial(_out_merge_kernel, final=final),
        grid=(t // tm,),
        in_specs=[
            pl.BlockSpec((tm, wa_w), row),
            pl.BlockSpec((tm, wb_w), row),
            pl.BlockSpec((tm, wa_w), lambda i: (i, za_col)),
            pl.BlockSpec((tm, wb_w), lambda i: (i, zb_col)),
            pl.BlockSpec((tm, d), lambda i: (i, ga_col)),
            pl.BlockSpec((tm, d), lambda i: (i, gb_col)),
            pl.BlockSpec((tm, d), row),
            pl.BlockSpec((tm, ple), row),
            _resident((wa_w, d), const),
            _resident((wb_w, d), const),
            _resident((d, d), const),
            _resident((d, d), const),
            _resident((ple, d), const),
            _resident((1, d), const),
            _resident((1, d), const),
        ],
        out_specs=pl.BlockSpec((tm, d), row),
        out_shape=jax.ShapeDtypeStruct((t, d), F32),
        compiler_params=_compiler_params(1),
        name="out_merge",
    )(oa, ob, proj, proj, proj, proj, x2, p2, wa, wb, wo, wg, wu, g_ple, g_final)


def kernel(x, p, positions, g_norm, w_in, b_f, w_branch_a, w_branch_b, w_out,
           g_ple, w_ple_gate, w_ple_up, g_final):
    batch, seq, d = x.shape
    depth = w_in.shape[0]
    t = batch * seq
    wa_w = w_branch_a.shape[1]
    wb_w = w_branch_b.shape[1]
    heads_a = wa_w // HEAD_DIM
    heads_b = wb_w // HEAD_DIM
    n_f = b_f.shape[1]
    assert seq % MOBA_BLOCK == 0 and wa_w == wb_w and 2 * wa_w == d
    f_start = 4 * wa_w + 3 * wb_w
    assert w_in.shape[2] == f_start + n_f + wb_w + 2 * d

    inv = ROPE_THETA ** (-jnp.arange(0, HEAD_DIM, 2, dtype=F32) / HEAD_DIM)
    inv_full = jnp.concatenate([inv, inv]).reshape(1, HEAD_DIM)
    pos = positions.reshape(t, 1)
    x2 = x.reshape(t, d)
    hb = HEAD_DIM

    for layer in range(depth):
        w_t = jnp.swapaxes(w_in[layer], 0, 1)
        w_main, w_f = _w_in_prep(w_t, f_start=f_start, n_f=n_f, tn=wa_w)
        b_pad = jnp.pad(b_f[layer], (0, LANES - n_f)).reshape(1, LANES)

        proj, f = _in_proj(x2, pos, g_norm[layer].reshape(1, d), inv_full, w_main, w_f,
                           tm=1024, tn=wa_w, width_a=wa_w)
        e = _fox_gate(f, b_pad, batch=batch, seq=seq, n_heads=heads_b)
        oa = _attention(proj, None, moba=True, batch=batch, seq=seq, heads=heads_a,
                        q_col=0, k_col=wa_w // hb, v_col=2 * wa_w // hb)
        ob = _attention(proj, e, moba=False, batch=batch, seq=seq, heads=heads_b,
                        q_col=4 * wa_w // hb, k_col=(4 * wa_w + wb_w) // hb,
                        v_col=(4 * wa_w + 2 * wb_w) // hb)
        x2 = _out_merge(
            oa, ob, proj, x2, p[layer].reshape(t, -1),
            w_branch_a[layer].astype(BF16), w_branch_b[layer].astype(BF16),
            w_out[layer].astype(BF16), w_ple_gate[layer].astype(BF16),
            w_ple_up[layer].astype(BF16), g_ple[layer].reshape(1, d), g_final.reshape(1, d),
            tm=256, final=(layer == depth - 1),
            za_col=3, zb_col=(4 * wa_w + 3 * wb_w) // wb_w,
            ga_col=(4 * wa_w + 4 * wb_w) // d, gb_col=(4 * wa_w + 4 * wb_w) // d + 1)
    return x2.reshape(batch, seq, d)
```

```python
import functools
import math

import jax
import jax.numpy as jnp
from jax import lax
from jax.experimental import pallas as pl
from jax.experimental.pallas import tpu as pltpu

HEAD_DIM = 128
MOBA_BLOCK = 256
MOBA_TOPK = 3
ROPE_THETA = 10000.0
RMS_EPS = 1e-6

V7X_VMEM_BYTES = 64 * 1024 * 1024
VMEM_LIMIT_BYTES = V7X_VMEM_BYTES * 7 // 8
LANES = 128
BF16_SUBLANES = 16

F32 = jnp.float32
BF16 = jnp.bfloat16
NEG_INF = float("-inf")
MASKED = -1e30
LOG2E = math.log2(math.e)
NT_DIMS = (((1,), (1,)), ((), ()))


def _compiler_params(n_grid_dims):
    return pltpu.CompilerParams(
        dimension_semantics=("arbitrary",) * n_grid_dims,
        vmem_limit_bytes=VMEM_LIMIT_BYTES,
    )


def _resident(block_shape, index_map):
    return pl.BlockSpec(block_shape, index_map, pipeline_mode=pl.Buffered(1))


F32_SUBLANES = 8
PREP_COLS = 512


def _w_in_prep_kernel(wt_hbm, out_ref, wf_ref, buf, fbuf, sem, fsem, *, n_lo, n_f):
    n = pl.program_id(0)
    tn = out_ref.shape[1]

    def fetch(j, slot):
        first_row = pl.multiple_of(j * tn + jnp.where(j >= n_lo, n_f, 0), F32_SUBLANES)
        return pltpu.make_async_copy(wt_hbm.at[pl.ds(first_row, tn), :], buf.at[slot],
                                     sem.at[slot])

    slot = jnp.bitwise_and(n, 1)

    @pl.when(n == 0)
    def _first():
        fetch(0, 0).start()
        gate_rows = pltpu.make_async_copy(wt_hbm.at[pl.ds(n_lo * tn, n_f), :], fbuf, fsem)
        gate_rows.start()
        gate_rows.wait()
        padded = jnp.concatenate(
            [fbuf[...], jnp.zeros((LANES - n_f, fbuf.shape[1]), F32)], axis=0)
        wf_ref[...] = padded.T.astype(BF16)

    @pl.when(n + 1 < pl.num_programs(0))
    def _ahead():
        fetch(n + 1, 1 - slot).start()

    fetch(n, slot).wait()
    for c0 in range(0, buf.shape[2], PREP_COLS):
        out_ref[c0:c0 + PREP_COLS, :] = buf[slot, :, c0:c0 + PREP_COLS].T.astype(BF16)


def _w_in_prep(w_t, *, f_start, n_f, tn):
    n_in, d = w_t.shape
    n_lo = f_start // tn
    n_tiles = (n_in - n_f) // tn
    assert f_start == n_lo * tn and n_in == n_tiles * tn + n_f
    assert n_f % F32_SUBLANES == 0 and n_f <= LANES and d % PREP_COLS == 0
    return pl.pallas_call(
        functools.partial(_w_in_prep_kernel, n_lo=n_lo, n_f=n_f),
        grid=(n_tiles,),
        in_specs=[pl.BlockSpec(memory_space=pl.ANY)],
        out_specs=[pl.BlockSpec((d, tn), lambda n: (0, n)),
                   pl.BlockSpec((d, LANES), lambda n: (0, 0))],
        out_shape=[jax.ShapeDtypeStruct((d, n_tiles * tn), BF16),
                   jax.ShapeDtypeStruct((d, LANES), BF16)],
        scratch_shapes=[pltpu.VMEM((2, tn, d), F32), pltpu.VMEM((n_f, d), F32),
                        pltpu.SemaphoreType.DMA((2,)), pltpu.SemaphoreType.DMA(())],
        compiler_params=_compiler_params(1),
        name="w_in_prep",
    )(w_t)


MXU_COLS = 256


def _in_proj_kernel(x_ref, pos_ref, g_ref, inv_ref, w_ref, wf_ref, proj_ref, f_ref,
                    h_scr, cos_scr, sin_scr, *, section, rope_sections, scaled_sections,
                    q_scale, row_tile):
    n = pl.program_id(1)
    half = HEAD_DIM // 2

    @pl.when(n == 0)
    def _normalize_and_tables():
        x = x_ref[...]
        ms = jnp.mean(x * x, axis=-1, keepdims=True)
        h = (x * lax.rsqrt(ms + RMS_EPS) * g_ref[...]).astype(BF16)
        h_scr[...] = h
        f_ref[...] = jnp.dot(h, wf_ref[...], preferred_element_type=F32)
        lane = lax.broadcasted_iota(jnp.int32, (x.shape[0], HEAD_DIM), 1)
        first = lane < half
        ang = pos_ref[...].astype(F32) * inv_ref[...]
        cs = jnp.cos(jnp.where(first, ang, ang - 0.5 * math.pi))
        sc = pltpu.roll(cs, half, 1)
        cos_scr[...] = jnp.where(first, cs, sc)
        sin_scr[...] = jnp.where(first, -sc, cs)

    tn = proj_ref.shape[1]
    coef_c, coef_s = [], []
    for k in range(tn // section):
        sec = n * (tn // section) + k
        is_rope = functools.reduce(jnp.logical_or, [sec == s for s in rope_sections])
        scaled = functools.reduce(jnp.logical_or, [sec == s for s in scaled_sections])
        scale = jnp.where(scaled, q_scale, 1.0).astype(F32)
        coef_c.append(jnp.where(is_rope, cos_scr[...], 1.0) * scale)
        coef_s.append(jnp.where(is_rope, sin_scr[...], 0.0) * scale)
    tm = h_scr.shape[0]
    for r0 in range(0, tm, row_tile):
        rows = slice(r0, r0 + row_tile)
        h = h_scr[rows, :]
        for c0 in range(0, tn, MXU_COLS):
            acc = jnp.dot(h, w_ref[:, c0:c0 + MXU_COLS],
                          preferred_element_type=F32)
            cc, cs = coef_c[c0 // section][rows], coef_s[c0 // section][rows]
            for j in range(0, MXU_COLS, HEAD_DIM):
                a = acc[:, j:j + HEAD_DIM]
                r = a * cc + pltpu.roll(a, half, 1) * cs
                proj_ref[rows, c0 + j:c0 + j + HEAD_DIM] = r.astype(BF16)


def _in_proj(x2, pos, g, inv_full, w_main, w_f, *, tm, tn, section):
    t, d = x2.shape
    n_main = w_main.shape[1]
    assert tn % section == 0 and n_main % tn == 0 and section % MXU_COLS == 0
    kern = functools.partial(
        _in_proj_kernel, section=section, rope_sections=(0, 1), scaled_sections=(0, 4),
        q_scale=LOG2E / math.sqrt(HEAD_DIM), row_tile=512)
    return pl.pallas_call(
        kern,
        grid=(t // tm, n_main // tn),
        in_specs=[
            pl.BlockSpec((tm, d), lambda i, n: (i, 0)),
            pl.BlockSpec((tm, 1), lambda i, n: (i, 0)),
            _resident((1, d), lambda i, n: (0, 0)),
            _resident((1, LANES), lambda i, n: (0, 0)),
            pl.BlockSpec((d, tn), lambda i, n: (0, n)),
            _resident((d, LANES), lambda i, n: (0, 0)),
        ],
        out_specs=[
            pl.BlockSpec((tm, tn), lambda i, n: (i, n)),
            pl.BlockSpec((tm, LANES), lambda i, n: (i, 0)),
        ],
        out_shape=[
            jax.ShapeDtypeStruct((t, n_main), BF16),
            jax.ShapeDtypeStruct((t, LANES), F32),
        ],
        scratch_shapes=[
            pltpu.VMEM((tm, d), BF16),
            pltpu.VMEM((tm, LANES), F32),
            pltpu.VMEM((tm, LANES), F32),
        ],
        compiler_params=_compiler_params(2),
        name="in_proj",
    )(x2, pos, g, inv_full, w_main, w_f)


FOX_TERMS = 3


def _fox_gate_kernel(f_ref, b_ref, e_ref, *, chunk, n_heads):
    n_chunks = f_ref.shape[0] // chunk
    r = lax.broadcasted_iota(jnp.int32, (chunk, chunk), 0)
    c = lax.broadcasted_iota(jnp.int32, (chunk, chunk), 1)
    tri = (c <= r).astype(F32)
    z = f_ref[...] + b_ref[...]
    logf = (jnp.minimum(z, 0.0) - jnp.log1p(jnp.exp(-jnp.abs(z)))) * LOG2E
    wide = jnp.concatenate([logf[j * chunk:(j + 1) * chunk] for j in range(n_chunks)], axis=1)
    local = jnp.dot(tri, wide, precision=lax.Precision.HIGHEST, preferred_element_type=F32)
    lane = lax.broadcasted_iota(jnp.int32, (chunk, LANES), 1)
    offset = jnp.zeros((1, LANES), F32)
    for j in range(n_chunks):
        csum = local[:, j * LANES:(j + 1) * LANES] + offset
        offset = csum[chunk - 1:chunk, :]
        hi = csum.astype(BF16).astype(F32)
        mid = (csum - hi).astype(BF16).astype(F32)
        lo = csum - hi - mid
        feats = jnp.where(
            lane < n_heads, hi,
            jnp.where(lane < 2 * n_heads, pltpu.roll(mid, n_heads, 1),
                      jnp.where(lane < 3 * n_heads, pltpu.roll(lo, 2 * n_heads, 1), 0.0)))
        e_ref[j * chunk:(j + 1) * chunk, :] = feats.astype(BF16)


def _fox_gate(f, b_pad, *, batch, seq, n_heads):
    assert FOX_TERMS * n_heads <= LANES
    return pl.pallas_call(
        functools.partial(_fox_gate_kernel, chunk=MOBA_BLOCK, n_heads=n_heads),
        grid=(batch,),
        in_specs=[
            pl.BlockSpec((seq, LANES), lambda b: (b, 0)),
            _resident((1, LANES), lambda b: (0, 0)),
        ],
        out_specs=pl.BlockSpec((seq, LANES), lambda b: (b, 0)),
        out_shape=jax.ShapeDtypeStruct(f.shape, BF16),
        compiler_params=_compiler_params(1),
        name="fox_gate",
    )(f, b_pad)


ATTN_GROUP = 4
VT_ROWS = HEAD_DIM + BF16_SUBLANES
KEY_PAIR = 2 * MOBA_BLOCK


def _attn_kernel(*refs, moba, n_blocks, group, n_heads):
    if moba:
        (q_ref, k_ref, v_ref, o_ref,
         vt_scr, kaug_scr, qaug_scr, s_scr, smax_scr, m_scr, acc_scr, kbar_scr) = refs
    else:
        (q_ref, k_ref, v_ref, e_ref, o_ref,
         vt_scr, kaug_scr, qaug_scr, s_scr, smax_scr, m_scr, acc_scr) = refs
    blk = MOBA_BLOCK
    pair = KEY_PAIR
    hg = pl.program_id(1)
    own = pl.program_id(2)
    head_cols = [slice(g * HEAD_DIM, (g + 1) * HEAD_DIM) for g in range(group)]
    extra = slice(HEAD_DIM, 2 * HEAD_DIM)

    @pl.when(own == 0)
    def _per_group_setup():
        ones_row = (lax.broadcasted_iota(jnp.int32, (BF16_SUBLANES, pair), 0) == 0).astype(BF16)
        lane = lax.broadcasted_iota(jnp.int32, (pair, LANES), 1)
        row = lax.broadcasted_iota(jnp.int32, (pair, LANES), 0)

        def body(a, carry):
            off = pl.multiple_of(a * pair, pair)
            if moba:
                block_of_row = 2 * a + (row >= blk).astype(jnp.int32)
                feats = jnp.where(lane == block_of_row, 1.0, 0.0).astype(BF16)
            else:
                feats = e_ref[pl.ds(off, pair), :]
            for g, cols in enumerate(head_cols):
                vt_scr[g, a, :HEAD_DIM, :] = (
                    v_ref[pl.ds(off, pair), cols].astype(F32).T.astype(BF16))
                vt_scr[g, a, HEAD_DIM:, :] = ones_row
                kb = k_ref[pl.ds(off, pair), cols]
                kaug_scr[g, pl.ds(off, pair), :HEAD_DIM] = kb
                kaug_scr[g, pl.ds(off, pair), extra] = feats
                if moba:
                    kf = kb.astype(F32)
                    kbar_scr[g, pl.ds(2 * a, 1), :] = jnp.mean(kf[:blk], axis=0, keepdims=True)
                    kbar_scr[g, pl.ds(2 * a + 1, 1), :] = jnp.mean(kf[blk:], axis=0,
                                                                   keepdims=True)
            return carry
        lax.fori_loop(0, n_blocks // 2, body, 0)

    for g, cols in enumerate(head_cols):
        q = q_ref[:, cols]
        qaug_scr[g, :, :HEAD_DIM] = q
        if moba:
            gs = lax.dot_general(kbar_scr[g], q.astype(F32), NT_DIMS,
                                 precision=lax.Precision.HIGHEST,
                                 preferred_element_type=F32)
            blk_id = lax.broadcasted_iota(jnp.int32, gs.shape, 0)
            q_blk = 2 * own + (lax.broadcasted_iota(jnp.int32, gs.shape, 1) >= blk).astype(
                jnp.int32)
            gs = jnp.where(blk_id < q_blk, gs, NEG_INF)
            bias = jnp.where(blk_id == q_blk, 0.0, MASKED)
            for _ in range(MOBA_TOPK):
                mx = jnp.max(gs, axis=0, keepdims=True)
                first = jnp.min(jnp.where(gs == mx, blk_id, n_blocks), axis=0, keepdims=True)
                pick = jnp.logical_and(blk_id == first, mx > NEG_INF)
                bias = jnp.where(pick, 0.0, bias)
                gs = jnp.where(pick, NEG_INF, gs)
            bias = jnp.concatenate([bias, jnp.zeros((LANES - n_blocks, pair), F32)], axis=0)
            qaug_scr[g, :, extra] = bias.T.astype(BF16)
        else:
            lane = lax.broadcasted_iota(jnp.int32, (pair, LANES), 1)
            head = hg * group + g
            mine = functools.reduce(
                jnp.logical_or, [lane == head + j * n_heads for j in range(FOX_TERMS)])
            qaug_scr[g, :, extra] = jnp.where(mine, -1.0, 0.0).astype(BF16)

    def scores(g, a):
        off = pl.multiple_of(a * pair, pair)
        return lax.dot_general(kaug_scr[g, pl.ds(off, pair), :], qaug_scr[g], NT_DIMS,
                               preferred_element_type=F32)

    def stage_scores(slot, g, s):
        s_scr[slot, g] = s
        smax_scr[slot, g] = jnp.max(s, axis=0, keepdims=True)

    def fold_head(slot, g, a, first=False):
        m_new = smax_scr[slot, g]
        if not first:
            m_old = m_scr[g]
            m_new = jnp.maximum(m_old, m_new)
        p = jnp.exp2(s_scr[slot, g] - m_new).astype(BF16)
        pv = jnp.dot(vt_scr[g, a], p, preferred_element_type=F32)
        if not first:
            pv = jnp.exp2(m_old - m_new) * acc_scr[g] + pv
        acc_scr[g] = pv
        m_scr[g] = m_new

    visible = (lax.broadcasted_iota(jnp.int32, (pair, pair), 0)
               <= lax.broadcasted_iota(jnp.int32, (pair, pair), 1))
    for g in range(group):
        stage_scores(1, g, jnp.where(visible, scores(g, own), NEG_INF))
    for g in range(group):
        stage_scores(0, g, scores(g, 0))
        fold_head(1, g, own, first=True)

    def pipelined(slot_next, a_next, slot_cur, a_cur):
        for g in range(group):
            stage_scores(slot_next, g, scores(g, a_next))
            fold_head(slot_cur, g, a_cur)

    def two_pairs(t, carry):
        a0 = 2 * t
        pipelined(1, a0 + 1, 0, a0)
        pipelined(0, jnp.minimum(a0 + 2, own - 1), 1, a0 + 1)
        return carry

    lax.fori_loop(0, lax.shift_right_logical(own, 1), two_pairs, 0)

    @pl.when(jnp.bitwise_and(own, 1) == 1)
    def _last_odd_pair():
        for g in range(group):
            fold_head(0, g, own - 1)

    for g, cols in enumerate(head_cols):
        acc = acc_scr[g]
        o = acc[:HEAD_DIM] * (1.0 / acc[HEAD_DIM:HEAD_DIM + 1])
        o_ref[:, cols] = o.T.astype(o_ref.dtype)


def _attention(proj, e, *, moba, batch, seq, heads, q_col, k_col, v_col):
    group = ATTN_GROUP
    tq = KEY_PAIR
    nq = seq // tq
    n_blocks = seq // MOBA_BLOCK
    width = group * HEAD_DIM
    assert heads % group == 0 and seq % KEY_PAIR == 0 and n_blocks <= LANES
    assert q_col % group == 0 and k_col % group == 0 and v_col % group == 0
    in_specs = [
        pl.BlockSpec((tq, width), lambda b, h, i: (b * nq + i, q_col // group + h)),
        pl.BlockSpec((seq, width), lambda b, h, i: (b, k_col // group + h)),
        pl.BlockSpec((seq, width), lambda b, h, i: (b, v_col // group + h)),
    ]
    args = [proj, proj, proj]
    if not moba:
        in_specs.append(pl.BlockSpec((seq, LANES), lambda b, h, i: (b, 0)))
        args.append(e)
    scratch = [
        pltpu.VMEM((group, seq // KEY_PAIR, VT_ROWS, KEY_PAIR), BF16),
        pltpu.VMEM((group, seq, 2 * HEAD_DIM), BF16),
        pltpu.VMEM((group, tq, 2 * HEAD_DIM), BF16),
        pltpu.VMEM((2, group, KEY_PAIR, tq), F32),
        pltpu.VMEM((2, group, 1, tq), F32),
        pltpu.VMEM((group, 1, tq), F32),
        pltpu.VMEM((group, VT_ROWS, tq), F32),
    ]
    if moba:
        scratch.append(pltpu.VMEM((group, n_blocks, HEAD_DIM), F32))
    return pl.pallas_call(
        functools.partial(_attn_kernel, moba=moba, n_blocks=n_blocks, group=group,
                          n_heads=heads),
        grid=(batch, heads // group, nq),
        in_specs=in_specs,
        out_specs=pl.BlockSpec((tq, width), lambda b, h, i: (b * nq + i, h)),
        out_shape=jax.ShapeDtypeStruct((batch * seq, heads * HEAD_DIM), BF16),
        scratch_shapes=scratch,
        compiler_params=_compiler_params(3),
        name="moba_attention" if moba else "fox_attention",
    )(*args)


def _rmsnorm(x, g):
    ms = jnp.mean(x * x, axis=-1, keepdims=True)
    return x * lax.rsqrt(ms + RMS_EPS) * g


def _out_merge_kernel(oa_ref, ob_ref, za_ref, zb_ref, ga_ref, gb_ref, x_ref, p_ref,
                      wa_ref, wb_ref, wo_ref, wg_ref, wu_ref, gple_ref, gfin_ref, out_ref,
                      *, final):
    def branch(o_ref, z_ref, w_ref):
        z = z_ref[...].astype(F32)
        a = o_ref[...].astype(F32) * (z * jax.nn.sigmoid(z))
        return jnp.dot(a.astype(BF16), w_ref[...], preferred_element_type=F32)

    ya = branch(oa_ref, za_ref, wa_ref)
    yb = branch(ob_ref, zb_ref, wb_ref)
    mixed = (jax.nn.sigmoid(ga_ref[...].astype(F32)) * ya
             + jax.nn.sigmoid(gb_ref[...].astype(F32)) * yb)
    x1 = x_ref[...] + jnp.dot(mixed.astype(BF16), wo_ref[...], preferred_element_type=F32)
    hn = _rmsnorm(x1, gple_ref[...]).astype(BF16)
    pg = jax.nn.sigmoid(jnp.dot(hn, wg_ref[...], preferred_element_type=F32))
    up = jnp.dot(p_ref[...].astype(BF16), wu_ref[...], preferred_element_type=F32)
    x2 = x1 + up * pg
    if final:
        x2 = _rmsnorm(x2, gfin_ref[...])
    out_ref[...] = x2


def _out_merge(oa, ob, proj, x2, p2, wa, wb, wo, wg, wu, g_ple, g_final, *, tm, final,
               za_col, zb_col, ga_col, gb_col):
    t, d = x2.shape
    wa_w = oa.shape[1]
    wb_w = ob.shape[1]
    ple = p2.shape[1]
    row = lambda i: (i, 0)
    const = lambda i: (0, 0)
    return pl.pallas_call(
        functools.partial(_out_merge_kernel, final=final),
        grid=(t // tm,),
        in_specs=[
            pl.BlockSpec((tm, wa_w), row),
            pl.BlockSpec((tm, wb_w), row),
            pl.BlockSpec((tm, wa_w), lambda i: (i, za_col)),
            pl.BlockSpec((tm, wb_w), lambda i: (i, zb_col)),
            pl.BlockSpec((tm, d), lambda i: (i, ga_col)),
            pl.BlockSpec((tm, d), lambda i: (i, gb_col)),
            pl.BlockSpec((tm, d), row),
            pl.BlockSpec((tm, ple), row),
            _resident((wa_w, d), const),
            _resident((wb_w, d), const),
            _resident((d, d), const),
            _resident((d, d), const),
            _resident((ple, d), const),
            _resident((1, d), const),
            _resident((1, d), const),
        ],
        out_specs=pl.BlockSpec((tm, d), row),
        out_shape=jax.ShapeDtypeStruct((t, d), F32),
        compiler_params=_compiler_params(1),
        name="out_merge",
    )(oa, ob, proj, proj, proj, proj, x2, p2, wa, wb, wo, wg, wu, g_ple, g_final)


def kernel(x, p, positions, g_norm, w_in, b_f, w_branch_a, w_branch_b, w_out,
           g_ple, w_ple_gate, w_ple_up, g_final):
    batch, seq, d = x.shape
    depth = w_in.shape[0]
    t = batch * seq
    wa_w = w_branch_a.shape[1]
    wb_w = w_branch_b.shape[1]
    heads_a = wa_w // HEAD_DIM
    heads_b = wb_w // HEAD_DIM
    n_f = b_f.shape[1]
    assert seq % MOBA_BLOCK == 0 and wa_w == wb_w and 2 * wa_w == d
    f_start = 4 * wa_w + 3 * wb_w
    assert w_in.shape[2] == f_start + n_f + wb_w + 2 * d

    inv = ROPE_THETA ** (-jnp.arange(0, HEAD_DIM, 2, dtype=F32) / HEAD_DIM)
    inv_full = jnp.concatenate([inv, inv]).reshape(1, HEAD_DIM)
    pos = positions.reshape(t, 1)
    x2 = x.reshape(t, d)
    hb = HEAD_DIM

    for layer in range(depth):
        w_t = jnp.swapaxes(w_in[layer], 0, 1)
        w_main, w_f = _w_in_prep(w_t, f_start=f_start, n_f=n_f, tn=wa_w)
        b_pad = jnp.pad(b_f[layer], (0, LANES - n_f)).reshape(1, LANES)

        proj, f = _in_proj(x2, pos, g_norm[layer].reshape(1, d), inv_full, w_main, w_f,
                           tm=1024, tn=2 * wa_w, section=wa_w)
        e = _fox_gate(f, b_pad, batch=batch, seq=seq, n_heads=heads_b)
        oa = _attention(proj, None, moba=True, batch=batch, seq=seq, heads=heads_a,
                        q_col=0, k_col=wa_w // hb, v_col=2 * wa_w // hb)
        ob = _attention(proj, e, moba=False, batch=batch, seq=seq, heads=heads_b,
                        q_col=4 * wa_w // hb, k_col=(4 * wa_w + wb_w) // hb,
                        v_col=(4 * wa_w + 2 * wb_w) // hb)
        x2 = _out_merge(
            oa, ob, proj, x2, p[layer].reshape(t, -1),
            w_branch_a[layer].astype(BF16), w_branch_b[layer].astype(BF16),
            w_out[layer].astype(BF16), w_ple_gate[layer].astype(BF16),
            w_ple_up[layer].astype(BF16), g_ple[layer].reshape(1, d), g_final.reshape(1, d),
            tm=256, final=(layer == depth - 1),
            za_col=3, zb_col=(4 * wa_w + 3 * wb_w) // wb_w,
            ga_col=(4 * wa_w + 4 * wb_w) // d, gb_col=(4 * wa_w + 4 * wb_w) // d + 1)
    return x2.reshape(batch, seq, d)
```

```python
import functools
import math

import jax
import jax.numpy as jnp
from jax import lax
from jax.experimental import pallas as pl
from jax.experimental.pallas import tpu as pltpu

HEAD_DIM = 128
MOBA_BLOCK = 256
MOBA_TOPK = 3
ROPE_THETA = 10000.0
RMS_EPS = 1e-6

V7X_VMEM_BYTES = 64 * 1024 * 1024
VMEM_LIMIT_BYTES = V7X_VMEM_BYTES * 7 // 8
LANES = 128
BF16_SUBLANES = 16

F32 = jnp.float32
BF16 = jnp.bfloat16
NEG_INF = float("-inf")
MASKED = -1e30
LOG2E = math.log2(math.e)
NT_DIMS = (((1,), (1,)), ((), ()))


def _compiler_params(n_grid_dims):
    return pltpu.CompilerParams(
        dimension_semantics=("arbitrary",) * n_grid_dims,
        vmem_limit_bytes=VMEM_LIMIT_BYTES,
    )


def _resident(block_shape, index_map):
    return pl.BlockSpec(block_shape, index_map, pipeline_mode=pl.Buffered(1))


F32_SUBLANES = 8
PREP_COLS = 512


def _w_in_prep_kernel(wt_hbm, out_ref, wf_ref, buf, fbuf, sem, fsem, *, n_lo, n_f):
    n = pl.program_id(0)
    tn = out_ref.shape[1]

    def fetch(j, slot):
        first_row = pl.multiple_of(j * tn + jnp.where(j >= n_lo, n_f, 0), F32_SUBLANES)
        return pltpu.make_async_copy(wt_hbm.at[pl.ds(first_row, tn), :], buf.at[slot],
                                     sem.at[slot])

    slot = jnp.bitwise_and(n, 1)

    @pl.when(n == 0)
    def _first():
        fetch(0, 0).start()
        gate_rows = pltpu.make_async_copy(wt_hbm.at[pl.ds(n_lo * tn, n_f), :], fbuf, fsem)
        gate_rows.start()
        gate_rows.wait()
        padded = jnp.concatenate(
            [fbuf[...], jnp.zeros((LANES - n_f, fbuf.shape[1]), F32)], axis=0)
        wf_ref[...] = padded.T.astype(BF16)

    @pl.when(n + 1 < pl.num_programs(0))
    def _ahead():
        fetch(n + 1, 1 - slot).start()

    fetch(n, slot).wait()
    for c0 in range(0, buf.shape[2], PREP_COLS):
        out_ref[c0:c0 + PREP_COLS, :] = buf[slot, :, c0:c0 + PREP_COLS].T.astype(BF16)


def _w_in_prep(w_t, *, f_start, n_f, tn):
    n_in, d = w_t.shape
    n_lo = f_start // tn
    n_tiles = (n_in - n_f) // tn
    assert f_start == n_lo * tn and n_in == n_tiles * tn + n_f
    assert n_f % F32_SUBLANES == 0 and n_f <= LANES and d % PREP_COLS == 0
    return pl.pallas_call(
        functools.partial(_w_in_prep_kernel, n_lo=n_lo, n_f=n_f),
        grid=(n_tiles,),
        in_specs=[pl.BlockSpec(memory_space=pl.ANY)],
        out_specs=[pl.BlockSpec((d, tn), lambda n: (0, n)),
                   pl.BlockSpec((d, LANES), lambda n: (0, 0))],
        out_shape=[jax.ShapeDtypeStruct((d, n_tiles * tn), BF16),
                   jax.ShapeDtypeStruct((d, LANES), BF16)],
        scratch_shapes=[pltpu.VMEM((2, tn, d), F32), pltpu.VMEM((n_f, d), F32),
                        pltpu.SemaphoreType.DMA((2,)), pltpu.SemaphoreType.DMA(())],
        compiler_params=_compiler_params(1),
        name="w_in_prep",
    )(w_t)


MXU_COLS = 256


def _in_proj_kernel(x_ref, pos_ref, g_ref, inv_ref, w_ref, wf_ref, proj_ref, f_ref,
                    h_scr, cos_scr, sin_scr, *, section, rope_sections, scaled_sections,
                    q_scale, row_tile):
    n = pl.program_id(1)
    half = HEAD_DIM // 2

    @pl.when(n == 0)
    def _normalize_and_tables():
        x = x_ref[...]
        ms = jnp.mean(x * x, axis=-1, keepdims=True)
        h = (x * lax.rsqrt(ms + RMS_EPS) * g_ref[...]).astype(BF16)
        h_scr[...] = h
        f_ref[...] = jnp.dot(h, wf_ref[...], preferred_element_type=F32)
        lane = lax.broadcasted_iota(jnp.int32, (x.shape[0], HEAD_DIM), 1)
        first = lane < half
        ang = pos_ref[...].astype(F32) * inv_ref[...]
        cs = jnp.cos(jnp.where(first, ang, ang - 0.5 * math.pi))
        sc = pltpu.roll(cs, half, 1)
        cos_scr[...] = jnp.where(first, cs, sc)
        sin_scr[...] = jnp.where(first, -sc, cs)

    tn = proj_ref.shape[1]
    coef_c, coef_s = [], []
    for k in range(tn // section):
        sec = n * (tn // section) + k
        is_rope = functools.reduce(jnp.logical_or, [sec == s for s in rope_sections])
        scaled = functools.reduce(jnp.logical_or, [sec == s for s in scaled_sections])
        scale = jnp.where(scaled, q_scale, 1.0).astype(F32)
        coef_c.append(jnp.where(is_rope, cos_scr[...], 1.0) * scale)
        coef_s.append(jnp.where(is_rope, sin_scr[...], 0.0) * scale)
    tm = h_scr.shape[0]
    for r0 in range(0, tm, row_tile):
        rows = slice(r0, r0 + row_tile)
        h = h_scr[rows, :]
        for c0 in range(0, tn, MXU_COLS):
            acc = jnp.dot(h, w_ref[:, c0:c0 + MXU_COLS],
                          preferred_element_type=F32)
            cc, cs = coef_c[c0 // section][rows], coef_s[c0 // section][rows]
            for j in range(0, MXU_COLS, HEAD_DIM):
                a = acc[:, j:j + HEAD_DIM]
                r = a * cc + pltpu.roll(a, half, 1) * cs
                proj_ref[rows, c0 + j:c0 + j + HEAD_DIM] = r.astype(BF16)


def _in_proj(x2, pos, g, inv_full, w_main, w_f, *, tm, tn, section):
    t, d = x2.shape
    n_main = w_main.shape[1]
    assert tn % section == 0 and n_main % tn == 0 and section % MXU_COLS == 0
    kern = functools.partial(
        _in_proj_kernel, section=section, rope_sections=(0, 1), scaled_sections=(0, 4),
        q_scale=LOG2E / math.sqrt(HEAD_DIM), row_tile=512)
    return pl.pallas_call(
        kern,
        grid=(t // tm, n_main // tn),
        in_specs=[
            pl.BlockSpec((tm, d), lambda i, n: (i, 0)),
            pl.BlockSpec((tm, 1), lambda i, n: (i, 0)),
            _resident((1, d), lambda i, n: (0, 0)),
            _resident((1, LANES), lambda i, n: (0, 0)),
            pl.BlockSpec((d, tn), lambda i, n: (0, n)),
            _resident((d, LANES), lambda i, n: (0, 0)),
        ],
        out_specs=[
            pl.BlockSpec((tm, tn), lambda i, n: (i, n)),
            pl.BlockSpec((tm, LANES), lambda i, n: (i, 0)),
        ],
        out_shape=[
            jax.ShapeDtypeStruct((t, n_main), BF16),
            jax.ShapeDtypeStruct((t, LANES), F32),
        ],
        scratch_shapes=[
            pltpu.VMEM((tm, d), BF16),
            pltpu.VMEM((tm, LANES), F32),
            pltpu.VMEM((tm, LANES), F32),
        ],
        compiler_params=_compiler_params(2),
        name="in_proj",
    )(x2, pos, g, inv_full, w_main, w_f)


FOX_TERMS = 3


def _fox_gate_kernel(f_ref, b_ref, e_ref, *, chunk, n_heads):
    n_chunks = f_ref.shape[0] // chunk
    r = lax.broadcasted_iota(jnp.int32, (chunk, chunk), 0)
    c = lax.broadcasted_iota(jnp.int32, (chunk, chunk), 1)
    tri = (c <= r).astype(F32)
    z = f_ref[...] + b_ref[...]
    logf = (jnp.minimum(z, 0.0) - jnp.log1p(jnp.exp(-jnp.abs(z)))) * LOG2E
    wide = jnp.concatenate([logf[j * chunk:(j + 1) * chunk] for j in range(n_chunks)], axis=1)
    local = jnp.dot(tri, wide, precision=lax.Precision.HIGHEST, preferred_element_type=F32)
    lane = lax.broadcasted_iota(jnp.int32, (chunk, LANES), 1)
    offset = jnp.zeros((1, LANES), F32)
    for j in range(n_chunks):
        csum = local[:, j * LANES:(j + 1) * LANES] + offset
        offset = csum[chunk - 1:chunk, :]
        hi = csum.astype(BF16).astype(F32)
        mid = (csum - hi).astype(BF16).astype(F32)
        lo = csum - hi - mid
        feats = jnp.where(
            lane < n_heads, hi,
            jnp.where(lane < 2 * n_heads, pltpu.roll(mid, n_heads, 1),
                      jnp.where(lane < 3 * n_heads, pltpu.roll(lo, 2 * n_heads, 1), 0.0)))
        e_ref[j * chunk:(j + 1) * chunk, :] = feats.astype(BF16)


def _fox_gate(f, b_pad, *, batch, seq, n_heads):
    assert FOX_TERMS * n_heads <= LANES
    return pl.pallas_call(
        functools.partial(_fox_gate_kernel, chunk=MOBA_BLOCK, n_heads=n_heads),
        grid=(batch,),
        in_specs=[
            pl.BlockSpec((seq, LANES), lambda b: (b, 0)),
            _resident((1, LANES), lambda b: (0, 0)),
        ],
        out_specs=pl.BlockSpec((seq, LANES), lambda b: (b, 0)),
        out_shape=jax.ShapeDtypeStruct(f.shape, BF16),
        compiler_params=_compiler_params(1),
        name="fox_gate",
    )(f, b_pad)


ATTN_GROUP = 4
VT_ROWS = HEAD_DIM + BF16_SUBLANES
KEY_PAIR = 2 * MOBA_BLOCK


def _attn_kernel(*refs, moba, n_blocks, group, n_heads):
    if moba:
        (q_ref, k_ref, v_ref, o_ref,
         vt_scr, kaug_scr, qaug_scr, s_scr, smax_scr, m_scr, acc_scr, kbar_scr) = refs
    else:
        (q_ref, k_ref, v_ref, e_ref, o_ref,
         vt_scr, kaug_scr, qaug_scr, s_scr, smax_scr, m_scr, acc_scr) = refs
    blk = MOBA_BLOCK
    pair = KEY_PAIR
    hg = pl.program_id(1)
    own = pl.program_id(2)
    head_cols = [slice(g * HEAD_DIM, (g + 1) * HEAD_DIM) for g in range(group)]
    extra = slice(HEAD_DIM, 2 * HEAD_DIM)

    @pl.when(own == 0)
    def _per_group_setup():
        ones_row = (lax.broadcasted_iota(jnp.int32, (BF16_SUBLANES, pair), 0) == 0).astype(BF16)
        lane = lax.broadcasted_iota(jnp.int32, (pair, LANES), 1)
        row = lax.broadcasted_iota(jnp.int32, (pair, LANES), 0)

        def body(a, carry):
            off = pl.multiple_of(a * pair, pair)
            if moba:
                block_of_row = 2 * a + (row >= blk).astype(jnp.int32)
                feats = jnp.where(lane == block_of_row, 1.0, 0.0).astype(BF16)
            else:
                feats = e_ref[pl.ds(off, pair), :]
            for g, cols in enumerate(head_cols):
                vt_scr[g, a, :HEAD_DIM, :] = (
                    v_ref[pl.ds(off, pair), cols].astype(F32).T.astype(BF16))
                vt_scr[g, a, HEAD_DIM:, :] = ones_row
                kb = k_ref[pl.ds(off, pair), cols]
                kaug_scr[g, pl.ds(off, pair), :HEAD_DIM] = kb
                kaug_scr[g, pl.ds(off, pair), extra] = feats
                if moba:
                    kf = kb.astype(F32)
                    kbar_scr[g, pl.ds(2 * a, 1), :] = jnp.mean(kf[:blk], axis=0, keepdims=True)
                    kbar_scr[g, pl.ds(2 * a + 1, 1), :] = jnp.mean(kf[blk:], axis=0,
                                                                   keepdims=True)
            return carry
        lax.fori_loop(0, n_blocks // 2, body, 0)

    def widen_queries(g):
        q = q_ref[:, head_cols[g]]
        qaug_scr[g, :, :HEAD_DIM] = q
        if moba:
            kbar = kbar_scr[g]
            hi = kbar.astype(BF16)
            mid = (kbar - hi.astype(F32)).astype(BF16)
            lo = (kbar - hi.astype(F32) - mid.astype(F32)).astype(BF16)
            gs3 = lax.dot_general(jnp.concatenate([hi, mid, lo], axis=0), q, NT_DIMS,
                                  preferred_element_type=F32)
            gs = gs3[:n_blocks] + gs3[n_blocks:2 * n_blocks] + gs3[2 * n_blocks:]
            blk_id = lax.broadcasted_iota(jnp.int32, gs.shape, 0)
            q_blk = 2 * own + (lax.broadcasted_iota(jnp.int32, gs.shape, 1) >= blk).astype(
                jnp.int32)
            gs = jnp.where(blk_id < q_blk, gs, NEG_INF)
            bias = jnp.where(blk_id == q_blk, 0.0, MASKED)
            for _ in range(MOBA_TOPK):
                mx = jnp.max(gs, axis=0, keepdims=True)
                first = jnp.min(jnp.where(gs == mx, blk_id, n_blocks), axis=0, keepdims=True)
                pick = jnp.logical_and(blk_id == first, mx > NEG_INF)
                bias = jnp.where(pick, 0.0, bias)
                gs = jnp.where(pick, NEG_INF, gs)
            bias = jnp.concatenate([bias, jnp.zeros((LANES - n_blocks, pair), F32)], axis=0)
            qaug_scr[g, :, extra] = bias.T.astype(BF16)
        else:
            lane = lax.broadcasted_iota(jnp.int32, (pair, LANES), 1)
            head = hg * group + g
            mine = functools.reduce(
                jnp.logical_or, [lane == head + j * n_heads for j in range(FOX_TERMS)])
            qaug_scr[g, :, extra] = jnp.where(mine, -1.0, 0.0).astype(BF16)

    def scores(g, a):
        off = pl.multiple_of(a * pair, pair)
        return lax.dot_general(kaug_scr[g, pl.ds(off, pair), :], qaug_scr[g], NT_DIMS,
                               preferred_element_type=F32)

    def stage_scores(slot, g, s):
        s_scr[slot, g] = s
        smax_scr[slot, g] = jnp.max(s, axis=0, keepdims=True)

    def fold_head(slot, g, a, first=False):
        m_new = smax_scr[slot, g]
        if not first:
            m_old = m_scr[g]
            m_new = jnp.maximum(m_old, m_new)
        p = jnp.exp2(s_scr[slot, g] - m_new).astype(BF16)
        pv = jnp.dot(vt_scr[g, a], p, preferred_element_type=F32)
        if not first:
            pv = jnp.exp2(m_old - m_new) * acc_scr[g] + pv
        acc_scr[g] = pv
        m_scr[g] = m_new

    visible = (lax.broadcasted_iota(jnp.int32, (pair, pair), 0)
               <= lax.broadcasted_iota(jnp.int32, (pair, pair), 1))
    for g in range(group):
        widen_queries(g)
    for g in range(group):
        stage_scores(1, g, jnp.where(visible, scores(g, own), NEG_INF))
    for g in range(group):
        stage_scores(0, g, scores(g, 0))
        fold_head(1, g, own, first=True)

    def pipelined(slot_next, a_next, slot_cur, a_cur):
        for g in range(group):
            stage_scores(slot_next, g, scores(g, a_next))
            fold_head(slot_cur, g, a_cur)

    def two_pairs(t, carry):
        a0 = 2 * t
        pipelined(1, a0 + 1, 0, a0)
        pipelined(0, jnp.minimum(a0 + 2, own - 1), 1, a0 + 1)
        return carry

    lax.fori_loop(0, lax.shift_right_logical(own, 1), two_pairs, 0)

    @pl.when(jnp.bitwise_and(own, 1) == 1)
    def _last_odd_pair():
        for g in range(group):
            fold_head(0, g, own - 1)

    for g, cols in enumerate(head_cols):
        acc = acc_scr[g]
        o = acc[:HEAD_DIM] * (1.0 / acc[HEAD_DIM:HEAD_DIM + 1])
        o_ref[:, cols] = o.T.astype(o_ref.dtype)


def _attention(proj, e, *, moba, batch, seq, heads, q_col, k_col, v_col):
    group = ATTN_GROUP
    tq = KEY_PAIR
    nq = seq // tq
    n_blocks = seq // MOBA_BLOCK
    width = group * HEAD_DIM
    assert heads % group == 0 and seq % KEY_PAIR == 0 and n_blocks <= LANES
    assert q_col % group == 0 and k_col % group == 0 and v_col % group == 0
    in_specs = [
        pl.BlockSpec((tq, width), lambda b, h, i: (b * nq + i, q_col // group + h)),
        pl.BlockSpec((seq, width), lambda b, h, i: (b, k_col // group + h)),
        pl.BlockSpec((seq, width), lambda b, h, i: (b, v_col // group + h)),
    ]
    args = [proj, proj, proj]
    if not moba:
        in_specs.append(pl.BlockSpec((seq, LANES), lambda b, h, i: (b, 0)))
        args.append(e)
    scratch = [
        pltpu.VMEM((group, seq // KEY_PAIR, VT_ROWS, KEY_PAIR), BF16),
        pltpu.VMEM((group, seq, 2 * HEAD_DIM), BF16),
        pltpu.VMEM((group, tq, 2 * HEAD_DIM), BF16),
        pltpu.VMEM((2, group, KEY_PAIR, tq), F32),
        pltpu.VMEM((2, group, 1, tq), F32),
        pltpu.VMEM((group, 1, tq), F32),
        pltpu.VMEM((group, VT_ROWS, tq), F32),
    ]
    if moba:
        scratch.append(pltpu.VMEM((group, n_blocks, HEAD_DIM), F32))
    return pl.pallas_call(
        functools.partial(_attn_kernel, moba=moba, n_blocks=n_blocks, group=group,
                          n_heads=heads),
        grid=(batch, heads // group, nq),
        in_specs=in_specs,
        out_specs=pl.BlockSpec((tq, width), lambda b, h, i: (b * nq + i, h)),
        out_shape=jax.ShapeDtypeStruct((batch * seq, heads * HEAD_DIM), BF16),
        scratch_shapes=scratch,
        compiler_params=_compiler_params(3),
        name="moba_attention" if moba else "fox_attention",
    )(*args)


def _rmsnorm(x, g):
    ms = jnp.mean(x * x, axis=-1, keepdims=True)
    return x * lax.rsqrt(ms + RMS_EPS) * g


def _out_merge_kernel(oa_ref, ob_ref, za_ref, zb_ref, ga_ref, gb_ref, x_ref, p_ref,
                      wa_ref, wb_ref, wo_ref, wg_ref, wu_ref, gple_ref, gfin_ref, out_ref,
                      *, final):
    def branch(o_ref, z_ref, w_ref):
        z = z_ref[...].astype(F32)
        a = o_ref[...].astype(F32) * (z * jax.nn.sigmoid(z))
        return jnp.dot(a.astype(BF16), w_ref[...], preferred_element_type=F32)

    p_in = p_ref[...].astype(BF16)
    d_half = out_ref.shape[1] // 2
    ya = branch(oa_ref, za_ref, wa_ref)
    yb = branch(ob_ref, zb_ref, wb_ref)
    up_lo = jnp.dot(p_in, wu_ref[:, :d_half], preferred_element_type=F32)
    mixed = (jax.nn.sigmoid(ga_ref[...].astype(F32)) * ya
             + jax.nn.sigmoid(gb_ref[...].astype(F32)) * yb)
    x1 = x_ref[...] + jnp.dot(mixed.astype(BF16), wo_ref[...], preferred_element_type=F32)
    up_hi = jnp.dot(p_in, wu_ref[:, d_half:], preferred_element_type=F32)
    hn = _rmsnorm(x1, gple_ref[...]).astype(BF16)
    pg = jax.nn.sigmoid(jnp.dot(hn, wg_ref[...], preferred_element_type=F32))
    x2 = x1 + jnp.concatenate([up_lo, up_hi], axis=1) * pg
    if final:
        x2 = _rmsnorm(x2, gfin_ref[...])
    out_ref[...] = x2


def _out_merge(oa, ob, proj, x2, p2, wa, wb, wo, wg, wu, g_ple, g_final, *, tm, final,
               za_col, zb_col, ga_col, gb_col):
    t, d = x2.shape
    wa_w = oa.shape[1]
    wb_w = ob.shape[1]
    ple = p2.shape[1]
    row = lambda i: (i, 0)
    const = lambda i: (0, 0)
    return pl.pallas_call(
        functools.partial(_out_merge_kernel, final=final),
        grid=(t // tm,),
        in_specs=[
            pl.BlockSpec((tm, wa_w), row),
            pl.BlockSpec((tm, wb_w), row),
            pl.BlockSpec((tm, wa_w), lambda i: (i, za_col)),
            pl.BlockSpec((tm, wb_w), lambda i: (i, zb_col)),
            pl.BlockSpec((tm, d), lambda i: (i, ga_col)),
            pl.BlockSpec((tm, d), lambda i: (i, gb_col)),
            pl.BlockSpec((tm, d), row),
            pl.BlockSpec((tm, ple), row),
            _resident((wa_w, d), const),
            _resident((wb_w, d), const),
            _resident((d, d), const),
            _resident((d, d), const),
            _resident((ple, d), const),
            _resident((1, d), const),
            _resident((1, d), const),
        ],
        out_specs=pl.BlockSpec((tm, d), row),
        out_shape=jax.ShapeDtypeStruct((t, d), F32),
        compiler_params=_compiler_params(1),
        name="out_merge",
    )(oa, ob, proj, proj, proj, proj, x2, p2, wa, wb, wo, wg, wu, g_ple, g_final)


def kernel(x, p, positions, g_norm, w_in, b_f, w_branch_a, w_branch_b, w_out,
           g_ple, w_ple_gate, w_ple_up, g_final):
    batch, seq, d = x.shape
    depth = w_in.shape[0]
    t = batch * seq
    wa_w = w_branch_a.shape[1]
    wb_w = w_branch_b.shape[1]
    heads_a = wa_w // HEAD_DIM
    heads_b = wb_w // HEAD_DIM
    n_f = b_f.shape[1]
    assert seq % MOBA_BLOCK == 0 and wa_w == wb_w and 2 * wa_w == d
    f_start = 4 * wa_w + 3 * wb_w
    assert w_in.shape[2] == f_start + n_f + wb_w + 2 * d

    inv = ROPE_THETA ** (-jnp.arange(0, HEAD_DIM, 2, dtype=F32) / HEAD_DIM)
    inv_full = jnp.concatenate([inv, inv]).reshape(1, HEAD_DIM)
    pos = positions.reshape(t, 1)
    x2 = x.reshape(t, d)
    hb = HEAD_DIM

    for layer in range(depth):
        w_t = jnp.swapaxes(w_in[layer], 0, 1)
        w_main, w_f = _w_in_prep(w_t, f_start=f_start, n_f=n_f, tn=wa_w)
        b_pad = jnp.pad(b_f[layer], (0, LANES - n_f)).reshape(1, LANES)

        proj, f = _in_proj(x2, pos, g_norm[layer].reshape(1, d), inv_full, w_main, w_f,
                           tm=1024, tn=2 * wa_w, section=wa_w)
        e = _fox_gate(f, b_pad, batch=batch, seq=seq, n_heads=heads_b)
        oa = _attention(proj, None, moba=True, batch=batch, seq=seq, heads=heads_a,
                        q_col=0, k_col=wa_w // hb, v_col=2 * wa_w // hb)
        ob = _attention(proj, e, moba=False, batch=batch, seq=seq, heads=heads_b,
                        q_col=4 * wa_w // hb, k_col=(4 * wa_w + wb_w) // hb,
                        v_col=(4 * wa_w + 2 * wb_w) // hb)
        x2 = _out_merge(
            oa, ob, proj, x2, p[layer].reshape(t, -1),
            w_branch_a[layer].astype(BF16), w_branch_b[layer].astype(BF16),
            w_out[layer].astype(BF16), w_ple_gate[layer].astype(BF16),
            w_ple_up[layer].astype(BF16), g_ple[layer].reshape(1, d), g_final.reshape(1, d),
            tm=256, final=(layer == depth - 1),
            za_col=3, zb_col=(4 * wa_w + 3 * wb_w) // wb_w,
            ga_col=(4 * wa_w + 4 * wb_w) // d, gb_col=(4 * wa_w + 4 * wb_w) // d + 1)
    return x2.reshape(batch, seq, d)
```

```python
import functools
import math

import jax
import jax.numpy as jnp
from jax import lax
from jax.experimental import pallas as pl
from jax.experimental.pallas import tpu as pltpu

HEAD_DIM = 128
MOBA_BLOCK = 256
MOBA_TOPK = 3
ROPE_THETA = 10000.0
RMS_EPS = 1e-6

V7X_VMEM_BYTES = 64 * 1024 * 1024
VMEM_LIMIT_BYTES = V7X_VMEM_BYTES * 7 // 8
LANES = 128
BF16_SUBLANES = 16

F32 = jnp.float32
BF16 = jnp.bfloat16
NEG_INF = float("-inf")
MASKED = -1e30
LOG2E = math.log2(math.e)
NT_DIMS = (((1,), (1,)), ((), ()))


def _compiler_params(n_grid_dims):
    return pltpu.CompilerParams(
        dimension_semantics=("arbitrary",) * n_grid_dims,
        vmem_limit_bytes=VMEM_LIMIT_BYTES,
    )


def _resident(block_shape, index_map):
    return pl.BlockSpec(block_shape, index_map, pipeline_mode=pl.Buffered(1))


F32_SUBLANES = 8
PREP_COLS = 512


def _w_in_prep_kernel(wt_hbm, out_ref, wf_ref, buf, fbuf, sem, fsem, *, n_lo, n_f):
    n = pl.program_id(0)
    tn = out_ref.shape[1]

    def fetch(j, slot):
        first_row = pl.multiple_of(j * tn + jnp.where(j >= n_lo, n_f, 0), F32_SUBLANES)
        return pltpu.make_async_copy(wt_hbm.at[pl.ds(first_row, tn), :], buf.at[slot],
                                     sem.at[slot])

    slot = jnp.bitwise_and(n, 1)

    @pl.when(n == 0)
    def _first():
        fetch(0, 0).start()
        gate_rows = pltpu.make_async_copy(wt_hbm.at[pl.ds(n_lo * tn, n_f), :], fbuf, fsem)
        gate_rows.start()
        gate_rows.wait()
        padded = jnp.concatenate(
            [fbuf[...], jnp.zeros((LANES - n_f, fbuf.shape[1]), F32)], axis=0)
        wf_ref[...] = padded.T.astype(BF16)

    @pl.when(n + 1 < pl.num_programs(0))
    def _ahead():
        fetch(n + 1, 1 - slot).start()

    fetch(n, slot).wait()
    for c0 in range(0, buf.shape[2], PREP_COLS):
        out_ref[c0:c0 + PREP_COLS, :] = buf[slot, :, c0:c0 + PREP_COLS].T.astype(BF16)


def _w_in_prep(w_t, *, f_start, n_f, tn):
    n_in, d = w_t.shape
    n_lo = f_start // tn
    n_tiles = (n_in - n_f) // tn
    assert f_start == n_lo * tn and n_in == n_tiles * tn + n_f
    assert n_f % F32_SUBLANES == 0 and n_f <= LANES and d % PREP_COLS == 0
    return pl.pallas_call(
        functools.partial(_w_in_prep_kernel, n_lo=n_lo, n_f=n_f),
        grid=(n_tiles,),
        in_specs=[pl.BlockSpec(memory_space=pl.ANY)],
        out_specs=[pl.BlockSpec((d, tn), lambda n: (0, n)),
                   pl.BlockSpec((d, LANES), lambda n: (0, 0))],
        out_shape=[jax.ShapeDtypeStruct((d, n_tiles * tn), BF16),
                   jax.ShapeDtypeStruct((d, LANES), BF16)],
        scratch_shapes=[pltpu.VMEM((2, tn, d), F32), pltpu.VMEM((n_f, d), F32),
                        pltpu.SemaphoreType.DMA((2,)), pltpu.SemaphoreType.DMA(())],
        compiler_params=_compiler_params(1),
        name="w_in_prep",
    )(w_t)


MXU_COLS = 256


def _in_proj_kernel(x_ref, pos_ref, g_ref, inv_ref, w_ref, wf_ref, proj_ref, f_ref,
                    h_scr, cos_scr, sin_scr, *, section, rope_sections, scaled_sections,
                    q_scale, row_tile):
    n = pl.program_id(1)
    half = HEAD_DIM // 2

    @pl.when(n == 0)
    def _normalize_and_tables():
        x = x_ref[...]
        ms = jnp.mean(x * x, axis=-1, keepdims=True)
        h = (x * lax.rsqrt(ms + RMS_EPS) * g_ref[...]).astype(BF16)
        h_scr[...] = h
        f_ref[...] = jnp.dot(h, wf_ref[...], preferred_element_type=F32)
        lane = lax.broadcasted_iota(jnp.int32, (x.shape[0], HEAD_DIM), 1)
        first = lane < half
        ang = pos_ref[...].astype(F32) * inv_ref[...]
        cs = jnp.cos(jnp.where(first, ang, ang - 0.5 * math.pi))
        sc = pltpu.roll(cs, half, 1)
        cos_scr[...] = jnp.where(first, cs, sc)
        sin_scr[...] = jnp.where(first, -sc, cs)

    tn = proj_ref.shape[1]
    coef_c, coef_s = [], []
    for k in range(tn // section):
        sec = n * (tn // section) + k
        is_rope = functools.reduce(jnp.logical_or, [sec == s for s in rope_sections])
        scaled = functools.reduce(jnp.logical_or, [sec == s for s in scaled_sections])
        scale = jnp.where(scaled, q_scale, 1.0).astype(F32)
        coef_c.append(jnp.where(is_rope, cos_scr[...], 1.0) * scale)
        coef_s.append(jnp.where(is_rope, sin_scr[...], 0.0) * scale)
    tm = h_scr.shape[0]
    for r0 in range(0, tm, row_tile):
        rows = slice(r0, r0 + row_tile)
        h = h_scr[rows, :]
        for c0 in range(0, tn, MXU_COLS):
            acc = jnp.dot(h, w_ref[:, c0:c0 + MXU_COLS],
                          preferred_element_type=F32)
            cc, cs = coef_c[c0 // section][rows], coef_s[c0 // section][rows]
            for j in range(0, MXU_COLS, HEAD_DIM):
                a = acc[:, j:j + HEAD_DIM]
                r = a * cc + pltpu.roll(a, half, 1) * cs
                proj_ref[rows, c0 + j:c0 + j + HEAD_DIM] = r.astype(BF16)


def _in_proj(x2, pos, g, inv_full, w_main, w_f, *, tm, tn, section):
    t, d = x2.shape
    n_main = w_main.shape[1]
    assert tn % section == 0 and n_main % tn == 0 and section % MXU_COLS == 0
    kern = functools.partial(
        _in_proj_kernel, section=section, rope_sections=(0, 1), scaled_sections=(0, 4),
        q_scale=LOG2E / math.sqrt(HEAD_DIM), row_tile=512)
    return pl.pallas_call(
        kern,
        grid=(t // tm, n_main // tn),
        in_specs=[
            pl.BlockSpec((tm, d), lambda i, n: (i, 0)),
            pl.BlockSpec((tm, 1), lambda i, n: (i, 0)),
            _resident((1, d), lambda i, n: (0, 0)),
            _resident((1, LANES), lambda i, n: (0, 0)),
            pl.BlockSpec((d, tn), lambda i, n: (0, n)),
            _resident((d, LANES), lambda i, n: (0, 0)),
        ],
        out_specs=[
            pl.BlockSpec((tm, tn), lambda i, n: (i, n)),
            pl.BlockSpec((tm, LANES), lambda i, n: (i, 0)),
        ],
        out_shape=[
            jax.ShapeDtypeStruct((t, n_main), BF16),
            jax.ShapeDtypeStruct((t, LANES), F32),
        ],
        scratch_shapes=[
            pltpu.VMEM((tm, d), BF16),
            pltpu.VMEM((tm, LANES), F32),
            pltpu.VMEM((tm, LANES), F32),
        ],
        compiler_params=_compiler_params(2),
        name="in_proj",
    )(x2, pos, g, inv_full, w_main, w_f)


FOX_TERMS = 3


def _fox_gate_kernel(f_ref, b_ref, e_ref, *, chunk, n_heads):
    n_chunks = f_ref.shape[0] // chunk
    r = lax.broadcasted_iota(jnp.int32, (chunk, chunk), 0)
    c = lax.broadcasted_iota(jnp.int32, (chunk, chunk), 1)
    tri = (c <= r).astype(F32)
    z = f_ref[...] + b_ref[...]
    logf = (jnp.minimum(z, 0.0) - jnp.log1p(jnp.exp(-jnp.abs(z)))) * LOG2E
    wide = jnp.concatenate([logf[j * chunk:(j + 1) * chunk] for j in range(n_chunks)], axis=1)
    local = jnp.dot(tri, wide, precision=lax.Precision.HIGHEST, preferred_element_type=F32)
    lane = lax.broadcasted_iota(jnp.int32, (chunk, LANES), 1)
    offset = jnp.zeros((1, LANES), F32)
    for j in range(n_chunks):
        csum = local[:, j * LANES:(j + 1) * LANES] + offset
        offset = csum[chunk - 1:chunk, :]
        hi = csum.astype(BF16).astype(F32)
        mid = (csum - hi).astype(BF16).astype(F32)
        lo = csum - hi - mid
        feats = jnp.where(
            lane < n_heads, hi,
            jnp.where(lane < 2 * n_heads, pltpu.roll(mid, n_heads, 1),
                      jnp.where(lane < 3 * n_heads, pltpu.roll(lo, 2 * n_heads, 1), 0.0)))
        e_ref[j * chunk:(j + 1) * chunk, :] = feats.astype(BF16)


def _fox_gate(f, b_pad, *, batch, seq, n_heads):
    assert FOX_TERMS * n_heads <= LANES
    return pl.pallas_call(
        functools.partial(_fox_gate_kernel, chunk=MOBA_BLOCK, n_heads=n_heads),
        grid=(batch,),
        in_specs=[
            pl.BlockSpec((seq, LANES), lambda b: (b, 0)),
            _resident((1, LANES), lambda b: (0, 0)),
        ],
        out_specs=pl.BlockSpec((seq, LANES), lambda b: (b, 0)),
        out_shape=jax.ShapeDtypeStruct(f.shape, BF16),
        compiler_params=_compiler_params(1),
        name="fox_gate",
    )(f, b_pad)


ATTN_GROUP = 4
VT_ROWS = HEAD_DIM + BF16_SUBLANES
KEY_PAIR = 2 * MOBA_BLOCK


def _attn_kernel(*refs, moba, n_blocks, group, n_heads):
    if moba:
        (q_ref, k_ref, v_ref, o_ref,
         vt_scr, kaug_scr, qaug_scr, s_scr, smax_scr, m_scr, acc_scr, kbar_scr) = refs
    else:
        (q_ref, k_ref, v_ref, e_ref, o_ref,
         vt_scr, kaug_scr, qaug_scr, s_scr, smax_scr, m_scr, acc_scr) = refs
    blk = MOBA_BLOCK
    pair = KEY_PAIR
    hg = pl.program_id(1)
    own = pl.program_id(2)
    head_cols = [slice(g * HEAD_DIM, (g + 1) * HEAD_DIM) for g in range(group)]
    extra = slice(HEAD_DIM, 2 * HEAD_DIM)

    @pl.when(own == 0)
    def _per_group_setup():
        ones_row = (lax.broadcasted_iota(jnp.int32, (BF16_SUBLANES, pair), 0) == 0).astype(BF16)
        lane = lax.broadcasted_iota(jnp.int32, (pair, LANES), 1)
        row = lax.broadcasted_iota(jnp.int32, (pair, LANES), 0)

        def body(a, carry):
            off = pl.multiple_of(a * pair, pair)
            if moba:
                block_of_row = 2 * a + (row >= blk).astype(jnp.int32)
                feats = jnp.where(lane == block_of_row, 1.0, 0.0).astype(BF16)
            else:
                feats = e_ref[pl.ds(off, pair), :]
            for g, cols in enumerate(head_cols):
                vt_scr[g, a, :HEAD_DIM, :] = (
                    v_ref[pl.ds(off, pair), cols].astype(F32).T.astype(BF16))
                vt_scr[g, a, HEAD_DIM:, :] = ones_row
                kb = k_ref[pl.ds(off, pair), cols]
                kaug_scr[g, pl.ds(off, pair), :HEAD_DIM] = kb
                kaug_scr[g, pl.ds(off, pair), extra] = feats
                if moba:
                    kf = kb.astype(F32)
                    kbar_scr[g, pl.ds(2 * a, 1), :] = jnp.mean(kf[:blk], axis=0, keepdims=True)
                    kbar_scr[g, pl.ds(2 * a + 1, 1), :] = jnp.mean(kf[blk:], axis=0,
                                                                   keepdims=True)
            return carry
        lax.fori_loop(0, n_blocks // 2, body, 0)

    def widen_queries(g):
        q = q_ref[:, head_cols[g]]
        qaug_scr[g, :, :HEAD_DIM] = q
        if moba:
            kbar = kbar_scr[g]
            hi = kbar.astype(BF16)
            mid = (kbar - hi.astype(F32)).astype(BF16)
            lo = (kbar - hi.astype(F32) - mid.astype(F32)).astype(BF16)
            gs3 = lax.dot_general(jnp.concatenate([hi, mid, lo], axis=0), q, NT_DIMS,
                                  preferred_element_type=F32)
            gs = gs3[:n_blocks] + gs3[n_blocks:2 * n_blocks] + gs3[2 * n_blocks:]
            blk_id = lax.broadcasted_iota(jnp.int32, gs.shape, 0)
            q_blk = 2 * own + (lax.broadcasted_iota(jnp.int32, gs.shape, 1) >= blk).astype(
                jnp.int32)
            gs = jnp.where(blk_id < q_blk, gs, NEG_INF)
            bias = jnp.where(blk_id == q_blk, 0.0, MASKED)
            for _ in range(MOBA_TOPK):
                mx = jnp.max(gs, axis=0, keepdims=True)
                first = jnp.min(jnp.where(gs == mx, blk_id, n_blocks), axis=0, keepdims=True)
                pick = jnp.logical_and(blk_id == first, mx > NEG_INF)
                bias = jnp.where(pick, 0.0, bias)
                gs = jnp.where(pick, NEG_INF, gs)
            bias = jnp.concatenate([bias, jnp.zeros((LANES - n_blocks, pair), F32)], axis=0)
            qaug_scr[g, :, extra] = bias.T.astype(BF16)
        else:
            lane = lax.broadcasted_iota(jnp.int32, (pair, LANES), 1)
            head = hg * group + g
            mine = functools.reduce(
                jnp.logical_or, [lane == head + j * n_heads for j in range(FOX_TERMS)])
            qaug_scr[g, :, extra] = jnp.where(mine, -1.0, 0.0).astype(BF16)

    def scores(g, a):
        off = pl.multiple_of(a * pair, pair)
        return lax.dot_general(kaug_scr[g, pl.ds(off, pair), :], qaug_scr[g], NT_DIMS,
                               preferred_element_type=F32)

    def stage_scores(slot, g, s):
        s_scr[slot, g] = s
        smax_scr[slot, g] = jnp.max(s, axis=0, keepdims=True)

    def fold_head(slot, g, a, first=False):
        m_new = smax_scr[slot, g]
        if not first:
            m_old = m_scr[g]
            m_new = jnp.maximum(m_old, m_new)
        p = jnp.exp2(s_scr[slot, g] - m_new).astype(BF16)
        pv = jnp.dot(vt_scr[g, a], p, preferred_element_type=F32)
        if not first:
            pv = jnp.exp2(m_old - m_new) * acc_scr[g] + pv
        acc_scr[g] = pv
        m_scr[g] = m_new

    visible = (lax.broadcasted_iota(jnp.int32, (pair, pair), 0)
               <= lax.broadcasted_iota(jnp.int32, (pair, pair), 1))
    for g in range(group):
        widen_queries(g)
    for g in range(group):
        stage_scores(1, g, jnp.where(visible, scores(g, own), NEG_INF))
    for g in range(group):
        stage_scores(0, g, scores(g, 0))
        fold_head(1, g, own, first=True)

    def pipelined(slot_next, a_next, slot_cur, a_cur):
        for g in range(group):
            stage_scores(slot_next, g, scores(g, a_next))
            fold_head(slot_cur, g, a_cur)

    def two_pairs(t, carry):
        a0 = 2 * t
        pipelined(1, a0 + 1, 0, a0)
        pipelined(0, a0 + 2, 1, a0 + 1)
        return carry

    is_odd = jnp.bitwise_and(own, 1) == 1
    n_full = jnp.where(is_odd, own - 1, jnp.maximum(own - 2, 0))
    lax.fori_loop(0, lax.shift_right_logical(n_full, 1), two_pairs, 0)

    @pl.when(is_odd)
    def _last_pair():
        for g in range(group):
            fold_head(0, g, own - 1)

    @pl.when(jnp.logical_and(jnp.logical_not(is_odd), own >= 2))
    def _last_two_pairs():
        for g in range(group):
            stage_scores(1, g, scores(g, own - 1))
            fold_head(0, g, own - 2)
        for g in range(group):
            fold_head(1, g, own - 1)

    for g, cols in enumerate(head_cols):
        acc = acc_scr[g]
        o = acc[:HEAD_DIM] * (1.0 / acc[HEAD_DIM:HEAD_DIM + 1])
        o_ref[:, cols] = o.T.astype(o_ref.dtype)


def _attention(proj, e, *, moba, batch, seq, heads, q_col, k_col, v_col):
    group = ATTN_GROUP
    tq = KEY_PAIR
    nq = seq // tq
    n_blocks = seq // MOBA_BLOCK
    width = group * HEAD_DIM
    assert heads % group == 0 and seq % KEY_PAIR == 0 and n_blocks <= LANES
    assert q_col % group == 0 and k_col % group == 0 and v_col % group == 0
    in_specs = [
        pl.BlockSpec((tq, width), lambda b, h, i: (b * nq + i, q_col // group + h)),
        pl.BlockSpec((seq, width), lambda b, h, i: (b, k_col // group + h)),
        pl.BlockSpec((seq, width), lambda b, h, i: (b, v_col // group + h)),
    ]
    args = [proj, proj, proj]
    if not moba:
        in_specs.append(pl.BlockSpec((seq, LANES), lambda b, h, i: (b, 0)))
        args.append(e)
    scratch = [
        pltpu.VMEM((group, seq // KEY_PAIR, VT_ROWS, KEY_PAIR), BF16),
        pltpu.VMEM((group, seq, 2 * HEAD_DIM), BF16),
        pltpu.VMEM((group, tq, 2 * HEAD_DIM), BF16),
        pltpu.VMEM((2, group, KEY_PAIR, tq), F32),
        pltpu.VMEM((2, group, 1, tq), F32),
        pltpu.VMEM((group, 1, tq), F32),
        pltpu.VMEM((group, VT_ROWS, tq), F32),
    ]
    if moba:
        scratch.append(pltpu.VMEM((group, n_blocks, HEAD_DIM), F32))
    return pl.pallas_call(
        functools.partial(_attn_kernel, moba=moba, n_blocks=n_blocks, group=group,
                          n_heads=heads),
        grid=(batch, heads // group, nq),
        in_specs=in_specs,
        out_specs=pl.BlockSpec((tq, width), lambda b, h, i: (b * nq + i, h)),
        out_shape=jax.ShapeDtypeStruct((batch * seq, heads * HEAD_DIM), BF16),
        scratch_shapes=scratch,
        compiler_params=_compiler_params(3),
        name="moba_attention" if moba else "fox_attention",
    )(*args)


def _rmsnorm(x, g):
    ms = jnp.mean(x * x, axis=-1, keepdims=True)
    return x * lax.rsqrt(ms + RMS_EPS) * g


def _out_merge_kernel(oa_ref, ob_ref, za_ref, zb_ref, ga_ref, gb_ref, x_ref, p_ref,
                      wa_ref, wb_ref, wo_ref, wg_ref, wu_ref, gple_ref, gfin_ref, out_ref,
                      *, final):
    def branch(o_ref, z_ref, w_ref):
        z = z_ref[...].astype(F32)
        a = o_ref[...].astype(F32) * (z * jax.nn.sigmoid(z))
        return jnp.dot(a.astype(BF16), w_ref[...], preferred_element_type=F32)

    p_in = p_ref[...].astype(BF16)
    d_half = out_ref.shape[1] // 2
    ya = branch(oa_ref, za_ref, wa_ref)
    yb = branch(ob_ref, zb_ref, wb_ref)
    up_lo = jnp.dot(p_in, wu_ref[:, :d_half], preferred_element_type=F32)
    mixed = (jax.nn.sigmoid(ga_ref[...].astype(F32)) * ya
             + jax.nn.sigmoid(gb_ref[...].astype(F32)) * yb)
    x1 = x_ref[...] + jnp.dot(mixed.astype(BF16), wo_ref[...], preferred_element_type=F32)
    up_hi = jnp.dot(p_in, wu_ref[:, d_half:], preferred_element_type=F32)
    hn = _rmsnorm(x1, gple_ref[...]).astype(BF16)
    pg = jax.nn.sigmoid(jnp.dot(hn, wg_ref[...], preferred_element_type=F32))
    x2 = x1 + jnp.concatenate([up_lo, up_hi], axis=1) * pg
    if final:
        x2 = _rmsnorm(x2, gfin_ref[...])
    out_ref[...] = x2


def _out_merge(oa, ob, proj, x2, p2, wa, wb, wo, wg, wu, g_ple, g_final, *, tm, final,
               za_col, zb_col, ga_col, gb_col):
    t, d = x2.shape
    wa_w = oa.shape[1]
    wb_w = ob.shape[1]
    ple = p2.shape[1]
    row = lambda i: (i, 0)
    const = lambda i: (0, 0)
    return pl.pallas_call(
        functools.partial(_out_merge_kernel, final=final),
        grid=(t // tm,),
        in_specs=[
            pl.BlockSpec((tm, wa_w), row),
            pl.BlockSpec((tm, wb_w), row),
            pl.BlockSpec((tm, wa_w), lambda i: (i, za_col)),
            pl.BlockSpec((tm, wb_w), lambda i: (i, zb_col)),
            pl.BlockSpec((tm, d), lambda i: (i, ga_col)),
            pl.BlockSpec((tm, d), lambda i: (i, gb_col)),
            pl.BlockSpec((tm, d), row),
            pl.BlockSpec((tm, ple), row),
            _resident((wa_w, d), const),
            _resident((wb_w, d), const),
            _resident((d, d), const),
            _resident((d, d), const),
            _resident((ple, d), const),
            _resident((1, d), const),
            _resident((1, d), const),
        ],
        out_specs=pl.BlockSpec((tm, d), row),
        out_shape=jax.ShapeDtypeStruct((t, d), F32),
        compiler_params=_compiler_params(1),
        name="out_merge",
    )(oa, ob, proj, proj, proj, proj, x2, p2, wa, wb, wo, wg, wu, g_ple, g_final)


def kernel(x, p, positions, g_norm, w_in, b_f, w_branch_a, w_branch_b, w_out,
           g_ple, w_ple_gate, w_ple_up, g_final):
    batch, seq, d = x.shape
    depth = w_in.shape[0]
    t = batch * seq
    wa_w = w_branch_a.shape[1]
    wb_w = w_branch_b.shape[1]
    heads_a = wa_w // HEAD_DIM
    heads_b = wb_w // HEAD_DIM
    n_f = b_f.shape[1]
    assert seq % MOBA_BLOCK == 0 and wa_w == wb_w and 2 * wa_w == d
    f_start = 4 * wa_w + 3 * wb_w
    assert w_in.shape[2] == f_start + n_f + wb_w + 2 * d

    inv = ROPE_THETA ** (-jnp.arange(0, HEAD_DIM, 2, dtype=F32) / HEAD_DIM)
    inv_full = jnp.concatenate([inv, inv]).reshape(1, HEAD_DIM)
    pos = positions.reshape(t, 1)
    x2 = x.reshape(t, d)
    hb = HEAD_DIM

    for layer in range(depth):
        w_t = jnp.swapaxes(w_in[layer], 0, 1)
        w_main, w_f = _w_in_prep(w_t, f_start=f_start, n_f=n_f, tn=wa_w)
        b_pad = jnp.pad(b_f[layer], (0, LANES - n_f)).reshape(1, LANES)

        proj, f = _in_proj(x2, pos, g_norm[layer].reshape(1, d), inv_full, w_main, w_f,
                           tm=1024, tn=2 * wa_w, section=wa_w)
        e = _fox_gate(f, b_pad, batch=batch, seq=seq, n_heads=heads_b)
        oa = _attention(proj, None, moba=True, batch=batch, seq=seq, heads=heads_a,
                        q_col=0, k_col=wa_w // hb, v_col=2 * wa_w // hb)
        ob = _attention(proj, e, moba=False, batch=batch, seq=seq, heads=heads_b,
                        q_col=4 * wa_w // hb, k_col=(4 * wa_w + wb_w) // hb,
                        v_col=(4 * wa_w + 2 * wb_w) // hb)
        x2 = _out_merge(
            oa, ob, proj, x2, p[layer].reshape(t, -1),
            w_branch_a[layer].astype(BF16), w_branch_b[layer].astype(BF16),
            w_out[layer].astype(BF16), w_ple_gate[layer].astype(BF16),
            w_ple_up[layer].astype(BF16), g_ple[layer].reshape(1, d), g_final.reshape(1, d),
            tm=256, final=(layer == depth - 1),
            za_col=3, zb_col=(4 * wa_w + 3 * wb_w) // wb_w,
            ga_col=(4 * wa_w + 4 * wb_w) // d, gb_col=(4 * wa_w + 4 * wb_w) // d + 1)
    return x2.reshape(batch, seq, d)
```

```python
import functools
import math

import jax
import jax.numpy as jnp
from jax import lax
from jax.experimental import pallas as pl
from jax.experimental.pallas import tpu as pltpu

HEAD_DIM = 128
MOBA_BLOCK = 256
MOBA_TOPK = 3
ROPE_THETA = 10000.0
RMS_EPS = 1e-6

V7X_VMEM_BYTES = 64 * 1024 * 1024
VMEM_LIMIT_BYTES = V7X_VMEM_BYTES * 7 // 8
LANES = 128
BF16_SUBLANES = 16

IN_PROJ_ROWS = 1024
IN_PROJ_ROW_PIECE = 512
IN_PROJ_SECTIONS = 2
OUT_MERGE_ROWS = 256

F32 = jnp.float32
BF16 = jnp.bfloat16
NEG_INF = float("-inf")
MASKED = -1e30
LOG2E = math.log2(math.e)
NT_DIMS = (((1,), (1,)), ((), ()))


def _compiler_params(n_grid_dims):
    return pltpu.CompilerParams(
        dimension_semantics=("arbitrary",) * n_grid_dims,
        vmem_limit_bytes=VMEM_LIMIT_BYTES,
    )


def _resident(block_shape, index_map):
    return pl.BlockSpec(block_shape, index_map, pipeline_mode=pl.Buffered(1))


F32_SUBLANES = 8
PREP_COLS = 512


def _w_in_prep_kernel(wt_hbm, out_ref, wf_ref, buf, fbuf, sem, fsem, *, n_lo, n_f):
    n = pl.program_id(0)
    tn = out_ref.shape[1]

    def fetch(j, slot):
        first_row = pl.multiple_of(j * tn + jnp.where(j >= n_lo, n_f, 0), F32_SUBLANES)
        return pltpu.make_async_copy(wt_hbm.at[pl.ds(first_row, tn), :], buf.at[slot],
                                     sem.at[slot])

    slot = jnp.bitwise_and(n, 1)

    @pl.when(n == 0)
    def _first():
        fetch(0, 0).start()
        gate_rows = pltpu.make_async_copy(wt_hbm.at[pl.ds(n_lo * tn, n_f), :], fbuf, fsem)
        gate_rows.start()
        gate_rows.wait()
        padded = jnp.concatenate(
            [fbuf[...], jnp.zeros((LANES - n_f, fbuf.shape[1]), F32)], axis=0)
        wf_ref[...] = padded.T.astype(BF16)

    @pl.when(n + 1 < pl.num_programs(0))
    def _ahead():
        fetch(n + 1, 1 - slot).start()

    fetch(n, slot).wait()
    for c0 in range(0, buf.shape[2], PREP_COLS):
        out_ref[c0:c0 + PREP_COLS, :] = buf[slot, :, c0:c0 + PREP_COLS].T.astype(BF16)


def _w_in_prep(w_t, *, f_start, n_f, tn):
    n_in, d = w_t.shape
    n_lo = f_start // tn
    n_tiles = (n_in - n_f) // tn
    assert f_start == n_lo * tn and n_in == n_tiles * tn + n_f
    assert n_f % F32_SUBLANES == 0 and n_f <= LANES and d % PREP_COLS == 0
    return pl.pallas_call(
        functools.partial(_w_in_prep_kernel, n_lo=n_lo, n_f=n_f),
        grid=(n_tiles,),
        in_specs=[pl.BlockSpec(memory_space=pl.ANY)],
        out_specs=[pl.BlockSpec((d, tn), lambda n: (0, n)),
                   pl.BlockSpec((d, LANES), lambda n: (0, 0))],
        out_shape=[jax.ShapeDtypeStruct((d, n_tiles * tn), BF16),
                   jax.ShapeDtypeStruct((d, LANES), BF16)],
        scratch_shapes=[pltpu.VMEM((2, tn, d), F32), pltpu.VMEM((n_f, d), F32),
                        pltpu.SemaphoreType.DMA((2,)), pltpu.SemaphoreType.DMA(())],
        compiler_params=_compiler_params(1),
        name="w_in_prep",
    )(w_t)


MXU_COLS = 256


def _in_proj_kernel(x_ref, pos_ref, g_ref, inv_ref, w_ref, wf_ref, proj_ref, f_ref,
                    h_scr, cos_scr, sin_scr, *, section, rope_sections, scaled_sections,
                    q_scale, row_tile):
    n = pl.program_id(1)
    half = HEAD_DIM // 2

    @pl.when(n == 0)
    def _normalize_and_tables():
        x = x_ref[...]
        ms = jnp.mean(x * x, axis=-1, keepdims=True)
        h = (x * lax.rsqrt(ms + RMS_EPS) * g_ref[...]).astype(BF16)
        h_scr[...] = h
        f_ref[...] = jnp.dot(h, wf_ref[...], preferred_element_type=F32)
        lane = lax.broadcasted_iota(jnp.int32, (x.shape[0], HEAD_DIM), 1)
        first = lane < half
        ang = pos_ref[...].astype(F32) * inv_ref[...]
        cs = jnp.cos(jnp.where(first, ang, ang - 0.5 * math.pi))
        sc = pltpu.roll(cs, half, 1)
        cos_scr[...] = jnp.where(first, cs, sc)
        sin_scr[...] = jnp.where(first, -sc, cs)

    tn = proj_ref.shape[1]
    coef_c, coef_s = [], []
    for k in range(tn // section):
        sec = n * (tn // section) + k
        is_rope = functools.reduce(jnp.logical_or, [sec == s for s in rope_sections])
        scaled = functools.reduce(jnp.logical_or, [sec == s for s in scaled_sections])
        scale = jnp.where(scaled, q_scale, 1.0).astype(F32)
        coef_c.append(jnp.where(is_rope, cos_scr[...], 1.0) * scale)
        coef_s.append(jnp.where(is_rope, sin_scr[...], 0.0) * scale)
    tm = h_scr.shape[0]
    for r0 in range(0, tm, row_tile):
        rows = slice(r0, r0 + row_tile)
        h = h_scr[rows, :]
        for c0 in range(0, tn, MXU_COLS):
            acc = jnp.dot(h, w_ref[:, c0:c0 + MXU_COLS],
                          preferred_element_type=F32)
            cc, cs = coef_c[c0 // section][rows], coef_s[c0 // section][rows]
            for j in range(0, MXU_COLS, HEAD_DIM):
                a = acc[:, j:j + HEAD_DIM]
                r = a * cc + pltpu.roll(a, half, 1) * cs
                proj_ref[rows, c0 + j:c0 + j + HEAD_DIM] = r.astype(BF16)


def _in_proj(x2, pos, g, inv_full, w_main, w_f, *, tm, tn, section):
    t, d = x2.shape
    n_main = w_main.shape[1]
    assert tn % section == 0 and n_main % tn == 0 and section % MXU_COLS == 0
    kern = functools.partial(
        _in_proj_kernel, section=section, rope_sections=(0, 1), scaled_sections=(0, 4),
        q_scale=LOG2E / math.sqrt(HEAD_DIM), row_tile=min(IN_PROJ_ROW_PIECE, tm))
    return pl.pallas_call(
        kern,
        grid=(t // tm, n_main // tn),
        in_specs=[
            pl.BlockSpec((tm, d), lambda i, n: (i, 0)),
            pl.BlockSpec((tm, 1), lambda i, n: (i, 0)),
            _resident((1, d), lambda i, n: (0, 0)),
            _resident((1, LANES), lambda i, n: (0, 0)),
            pl.BlockSpec((d, tn), lambda i, n: (0, n)),
            _resident((d, LANES), lambda i, n: (0, 0)),
        ],
        out_specs=[
            pl.BlockSpec((tm, tn), lambda i, n: (i, n)),
            pl.BlockSpec((tm, LANES), lambda i, n: (i, 0)),
        ],
        out_shape=[
            jax.ShapeDtypeStruct((t, n_main), BF16),
            jax.ShapeDtypeStruct((t, LANES), F32),
        ],
        scratch_shapes=[
            pltpu.VMEM((tm, d), BF16),
            pltpu.VMEM((tm, LANES), F32),
            pltpu.VMEM((tm, LANES), F32),
        ],
        compiler_params=_compiler_params(2),
        name="in_proj",
    )(x2, pos, g, inv_full, w_main, w_f)


FOX_TERMS = 3


def _fox_gate_kernel(f_ref, b_ref, e_ref, *, chunk, n_heads):
    n_chunks = f_ref.shape[0] // chunk
    r = lax.broadcasted_iota(jnp.int32, (chunk, chunk), 0)
    c = lax.broadcasted_iota(jnp.int32, (chunk, chunk), 1)
    tri = (c <= r).astype(BF16)
    z = f_ref[...] + b_ref[...]
    logf = (jnp.minimum(z, 0.0) - jnp.log1p(jnp.exp(-jnp.abs(z)))) * LOG2E
    wide = jnp.concatenate([logf[j * chunk:(j + 1) * chunk] for j in range(n_chunks)], axis=1)
    local = jnp.zeros_like(wide)
    for _ in range(FOX_TERMS):
        term = wide.astype(BF16)
        local = local + jnp.dot(tri, term, preferred_element_type=F32)
        wide = wide - term.astype(F32)
    lane = lax.broadcasted_iota(jnp.int32, (chunk, LANES), 1)
    offset = jnp.zeros((1, LANES), F32)
    for j in range(n_chunks):
        csum = local[:, j * LANES:(j + 1) * LANES] + offset
        offset = csum[chunk - 1:chunk, :]
        hi = csum.astype(BF16).astype(F32)
        mid = (csum - hi).astype(BF16).astype(F32)
        lo = csum - hi - mid
        feats = jnp.where(
            lane < n_heads, hi,
            jnp.where(lane < 2 * n_heads, pltpu.roll(mid, n_heads, 1),
                      jnp.where(lane < 3 * n_heads, pltpu.roll(lo, 2 * n_heads, 1), 0.0)))
        e_ref[j * chunk:(j + 1) * chunk, :] = feats.astype(BF16)


def _fox_gate(f, b_pad, *, batch, seq, n_heads):
    assert FOX_TERMS * n_heads <= LANES
    return pl.pallas_call(
        functools.partial(_fox_gate_kernel, chunk=MOBA_BLOCK, n_heads=n_heads),
        grid=(batch,),
        in_specs=[
            pl.BlockSpec((seq, LANES), lambda b: (b, 0)),
            _resident((1, LANES), lambda b: (0, 0)),
        ],
        out_specs=pl.BlockSpec((seq, LANES), lambda b: (b, 0)),
        out_shape=jax.ShapeDtypeStruct(f.shape, BF16),
        compiler_params=_compiler_params(1),
        name="fox_gate",
    )(f, b_pad)


ATTN_GROUP = 4
VT_ROWS = HEAD_DIM + BF16_SUBLANES
KEY_PAIR = 2 * MOBA_BLOCK


def _attn_kernel(*refs, moba, n_blocks, group, n_heads):
    if moba:
        (q_ref, k_ref, v_ref, o_ref,
         vt_scr, kaug_scr, qaug_scr, s_scr, smax_scr, m_scr, acc_scr, kbar_scr) = refs
    else:
        (q_ref, k_ref, v_ref, e_ref, o_ref,
         vt_scr, kaug_scr, qaug_scr, s_scr, smax_scr, m_scr, acc_scr) = refs
    blk = MOBA_BLOCK
    pair = KEY_PAIR
    hg = pl.program_id(1)
    own = pl.program_id(2)
    head_cols = [slice(g * HEAD_DIM, (g + 1) * HEAD_DIM) for g in range(group)]
    extra = slice(HEAD_DIM, 2 * HEAD_DIM)

    @pl.when(own == 0)
    def _per_group_setup():
        ones_row = (lax.broadcasted_iota(jnp.int32, (BF16_SUBLANES, pair), 0) == 0).astype(BF16)
        lane = lax.broadcasted_iota(jnp.int32, (pair, LANES), 1)
        row = lax.broadcasted_iota(jnp.int32, (pair, LANES), 0)

        def body(a, carry):
            off = pl.multiple_of(a * pair, pair)
            if moba:
                block_of_row = 2 * a + (row >= blk).astype(jnp.int32)
                feats = jnp.where(lane == block_of_row, 1.0, 0.0).astype(BF16)
            else:
                feats = e_ref[pl.ds(off, pair), :]
            for g, cols in enumerate(head_cols):
                vt_scr[g, a, :HEAD_DIM, :] = (
                    v_ref[pl.ds(off, pair), cols].T)
                vt_scr[g, a, HEAD_DIM:, :] = ones_row
                kb = k_ref[pl.ds(off, pair), cols]
                kaug_scr[g, pl.ds(off, pair), :HEAD_DIM] = kb
                kaug_scr[g, pl.ds(off, pair), extra] = feats
                if moba:
                    kf = kb.astype(F32)
                    kbar_scr[g, pl.ds(2 * a, 1), :] = jnp.mean(kf[:blk], axis=0, keepdims=True)
                    kbar_scr[g, pl.ds(2 * a + 1, 1), :] = jnp.mean(kf[blk:], axis=0,
                                                                   keepdims=True)
            return carry
        lax.fori_loop(0, n_blocks // 2, body, 0)

    def widen_queries(g):
        q = q_ref[:, head_cols[g]]
        qaug_scr[g, :, :HEAD_DIM] = q
        if moba:
            kbar = kbar_scr[g]
            hi = kbar.astype(BF16)
            mid = (kbar - hi.astype(F32)).astype(BF16)
            lo = (kbar - hi.astype(F32) - mid.astype(F32)).astype(BF16)
            gs3 = lax.dot_general(jnp.concatenate([hi, mid, lo], axis=0), q, NT_DIMS,
                                  preferred_element_type=F32)
            gs = gs3[:n_blocks] + gs3[n_blocks:2 * n_blocks] + gs3[2 * n_blocks:]
            blk_id = lax.broadcasted_iota(jnp.int32, gs.shape, 0)
            q_blk = 2 * own + (lax.broadcasted_iota(jnp.int32, gs.shape, 1) >= blk).astype(
                jnp.int32)
            gs = jnp.where(blk_id < q_blk, gs, NEG_INF)
            bias = jnp.where(blk_id == q_blk, 0.0, MASKED)
            for _ in range(MOBA_TOPK):
                mx = jnp.max(gs, axis=0, keepdims=True)
                first = jnp.min(jnp.where(gs == mx, blk_id, n_blocks), axis=0, keepdims=True)
                pick = jnp.logical_and(blk_id == first, mx > NEG_INF)
                bias = jnp.where(pick, 0.0, bias)
                gs = jnp.where(pick, NEG_INF, gs)
            bias = jnp.concatenate([bias, jnp.zeros((LANES - n_blocks, pair), F32)], axis=0)
            qaug_scr[g, :, extra] = bias.T.astype(BF16)
        else:
            lane = lax.broadcasted_iota(jnp.int32, (pair, LANES), 1)
            head = hg * group + g
            mine = functools.reduce(
                jnp.logical_or, [lane == head + j * n_heads for j in range(FOX_TERMS)])
            qaug_scr[g, :, extra] = jnp.where(mine, -1.0, 0.0).astype(BF16)

    def scores(g, a):
        off = pl.multiple_of(a * pair, pair)
        return lax.dot_general(kaug_scr[g, pl.ds(off, pair), :], qaug_scr[g], NT_DIMS,
                               preferred_element_type=F32)

    def stage_scores(slot, g, s):
        s_scr[slot, g] = s
        smax_scr[slot, g] = jnp.max(s, axis=0, keepdims=True)

    def fold_head(slot, g, a, first=False):
        m_new = smax_scr[slot, g]
        if not first:
            m_old = m_scr[g]
            m_new = jnp.maximum(m_old, m_new)
        p = jnp.exp2(s_scr[slot, g] - m_new).astype(BF16)
        pv = jnp.dot(vt_scr[g, a], p, preferred_element_type=F32)
        if not first:
            pv = jnp.exp2(m_old - m_new) * acc_scr[g] + pv
        acc_scr[g] = pv
        m_scr[g] = m_new

    visible = (lax.broadcasted_iota(jnp.int32, (pair, pair), 0)
               <= lax.broadcasted_iota(jnp.int32, (pair, pair), 1))
    for g in range(group):
        widen_queries(g)
    for g in range(group):
        stage_scores(1, g, jnp.where(visible, scores(g, own), NEG_INF))
    for g in range(group):
        stage_scores(0, g, scores(g, 0))
        fold_head(1, g, own, first=True)

    def pipelined(slot_next, a_next, slot_cur, a_cur):
        for g in range(group):
            stage_scores(slot_next, g, scores(g, a_next))
            fold_head(slot_cur, g, a_cur)

    def two_pairs(t, carry):
        a0 = 2 * t
        pipelined(1, a0 + 1, 0, a0)
        pipelined(0, a0 + 2, 1, a0 + 1)
        return carry

    is_odd = jnp.bitwise_and(own, 1) == 1
    n_full = jnp.where(is_odd, own - 1, jnp.maximum(own - 2, 0))
    lax.fori_loop(0, lax.shift_right_logical(n_full, 1), two_pairs, 0)

    @pl.when(is_odd)
    def _last_pair():
        for g in range(group):
            fold_head(0, g, own - 1)

    @pl.when(jnp.logical_and(jnp.logical_not(is_odd), own >= 2))
    def _last_two_pairs():
        for g in range(group):
            stage_scores(1, g, scores(g, own - 1))
            fold_head(0, g, own - 2)
        for g in range(group):
            fold_head(1, g, own - 1)

    for g, cols in enumerate(head_cols):
        acc = acc_scr[g]
        o = acc[:HEAD_DIM] * (1.0 / acc[HEAD_DIM:HEAD_DIM + 1])
        o_ref[:, cols] = o.T.astype(o_ref.dtype)


def _attention(proj, e, *, moba, batch, seq, heads, q_col, k_col, v_col):
    group = ATTN_GROUP
    tq = KEY_PAIR
    nq = seq // tq
    n_blocks = seq // MOBA_BLOCK
    width = group * HEAD_DIM
    assert heads % group == 0 and seq % KEY_PAIR == 0 and n_blocks <= LANES
    assert q_col % group == 0 and k_col % group == 0 and v_col % group == 0
    in_specs = [
        pl.BlockSpec((tq, width), lambda b, h, i: (b * nq + i, q_col // group + h)),
        pl.BlockSpec((seq, width), lambda b, h, i: (b, k_col // group + h)),
        pl.BlockSpec((seq, width), lambda b, h, i: (b, v_col // group + h)),
    ]
    args = [proj, proj, proj]
    if not moba:
        in_specs.append(pl.BlockSpec((seq, LANES), lambda b, h, i: (b, 0)))
        args.append(e)
    scratch = [
        pltpu.VMEM((group, seq // KEY_PAIR, VT_ROWS, KEY_PAIR), BF16),
        pltpu.VMEM((group, seq, 2 * HEAD_DIM), BF16),
        pltpu.VMEM((group, tq, 2 * HEAD_DIM), BF16),
        pltpu.VMEM((2, group, KEY_PAIR, tq), F32),
        pltpu.VMEM((2, group, 1, tq), F32),
        pltpu.VMEM((group, 1, tq), F32),
        pltpu.VMEM((group, VT_ROWS, tq), F32),
    ]
    if moba:
        scratch.append(pltpu.VMEM((group, n_blocks, HEAD_DIM), F32))
    return pl.pallas_call(
        functools.partial(_attn_kernel, moba=moba, n_blocks=n_blocks, group=group,
                          n_heads=heads),
        grid=(batch, heads // group, nq),
        in_specs=in_specs,
        out_specs=pl.BlockSpec((tq, width), lambda b, h, i: (b * nq + i, h)),
        out_shape=jax.ShapeDtypeStruct((batch * seq, heads * HEAD_DIM), BF16),
        scratch_shapes=scratch,
        compiler_params=_compiler_params(3),
        name="moba_attention" if moba else "fox_attention",
    )(*args)


def _rmsnorm(x, g):
    ms = jnp.mean(x * x, axis=-1, keepdims=True)
    return x * lax.rsqrt(ms + RMS_EPS) * g


def _out_merge_kernel(oa_ref, ob_ref, za_ref, zb_ref, ga_ref, gb_ref, x_ref, p_ref,
                      wa_ref, wb_ref, wo_ref, wg_ref, wu_ref, gple_ref, gfin_ref, out_ref,
                      *, final):
    def branch(o_ref, z_ref, w_ref):
        z = z_ref[...].astype(F32)
        a = o_ref[...].astype(F32) * (z * jax.nn.sigmoid(z))
        return jnp.dot(a.astype(BF16), w_ref[...], preferred_element_type=F32)

    p_in = p_ref[...].astype(BF16)
    d_half = out_ref.shape[1] // 2
    ya = branch(oa_ref, za_ref, wa_ref)
    yb = branch(ob_ref, zb_ref, wb_ref)
    up_lo = jnp.dot(p_in, wu_ref[:, :d_half], preferred_element_type=F32)
    mixed = (jax.nn.sigmoid(ga_ref[...].astype(F32)) * ya
             + jax.nn.sigmoid(gb_ref[...].astype(F32)) * yb)
    x1 = x_ref[...] + jnp.dot(mixed.astype(BF16), wo_ref[...], preferred_element_type=F32)
    up_hi = jnp.dot(p_in, wu_ref[:, d_half:], preferred_element_type=F32)
    hn = _rmsnorm(x1, gple_ref[...]).astype(BF16)
    pg = jax.nn.sigmoid(jnp.dot(hn, wg_ref[...], preferred_element_type=F32))
    x2 = x1 + jnp.concatenate([up_lo, up_hi], axis=1) * pg
    if final:
        x2 = _rmsnorm(x2, gfin_ref[...])
    out_ref[...] = x2


def _out_merge(oa, ob, proj, x2, p2, wa, wb, wo, wg, wu, g_ple, g_final, *, tm, final,
               za_col, zb_col, ga_col, gb_col):
    t, d = x2.shape
    wa_w = oa.shape[1]
    wb_w = ob.shape[1]
    ple = p2.shape[1]
    row = lambda i: (i, 0)
    const = lambda i: (0, 0)
    return pl.pallas_call(
        functools.partial(_out_merge_kernel, final=final),
        grid=(t // tm,),
        in_specs=[
            pl.BlockSpec((tm, wa_w), row),
            pl.BlockSpec((tm, wb_w), row),
            pl.BlockSpec((tm, wa_w), lambda i: (i, za_col)),
            pl.BlockSpec((tm, wb_w), lambda i: (i, zb_col)),
            pl.BlockSpec((tm, d), lambda i: (i, ga_col)),
            pl.BlockSpec((tm, d), lambda i: (i, gb_col)),
            pl.BlockSpec((tm, d), row),
            pl.BlockSpec((tm, ple), row),
            _resident((wa_w, d), const),
            _resident((wb_w, d), const),
            _resident((d, d), const),
            _resident((d, d), const),
            _resident((ple, d), const),
            _resident((1, d), const),
            _resident((1, d), const),
        ],
        out_specs=pl.BlockSpec((tm, d), row),
        out_shape=jax.ShapeDtypeStruct((t, d), F32),
        compiler_params=_compiler_params(1),
        name="out_merge",
    )(oa, ob, proj, proj, proj, proj, x2, p2, wa, wb, wo, wg, wu, g_ple, g_final)


def kernel(x, p, positions, g_norm, w_in, b_f, w_branch_a, w_branch_b, w_out,
           g_ple, w_ple_gate, w_ple_up, g_final):
    batch, seq, d = x.shape
    depth = w_in.shape[0]
    t = batch * seq
    wa_w = w_branch_a.shape[1]
    wb_w = w_branch_b.shape[1]
    heads_a = wa_w // HEAD_DIM
    heads_b = wb_w // HEAD_DIM
    n_f = b_f.shape[1]
    assert seq % MOBA_BLOCK == 0 and wa_w == wb_w and 2 * wa_w == d
    f_start = 4 * wa_w + 3 * wb_w
    assert w_in.shape[2] == f_start + n_f + wb_w + 2 * d

    inv = ROPE_THETA ** (-jnp.arange(0, HEAD_DIM, 2, dtype=F32) / HEAD_DIM)
    inv_full = jnp.concatenate([inv, inv]).reshape(1, HEAD_DIM)
    pos = positions.reshape(t, 1)
    x2 = x.reshape(t, d)
    hb = HEAD_DIM

    for layer in range(depth):
        w_t = jnp.swapaxes(w_in[layer], 0, 1)
        w_main, w_f = _w_in_prep(w_t, f_start=f_start, n_f=n_f, tn=wa_w)
        b_pad = jnp.pad(b_f[layer], (0, LANES - n_f)).reshape(1, LANES)

        proj, f = _in_proj(x2, pos, g_norm[layer].reshape(1, d), inv_full, w_main, w_f,
                           tm=IN_PROJ_ROWS, tn=IN_PROJ_SECTIONS * wa_w, section=wa_w)
        e = _fox_gate(f, b_pad, batch=batch, seq=seq, n_heads=heads_b)
        oa = _attention(proj, None, moba=True, batch=batch, seq=seq, heads=heads_a,
                        q_col=0, k_col=wa_w // hb, v_col=2 * wa_w // hb)
        ob = _attention(proj, e, moba=False, batch=batch, seq=seq, heads=heads_b,
                        q_col=4 * wa_w // hb, k_col=(4 * wa_w + wb_w) // hb,
                        v_col=(4 * wa_w + 2 * wb_w) // hb)
        x2 = _out_merge(
            oa, ob, proj, x2, p[layer].reshape(t, -1),
            w_branch_a[layer].astype(BF16), w_branch_b[layer].astype(BF16),
            w_out[layer].astype(BF16), w_ple_gate[layer].astype(BF16),
            w_ple_up[layer].astype(BF16), g_ple[layer].reshape(1, d), g_final.reshape(1, d),
            tm=OUT_MERGE_ROWS, final=(layer == depth - 1),
            za_col=3 * wa_w // wa_w, zb_col=(4 * wa_w + 3 * wb_w) // wb_w,
            ga_col=(4 * wa_w + 4 * wb_w) // d, gb_col=(4 * wa_w + 4 * wb_w) // d + 1)
    return x2.reshape(batch, seq, d)
```

```python
import functools
import math

import jax
import jax.numpy as jnp
from jax import lax
from jax.experimental import pallas as pl
from jax.experimental.pallas import tpu as pltpu

HEAD_DIM = 128
MOBA_BLOCK = 256
MOBA_TOPK = 3
ROPE_THETA = 10000.0
RMS_EPS = 1e-6

V7X_VMEM_BYTES = 64 * 1024 * 1024
VMEM_LIMIT_BYTES = V7X_VMEM_BYTES * 7 // 8
LANES = 128
BF16_SUBLANES = 16

IN_PROJ_ROWS = 1024
IN_PROJ_ROW_PIECE = 512
IN_PROJ_SECTIONS = 2
OUT_MERGE_ROWS = 256

F32 = jnp.float32
BF16 = jnp.bfloat16
NEG_INF = float("-inf")
MASKED = -1e30
LOG2E = math.log2(math.e)
NT_DIMS = (((1,), (1,)), ((), ()))


def _compiler_params(n_grid_dims):
    return pltpu.CompilerParams(
        dimension_semantics=("arbitrary",) * n_grid_dims,
        vmem_limit_bytes=VMEM_LIMIT_BYTES,
    )


def _resident(block_shape, index_map):
    return pl.BlockSpec(block_shape, index_map, pipeline_mode=pl.Buffered(1))


F32_SUBLANES = 8
PREP_COLS = 512


def _w_in_prep_kernel(wt_hbm, out_ref, wf_ref, buf, fbuf, sem, fsem, *, n_lo, n_f):
    n = pl.program_id(0)
    tn = out_ref.shape[1]

    def fetch(j, slot):
        first_row = pl.multiple_of(j * tn + jnp.where(j >= n_lo, n_f, 0), F32_SUBLANES)
        return pltpu.make_async_copy(wt_hbm.at[pl.ds(first_row, tn), :], buf.at[slot],
                                     sem.at[slot])

    slot = jnp.bitwise_and(n, 1)

    @pl.when(n == 0)
    def _first():
        fetch(0, 0).start()
        gate_rows = pltpu.make_async_copy(wt_hbm.at[pl.ds(n_lo * tn, n_f), :], fbuf, fsem)
        gate_rows.start()
        gate_rows.wait()
        padded = jnp.concatenate(
            [fbuf[...], jnp.zeros((LANES - n_f, fbuf.shape[1]), F32)], axis=0)
        wf_ref[...] = padded.T.astype(BF16)

    @pl.when(n + 1 < pl.num_programs(0))
    def _ahead():
        fetch(n + 1, 1 - slot).start()

    fetch(n, slot).wait()
    for c0 in range(0, buf.shape[2], PREP_COLS):
        out_ref[c0:c0 + PREP_COLS, :] = buf[slot, :, c0:c0 + PREP_COLS].T.astype(BF16)


def _w_in_prep(w_t, *, f_start, n_f, tn):
    n_in, d = w_t.shape
    n_lo = f_start // tn
    n_tiles = (n_in - n_f) // tn
    assert f_start == n_lo * tn and n_in == n_tiles * tn + n_f
    assert n_f % F32_SUBLANES == 0 and n_f <= LANES and d % PREP_COLS == 0
    return pl.pallas_call(
        functools.partial(_w_in_prep_kernel, n_lo=n_lo, n_f=n_f),
        grid=(n_tiles,),
        in_specs=[pl.BlockSpec(memory_space=pl.ANY)],
        out_specs=[pl.BlockSpec((d, tn), lambda n: (0, n)),
                   pl.BlockSpec((d, LANES), lambda n: (0, 0))],
        out_shape=[jax.ShapeDtypeStruct((d, n_tiles * tn), BF16),
                   jax.ShapeDtypeStruct((d, LANES), BF16)],
        scratch_shapes=[pltpu.VMEM((2, tn, d), F32), pltpu.VMEM((n_f, d), F32),
                        pltpu.SemaphoreType.DMA((2,)), pltpu.SemaphoreType.DMA(())],
        compiler_params=_compiler_params(1),
        name="w_in_prep",
    )(w_t)


MXU_COLS = 256


def _in_proj_kernel(x_ref, pos_ref, g_ref, inv_ref, w_ref, wf_ref, proj_ref, f_ref,
                    h_scr, cos_scr, sin_scr, *, section, rope_sections, scaled_sections,
                    q_scale, row_tile):
    n = pl.program_id(1)
    half = HEAD_DIM // 2

    @pl.when(n == 0)
    def _normalize_and_tables():
        x = x_ref[...]
        ms = jnp.mean(x * x, axis=-1, keepdims=True)
        h = (x * lax.rsqrt(ms + RMS_EPS) * g_ref[...]).astype(BF16)
        h_scr[...] = h
        f_ref[...] = jnp.dot(h, wf_ref[...], preferred_element_type=F32)
        lane = lax.broadcasted_iota(jnp.int32, (x.shape[0], HEAD_DIM), 1)
        first = lane < half
        ang = pos_ref[...].astype(F32) * inv_ref[...]
        cs = jnp.cos(jnp.where(first, ang, ang - 0.5 * math.pi))
        sc = pltpu.roll(cs, half, 1)
        cos_scr[...] = jnp.where(first, cs, sc)
        sin_scr[...] = jnp.where(first, -sc, cs)

    tn = proj_ref.shape[1]
    coef_c, coef_s = [], []
    for k in range(tn // section):
        sec = n * (tn // section) + k
        is_rope = functools.reduce(jnp.logical_or, [sec == s for s in rope_sections])
        scaled = functools.reduce(jnp.logical_or, [sec == s for s in scaled_sections])
        scale = jnp.where(scaled, q_scale, 1.0).astype(F32)
        coef_c.append(jnp.where(is_rope, cos_scr[...], 1.0) * scale)
        coef_s.append(jnp.where(is_rope, sin_scr[...], 0.0) * scale)
    tm = h_scr.shape[0]
    for r0 in range(0, tm, row_tile):
        rows = slice(r0, r0 + row_tile)
        h = h_scr[rows, :]
        for c0 in range(0, tn, MXU_COLS):
            acc = jnp.dot(h, w_ref[:, c0:c0 + MXU_COLS],
                          preferred_element_type=F32)
            cc, cs = coef_c[c0 // section][rows], coef_s[c0 // section][rows]
            for j in range(0, MXU_COLS, HEAD_DIM):
                a = acc[:, j:j + HEAD_DIM]
                r = a * cc + pltpu.roll(a, half, 1) * cs
                proj_ref[rows, c0 + j:c0 + j + HEAD_DIM] = r.astype(BF16)


def _in_proj(x2, pos, g, inv_full, w_main, w_f, *, tm, tn, section):
    t, d = x2.shape
    n_main = w_main.shape[1]
    assert tn % section == 0 and n_main % tn == 0 and section % MXU_COLS == 0
    kern = functools.partial(
        _in_proj_kernel, section=section, rope_sections=(0, 1), scaled_sections=(0, 4),
        q_scale=LOG2E / math.sqrt(HEAD_DIM), row_tile=min(IN_PROJ_ROW_PIECE, tm))
    return pl.pallas_call(
        kern,
        grid=(t // tm, n_main // tn),
        in_specs=[
            pl.BlockSpec((tm, d), lambda i, n: (i, 0)),
            pl.BlockSpec((tm, 1), lambda i, n: (i, 0)),
            _resident((1, d), lambda i, n: (0, 0)),
            _resident((1, LANES), lambda i, n: (0, 0)),
            pl.BlockSpec((d, tn), lambda i, n: (0, n)),
            _resident((d, LANES), lambda i, n: (0, 0)),
        ],
        out_specs=[
            pl.BlockSpec((tm, tn), lambda i, n: (i, n)),
            pl.BlockSpec((tm, LANES), lambda i, n: (i, 0)),
        ],
        out_shape=[
            jax.ShapeDtypeStruct((t, n_main), BF16),
            jax.ShapeDtypeStruct((t, LANES), F32),
        ],
        scratch_shapes=[
            pltpu.VMEM((tm, d), BF16),
            pltpu.VMEM((tm, LANES), F32),
            pltpu.VMEM((tm, LANES), F32),
        ],
        compiler_params=_compiler_params(2),
        name="in_proj",
    )(x2, pos, g, inv_full, w_main, w_f)


FOX_TERMS = 3


def _fox_gate_kernel(f_ref, b_ref, e_ref, *, chunk, n_heads):
    n_chunks = f_ref.shape[0] // chunk
    r = lax.broadcasted_iota(jnp.int32, (chunk, chunk), 0)
    c = lax.broadcasted_iota(jnp.int32, (chunk, chunk), 1)
    tri = (c <= r).astype(BF16)
    z = f_ref[...] + b_ref[...]
    logf = (jnp.minimum(z, 0.0) - jnp.log1p(jnp.exp(-jnp.abs(z)))) * LOG2E
    wide = jnp.concatenate([logf[j * chunk:(j + 1) * chunk] for j in range(n_chunks)], axis=1)
    local = jnp.zeros_like(wide)
    for _ in range(FOX_TERMS):
        term = wide.astype(BF16)
        local = local + jnp.dot(tri, term, preferred_element_type=F32)
        wide = wide - term.astype(F32)
    lane = lax.broadcasted_iota(jnp.int32, (chunk, LANES), 1)
    offset = jnp.zeros((1, LANES), F32)
    for j in range(n_chunks):
        csum = local[:, j * LANES:(j + 1) * LANES] + offset
        offset = csum[chunk - 1:chunk, :]
        hi = csum.astype(BF16).astype(F32)
        mid = (csum - hi).astype(BF16).astype(F32)
        lo = csum - hi - mid
        feats = jnp.where(
            lane < n_heads, hi,
            jnp.where(lane < 2 * n_heads, pltpu.roll(mid, n_heads, 1),
                      jnp.where(lane < 3 * n_heads, pltpu.roll(lo, 2 * n_heads, 1), 0.0)))
        e_ref[j * chunk:(j + 1) * chunk, :] = feats.astype(BF16)


def _fox_gate(f, b_pad, *, batch, seq, n_heads):
    assert FOX_TERMS * n_heads <= LANES
    return pl.pallas_call(
        functools.partial(_fox_gate_kernel, chunk=MOBA_BLOCK, n_heads=n_heads),
        grid=(batch,),
        in_specs=[
            pl.BlockSpec((seq, LANES), lambda b: (b, 0)),
            _resident((1, LANES), lambda b: (0, 0)),
        ],
        out_specs=pl.BlockSpec((seq, LANES), lambda b: (b, 0)),
        out_shape=jax.ShapeDtypeStruct(f.shape, BF16),
        compiler_params=_compiler_params(1),
        name="fox_gate",
    )(f, b_pad)


ATTN_GROUP = 4
VT_ROWS = HEAD_DIM + BF16_SUBLANES
KEY_PAIR = 2 * MOBA_BLOCK


def _attn_kernel(*refs, moba, n_blocks, group, n_heads):
    if moba:
        (q_ref, k_ref, v_ref, o_ref,
         vt_scr, kaug_scr, qaug_scr, s_scr, smax_scr, m_scr, acc_scr, kbar_scr) = refs
    else:
        (q_ref, k_ref, v_ref, e_ref, o_ref,
         vt_scr, kaug_scr, qaug_scr, s_scr, smax_scr, m_scr, acc_scr) = refs
    blk = MOBA_BLOCK
    pair = KEY_PAIR
    hg = pl.program_id(1)
    own = pl.program_id(2)
    head_cols = [slice(g * HEAD_DIM, (g + 1) * HEAD_DIM) for g in range(group)]
    extra = slice(HEAD_DIM, 2 * HEAD_DIM)

    @pl.when(own == 0)
    def _per_group_setup():
        ones_row = (lax.broadcasted_iota(jnp.int32, (BF16_SUBLANES, pair), 0) == 0).astype(BF16)
        lane = lax.broadcasted_iota(jnp.int32, (pair, LANES), 1)
        row = lax.broadcasted_iota(jnp.int32, (pair, LANES), 0)

        def body(a, carry):
            off = pl.multiple_of(a * pair, pair)
            if moba:
                block_of_row = 2 * a + (row >= blk).astype(jnp.int32)
                feats = jnp.where(lane == block_of_row, 1.0, 0.0).astype(BF16)
            else:
                feats = e_ref[pl.ds(off, pair), :]
            for g, cols in enumerate(head_cols):
                vt_scr[g, a, :HEAD_DIM, :] = (
                    v_ref[pl.ds(off, pair), cols].T)
                vt_scr[g, a, HEAD_DIM:, :] = ones_row
                kb = k_ref[pl.ds(off, pair), cols]
                kaug_scr[g, pl.ds(off, pair), :HEAD_DIM] = kb
                kaug_scr[g, pl.ds(off, pair), extra] = feats
                if moba:
                    kf = kb.astype(F32)
                    kbar_scr[g, pl.ds(2 * a, 1), :] = jnp.mean(kf[:blk], axis=0, keepdims=True)
                    kbar_scr[g, pl.ds(2 * a + 1, 1), :] = jnp.mean(kf[blk:], axis=0,
                                                                   keepdims=True)
            return carry
        lax.fori_loop(0, n_blocks // 2, body, 0)

    def widen_queries(g):
        q = q_ref[:, head_cols[g]]
        qaug_scr[g, :, :HEAD_DIM] = q
        if moba:
            kbar = kbar_scr[g]
            hi = kbar.astype(BF16)
            mid = (kbar - hi.astype(F32)).astype(BF16)
            lo = (kbar - hi.astype(F32) - mid.astype(F32)).astype(BF16)
            gs3 = lax.dot_general(jnp.concatenate([hi, mid, lo], axis=0), q, NT_DIMS,
                                  preferred_element_type=F32)
            gs = gs3[:n_blocks] + gs3[n_blocks:2 * n_blocks] + gs3[2 * n_blocks:]
            blk_id = lax.broadcasted_iota(jnp.int32, gs.shape, 0)
            q_blk = 2 * own + (lax.broadcasted_iota(jnp.int32, gs.shape, 1) >= blk).astype(
                jnp.int32)
            gs = jnp.where(blk_id < q_blk, gs, NEG_INF)
            bias = jnp.where(blk_id == q_blk, 0.0, MASKED)
            for _ in range(MOBA_TOPK):
                mx = jnp.max(gs, axis=0, keepdims=True)
                first = jnp.min(jnp.where(gs == mx, blk_id, n_blocks), axis=0, keepdims=True)
                pick = jnp.logical_and(blk_id == first, mx > NEG_INF)
                bias = jnp.where(pick, 0.0, bias)
                gs = jnp.where(pick, NEG_INF, gs)
            bias = jnp.concatenate([bias, jnp.zeros((LANES - n_blocks, pair), F32)], axis=0)
            qaug_scr[g, :, extra] = bias.T.astype(BF16)
        else:
            lane = lax.broadcasted_iota(jnp.int32, (pair, LANES), 1)
            head = hg * group + g
            mine = functools.reduce(
                jnp.logical_or, [lane == head + j * n_heads for j in range(FOX_TERMS)])
            qaug_scr[g, :, extra] = jnp.where(mine, -1.0, 0.0).astype(BF16)

    def scores(g, a):
        off = pl.multiple_of(a * pair, pair)
        return lax.dot_general(kaug_scr[g, pl.ds(off, pair), :], qaug_scr[g], NT_DIMS,
                               preferred_element_type=F32)

    def stage_scores(slot, g, s):
        s_scr[slot, g] = s
        smax_scr[slot, g] = jnp.max(s, axis=0, keepdims=True)

    def fold_head(slot, g, a, first=False):
        m_new = smax_scr[slot, g]
        if not first:
            m_old = m_scr[g]
            m_new = jnp.maximum(m_old, m_new)
        p = jnp.exp2(s_scr[slot, g] - m_new).astype(BF16)
        pv = jnp.dot(vt_scr[g, a], p, preferred_element_type=F32)
        if not first:
            pv = jnp.exp2(m_old - m_new) * acc_scr[g] + pv
        acc_scr[g] = pv
        m_scr[g] = m_new

    own_off = pl.multiple_of(own * pair, pair)
    key_id = lax.broadcasted_iota(jnp.int32, (pair, blk), 0)
    qry_id = lax.broadcasted_iota(jnp.int32, (pair, blk), 1)
    visible_lo = (lax.broadcasted_iota(jnp.int32, (blk, blk), 0)
                  <= lax.broadcasted_iota(jnp.int32, (blk, blk), 1))
    visible_hi = key_id <= qry_id + blk
    for g in range(group):
        widen_queries(g)
    for g in range(group):
        s_lo = lax.dot_general(kaug_scr[g, pl.ds(own_off, blk), :], qaug_scr[g, :blk, :],
                               NT_DIMS, preferred_element_type=F32)
        s_hi = lax.dot_general(kaug_scr[g, pl.ds(own_off, pair), :], qaug_scr[g, blk:, :],
                               NT_DIMS, preferred_element_type=F32)
        s_scr[1, g, :blk, :blk] = jnp.where(visible_lo, s_lo, NEG_INF)
        s_scr[1, g, :, blk:] = jnp.where(visible_hi, s_hi, NEG_INF)
    for g in range(group):
        stage_scores(0, g, scores(g, 0))
        s_lo = s_scr[1, g, :blk, :blk]
        s_hi = s_scr[1, g, :, blk:]
        m_lo = jnp.max(s_lo, axis=0, keepdims=True)
        m_hi = jnp.max(s_hi, axis=0, keepdims=True)
        p_lo = jnp.exp2(s_lo - m_lo).astype(BF16)
        p_hi = jnp.exp2(s_hi - m_hi).astype(BF16)
        acc_scr[g, :, :blk] = jnp.dot(vt_scr[g, own, :, :blk], p_lo, preferred_element_type=F32)
        acc_scr[g, :, blk:] = jnp.dot(vt_scr[g, own], p_hi, preferred_element_type=F32)
        m_scr[g, :, :blk] = m_lo
        m_scr[g, :, blk:] = m_hi

    def pipelined(slot_next, a_next, slot_cur, a_cur):
        for g in range(group):
            stage_scores(slot_next, g, scores(g, a_next))
            fold_head(slot_cur, g, a_cur)

    def two_pairs(t, carry):
        a0 = 2 * t
        pipelined(1, a0 + 1, 0, a0)
        pipelined(0, a0 + 2, 1, a0 + 1)
        return carry

    is_odd = jnp.bitwise_and(own, 1) == 1
    n_full = jnp.where(is_odd, own - 1, jnp.maximum(own - 2, 0))
    lax.fori_loop(0, lax.shift_right_logical(n_full, 1), two_pairs, 0)

    @pl.when(is_odd)
    def _last_pair():
        for g in range(group):
            fold_head(0, g, own - 1)

    @pl.when(jnp.logical_and(jnp.logical_not(is_odd), own >= 2))
    def _last_two_pairs():
        for g in range(group):
            stage_scores(1, g, scores(g, own - 1))
            fold_head(0, g, own - 2)
        for g in range(group):
            fold_head(1, g, own - 1)

    for g, cols in enumerate(head_cols):
        acc = acc_scr[g]
        o = acc[:HEAD_DIM] * (1.0 / acc[HEAD_DIM:HEAD_DIM + 1])
        o_ref[:, cols] = o.T.astype(o_ref.dtype)


def _attention(proj, e, *, moba, batch, seq, heads, q_col, k_col, v_col):
    group = ATTN_GROUP
    tq = KEY_PAIR
    nq = seq // tq
    n_blocks = seq // MOBA_BLOCK
    width = group * HEAD_DIM
    assert heads % group == 0 and seq % KEY_PAIR == 0 and n_blocks <= LANES
    assert q_col % group == 0 and k_col % group == 0 and v_col % group == 0
    in_specs = [
        pl.BlockSpec((tq, width), lambda b, h, i: (b * nq + i, q_col // group + h)),
        pl.BlockSpec((seq, width), lambda b, h, i: (b, k_col // group + h)),
        pl.BlockSpec((seq, width), lambda b, h, i: (b, v_col // group + h)),
    ]
    args = [proj, proj, proj]
    if not moba:
        in_specs.append(pl.BlockSpec((seq, LANES), lambda b, h, i: (b, 0)))
        args.append(e)
    scratch = [
        pltpu.VMEM((group, seq // KEY_PAIR, VT_ROWS, KEY_PAIR), BF16),
        pltpu.VMEM((group, seq, 2 * HEAD_DIM), BF16),
        pltpu.VMEM((group, tq, 2 * HEAD_DIM), BF16),
        pltpu.VMEM((2, group, KEY_PAIR, tq), F32),
        pltpu.VMEM((2, group, 1, tq), F32),
        pltpu.VMEM((group, 1, tq), F32),
        pltpu.VMEM((group, VT_ROWS, tq), F32),
    ]
    if moba:
        scratch.append(pltpu.VMEM((group, n_blocks, HEAD_DIM), F32))
    return pl.pallas_call(
        functools.partial(_attn_kernel, moba=moba, n_blocks=n_blocks, group=group,
                          n_heads=heads),
        grid=(batch, heads // group, nq),
        in_specs=in_specs,
        out_specs=pl.BlockSpec((tq, width), lambda b, h, i: (b * nq + i, h)),
        out_shape=jax.ShapeDtypeStruct((batch * seq, heads * HEAD_DIM), BF16),
        scratch_shapes=scratch,
        compiler_params=_compiler_params(3),
        name="moba_attention" if moba else "fox_attention",
    )(*args)


def _rmsnorm(x, g):
    ms = jnp.mean(x * x, axis=-1, keepdims=True)
    return x * lax.rsqrt(ms + RMS_EPS) * g


def _out_merge_kernel(oa_ref, ob_ref, za_ref, zb_ref, ga_ref, gb_ref, x_ref, p_ref,
                      wa_ref, wb_ref, wo_ref, wg_ref, wu_ref, gple_ref, gfin_ref, out_ref,
                      *, final):
    def branch(o_ref, z_ref, w_ref):
        z = z_ref[...].astype(F32)
        a = o_ref[...].astype(F32) * (z * jax.nn.sigmoid(z))
        return jnp.dot(a.astype(BF16), w_ref[...], preferred_element_type=F32)

    p_in = p_ref[...].astype(BF16)
    d_half = out_ref.shape[1] // 2
    ya = branch(oa_ref, za_ref, wa_ref)
    yb = branch(ob_ref, zb_ref, wb_ref)
    up_lo = jnp.dot(p_in, wu_ref[:, :d_half], preferred_element_type=F32)
    mixed = (jax.nn.sigmoid(ga_ref[...].astype(F32)) * ya
             + jax.nn.sigmoid(gb_ref[...].astype(F32)) * yb)
    x1 = x_ref[...] + jnp.dot(mixed.astype(BF16), wo_ref[...], preferred_element_type=F32)
    up_hi = jnp.dot(p_in, wu_ref[:, d_half:], preferred_element_type=F32)
    hn = _rmsnorm(x1, gple_ref[...]).astype(BF16)
    pg = jax.nn.sigmoid(jnp.dot(hn, wg_ref[...], preferred_element_type=F32))
    x2 = x1 + jnp.concatenate([up_lo, up_hi], axis=1) * pg
    if final:
        x2 = _rmsnorm(x2, gfin_ref[...])
    out_ref[...] = x2


def _out_merge(oa, ob, proj, x2, p2, wa, wb, wo, wg, wu, g_ple, g_final, *, tm, final,
               za_col, zb_col, ga_col, gb_col):
    t, d = x2.shape
    wa_w = oa.shape[1]
    wb_w = ob.shape[1]
    ple = p2.shape[1]
    row = lambda i: (i, 0)
    const = lambda i: (0, 0)
    return pl.pallas_call(
        functools.partial(_out_merge_kernel, final=final),
        grid=(t // tm,),
        in_specs=[
            pl.BlockSpec((tm, wa_w), row),
            pl.BlockSpec((tm, wb_w), row),
            pl.BlockSpec((tm, wa_w), lambda i: (i, za_col)),
            pl.BlockSpec((tm, wb_w), lambda i: (i, zb_col)),
            pl.BlockSpec((tm, d), lambda i: (i, ga_col)),
            pl.BlockSpec((tm, d), lambda i: (i, gb_col)),
            pl.BlockSpec((tm, d), row),
            pl.BlockSpec((tm, ple), row),
            _resident((wa_w, d), const),
            _resident((wb_w, d), const),
            _resident((d, d), const),
            _resident((d, d), const),
            _resident((ple, d), const),
            _resident((1, d), const),
            _resident((1, d), const),
        ],
        out_specs=pl.BlockSpec((tm, d), row),
        out_shape=jax.ShapeDtypeStruct((t, d), F32),
        compiler_params=_compiler_params(1),
        name="out_merge",
    )(oa, ob, proj, proj, proj, proj, x2, p2, wa, wb, wo, wg, wu, g_ple, g_final)


def kernel(x, p, positions, g_norm, w_in, b_f, w_branch_a, w_branch_b, w_out,
           g_ple, w_ple_gate, w_ple_up, g_final):
    batch, seq, d = x.shape
    depth = w_in.shape[0]
    t = batch * seq
    wa_w = w_branch_a.shape[1]
    wb_w = w_branch_b.shape[1]
    heads_a = wa_w // HEAD_DIM
    heads_b = wb_w // HEAD_DIM
    n_f = b_f.shape[1]
    assert seq % MOBA_BLOCK == 0 and wa_w == wb_w and 2 * wa_w == d
    f_start = 4 * wa_w + 3 * wb_w
    assert w_in.shape[2] == f_start + n_f + wb_w + 2 * d

    inv = ROPE_THETA ** (-jnp.arange(0, HEAD_DIM, 2, dtype=F32) / HEAD_DIM)
    inv_full = jnp.concatenate([inv, inv]).reshape(1, HEAD_DIM)
    pos = positions.reshape(t, 1)
    x2 = x.reshape(t, d)
    hb = HEAD_DIM

    for layer in range(depth):
        w_t = jnp.swapaxes(w_in[layer], 0, 1)
        w_main, w_f = _w_in_prep(w_t, f_start=f_start, n_f=n_f, tn=wa_w)
        b_pad = jnp.pad(b_f[layer], (0, LANES - n_f)).reshape(1, LANES)

        proj, f = _in_proj(x2, pos, g_norm[layer].reshape(1, d), inv_full, w_main, w_f,
                           tm=IN_PROJ_ROWS, tn=IN_PROJ_SECTIONS * wa_w, section=wa_w)
        e = _fox_gate(f, b_pad, batch=batch, seq=seq, n_heads=heads_b)
        oa = _attention(proj, None, moba=True, batch=batch, seq=seq, heads=heads_a,
                        q_col=0, k_col=wa_w // hb, v_col=2 * wa_w // hb)
        ob = _attention(proj, e, moba=False, batch=batch, seq=seq, heads=heads_b,
                        q_col=4 * wa_w // hb, k_col=(4 * wa_w + wb_w) // hb,
                        v_col=(4 * wa_w + 2 * wb_w) // hb)
        x2 = _out_merge(
            oa, ob, proj, x2, p[layer].reshape(t, -1),
            w_branch_a[layer].astype(BF16), w_branch_b[layer].astype(BF16),
            w_out[layer].astype(BF16), w_ple_gate[layer].astype(BF16),
            w_ple_up[layer].astype(BF16), g_ple[layer].reshape(1, d), g_final.reshape(1, d),
            tm=OUT_MERGE_ROWS, final=(layer == depth - 1),
            za_col=3 * wa_w // wa_w, zb_col=(4 * wa_w + 3 * wb_w) // wb_w,
            ga_col=(4 * wa_w + 4 * wb_w) // d, gb_col=(4 * wa_w + 4 * wb_w) // d + 1)
    return x2.reshape(batch, seq, d)
```

```python
import functools
import math

import jax
import jax.numpy as jnp
from jax import lax
from jax.experimental import pallas as pl
from jax.experimental.pallas import tpu as pltpu

HEAD_DIM = 128
MOBA_BLOCK = 256
MOBA_TOPK = 3
ROPE_THETA = 10000.0
RMS_EPS = 1e-6

V7X_VMEM_BYTES = 64 * 1024 * 1024
VMEM_LIMIT_BYTES = V7X_VMEM_BYTES * 7 // 8
LANES = 128
BF16_SUBLANES = 16

IN_PROJ_ROWS = 1024
IN_PROJ_ROW_PIECE = 512
IN_PROJ_SECTIONS = 2
OUT_MERGE_ROWS = 256

F32 = jnp.float32
BF16 = jnp.bfloat16
NEG_INF = float("-inf")
MASKED = -1e30
LOG2E = math.log2(math.e)
NT_DIMS = (((1,), (1,)), ((), ()))


def _compiler_params(n_grid_dims):
    return pltpu.CompilerParams(
        dimension_semantics=("arbitrary",) * n_grid_dims,
        vmem_limit_bytes=VMEM_LIMIT_BYTES,
    )


def _resident(block_shape, index_map):
    return pl.BlockSpec(block_shape, index_map, pipeline_mode=pl.Buffered(1))


F32_SUBLANES = 8
PREP_COLS = 512


def _w_in_prep_kernel(wt_hbm, out_ref, wf_ref, buf, fbuf, sem, fsem, *, n_lo, n_f):
    n = pl.program_id(0)
    tn = out_ref.shape[1]

    def fetch(j, slot):
        first_row = pl.multiple_of(j * tn + jnp.where(j >= n_lo, n_f, 0), F32_SUBLANES)
        return pltpu.make_async_copy(wt_hbm.at[pl.ds(first_row, tn), :], buf.at[slot],
                                     sem.at[slot])

    slot = jnp.bitwise_and(n, 1)

    @pl.when(n == 0)
    def _first():
        fetch(0, 0).start()
        gate_rows = pltpu.make_async_copy(wt_hbm.at[pl.ds(n_lo * tn, n_f), :], fbuf, fsem)
        gate_rows.start()
        gate_rows.wait()
        padded = jnp.concatenate(
            [fbuf[...], jnp.zeros((LANES - n_f, fbuf.shape[1]), F32)], axis=0)
        wf_ref[...] = padded.T.astype(BF16)

    @pl.when(n + 1 < pl.num_programs(0))
    def _ahead():
        fetch(n + 1, 1 - slot).start()

    fetch(n, slot).wait()
    for c0 in range(0, buf.shape[2], PREP_COLS):
        out_ref[c0:c0 + PREP_COLS, :] = buf[slot, :, c0:c0 + PREP_COLS].T.astype(BF16)


def _w_in_prep(w_t, *, f_start, n_f, tn):
    n_in, d = w_t.shape
    n_lo = f_start // tn
    n_tiles = (n_in - n_f) // tn
    assert f_start == n_lo * tn and n_in == n_tiles * tn + n_f
    assert n_f % F32_SUBLANES == 0 and n_f <= LANES and d % PREP_COLS == 0
    return pl.pallas_call(
        functools.partial(_w_in_prep_kernel, n_lo=n_lo, n_f=n_f),
        grid=(n_tiles,),
        in_specs=[pl.BlockSpec(memory_space=pl.ANY)],
        out_specs=[pl.BlockSpec((d, tn), lambda n: (0, n)),
                   pl.BlockSpec((d, LANES), lambda n: (0, 0))],
        out_shape=[jax.ShapeDtypeStruct((d, n_tiles * tn), BF16),
                   jax.ShapeDtypeStruct((d, LANES), BF16)],
        scratch_shapes=[pltpu.VMEM((2, tn, d), F32), pltpu.VMEM((n_f, d), F32),
                        pltpu.SemaphoreType.DMA((2,)), pltpu.SemaphoreType.DMA(())],
        compiler_params=_compiler_params(1),
        name="w_in_prep",
    )(w_t)


MXU_COLS = 256


def _in_proj_kernel(x_ref, pos_ref, g_ref, inv_ref, w_ref, wf_ref, proj_ref, f_ref,
                    h_scr, cos_scr, sin_scr, *, section, rope_sections, scaled_sections,
                    q_scale, row_tile):
    n = pl.program_id(1)
    half = HEAD_DIM // 2

    @pl.when(n == 0)
    def _normalize_and_tables():
        x = x_ref[...]
        ms = jnp.mean(x * x, axis=-1, keepdims=True)
        h = (x * lax.rsqrt(ms + RMS_EPS) * g_ref[...]).astype(BF16)
        h_scr[...] = h
        f_ref[...] = jnp.dot(h, wf_ref[...], preferred_element_type=F32)
        lane = lax.broadcasted_iota(jnp.int32, (x.shape[0], HEAD_DIM), 1)
        first = lane < half
        ang = pos_ref[...].astype(F32) * inv_ref[...]
        cs = jnp.cos(jnp.where(first, ang, ang - 0.5 * math.pi))
        sc = pltpu.roll(cs, half, 1)
        cos_scr[...] = jnp.where(first, cs, sc)
        sin_scr[...] = jnp.where(first, -sc, cs)

    tn = proj_ref.shape[1]
    coef_c, coef_s = [], []
    for k in range(tn // section):
        sec = n * (tn // section) + k
        is_rope = functools.reduce(jnp.logical_or, [sec == s for s in rope_sections])
        scaled = functools.reduce(jnp.logical_or, [sec == s for s in scaled_sections])
        scale = jnp.where(scaled, q_scale, 1.0).astype(F32)
        coef_c.append(jnp.where(is_rope, cos_scr[...], 1.0) * scale)
        coef_s.append(jnp.where(is_rope, sin_scr[...], 0.0) * scale)
    tm = h_scr.shape[0]
    for r0 in range(0, tm, row_tile):
        rows = slice(r0, r0 + row_tile)
        h = h_scr[rows, :]
        for c0 in range(0, tn, MXU_COLS):
            acc = jnp.dot(h, w_ref[:, c0:c0 + MXU_COLS],
                          preferred_element_type=F32)
            cc, cs = coef_c[c0 // section][rows], coef_s[c0 // section][rows]
            for j in range(0, MXU_COLS, HEAD_DIM):
                a = acc[:, j:j + HEAD_DIM]
                r = a * cc + pltpu.roll(a, half, 1) * cs
                proj_ref[rows, c0 + j:c0 + j + HEAD_DIM] = r.astype(BF16)


def _in_proj(x2, pos, g, inv_full, w_main, w_f, *, tm, tn, section):
    t, d = x2.shape
    n_main = w_main.shape[1]
    assert tn % section == 0 and n_main % tn == 0 and section % MXU_COLS == 0
    kern = functools.partial(
        _in_proj_kernel, section=section, rope_sections=(0, 1), scaled_sections=(0, 4),
        q_scale=LOG2E / math.sqrt(HEAD_DIM), row_tile=min(IN_PROJ_ROW_PIECE, tm))
    return pl.pallas_call(
        kern,
        grid=(t // tm, n_main // tn),
        in_specs=[
            pl.BlockSpec((tm, d), lambda i, n: (i, 0)),
            pl.BlockSpec((tm, 1), lambda i, n: (i, 0)),
            _resident((1, d), lambda i, n: (0, 0)),
            _resident((1, LANES), lambda i, n: (0, 0)),
            pl.BlockSpec((d, tn), lambda i, n: (0, n)),
            _resident((d, LANES), lambda i, n: (0, 0)),
        ],
        out_specs=[
            pl.BlockSpec((tm, tn), lambda i, n: (i, n)),
            pl.BlockSpec((tm, LANES), lambda i, n: (i, 0)),
        ],
        out_shape=[
            jax.ShapeDtypeStruct((t, n_main), BF16),
            jax.ShapeDtypeStruct((t, LANES), F32),
        ],
        scratch_shapes=[
            pltpu.VMEM((tm, d), BF16),
            pltpu.VMEM((tm, LANES), F32),
            pltpu.VMEM((tm, LANES), F32),
        ],
        compiler_params=_compiler_params(2),
        name="in_proj",
    )(x2, pos, g, inv_full, w_main, w_f)


FOX_TERMS = 3


def _fox_gate_kernel(f_ref, b_ref, e_ref, *, chunk, n_heads):
    n_chunks = f_ref.shape[0] // chunk
    r = lax.broadcasted_iota(jnp.int32, (chunk, chunk), 0)
    c = lax.broadcasted_iota(jnp.int32, (chunk, chunk), 1)
    tri = (c <= r).astype(BF16)
    z = f_ref[...] + b_ref[...]
    logf = (jnp.minimum(z, 0.0) - jnp.log1p(jnp.exp(-jnp.abs(z)))) * LOG2E
    wide = jnp.concatenate([logf[j * chunk:(j + 1) * chunk] for j in range(n_chunks)], axis=1)
    local = jnp.zeros_like(wide)
    for _ in range(FOX_TERMS):
        term = wide.astype(BF16)
        local = local + jnp.dot(tri, term, preferred_element_type=F32)
        wide = wide - term.astype(F32)
    lane = lax.broadcasted_iota(jnp.int32, (chunk, LANES), 1)
    offset = jnp.zeros((1, LANES), F32)
    for j in range(n_chunks):
        csum = local[:, j * LANES:(j + 1) * LANES] + offset
        offset = csum[chunk - 1:chunk, :]
        hi = csum.astype(BF16).astype(F32)
        mid = (csum - hi).astype(BF16).astype(F32)
        lo = csum - hi - mid
        feats = jnp.where(
            lane < n_heads, hi,
            jnp.where(lane < 2 * n_heads, pltpu.roll(mid, n_heads, 1),
                      jnp.where(lane < 3 * n_heads, pltpu.roll(lo, 2 * n_heads, 1), 0.0)))
        e_ref[j * chunk:(j + 1) * chunk, :] = feats.astype(BF16)


def _fox_gate(f, b_pad, *, batch, seq, n_heads):
    assert FOX_TERMS * n_heads <= LANES
    return pl.pallas_call(
        functools.partial(_fox_gate_kernel, chunk=MOBA_BLOCK, n_heads=n_heads),
        grid=(batch,),
        in_specs=[
            pl.BlockSpec((seq, LANES), lambda b: (b, 0)),
            _resident((1, LANES), lambda b: (0, 0)),
        ],
        out_specs=pl.BlockSpec((seq, LANES), lambda b: (b, 0)),
        out_shape=jax.ShapeDtypeStruct(f.shape, BF16),
        compiler_params=_compiler_params(1),
        name="fox_gate",
    )(f, b_pad)


ATTN_GROUP = 4
VT_ROWS = HEAD_DIM + BF16_SUBLANES
KEY_PAIR = 2 * MOBA_BLOCK
ATTN_TILES_PER_STEP = 2


def _attn_kernel(*refs, moba, n_blocks, group, n_heads, tiles_per_step):
    if moba:
        (q_ref, k_ref, v_ref, o_ref,
         vt_scr, kaug_scr, qaug_scr, s_scr, smax_scr, m_scr, acc_scr, kbar_scr) = refs
    else:
        (q_ref, k_ref, v_ref, e_ref, o_ref,
         vt_scr, kaug_scr, qaug_scr, s_scr, smax_scr, m_scr, acc_scr) = refs
    blk = MOBA_BLOCK
    pair = KEY_PAIR
    hg = pl.program_id(1)
    step = pl.program_id(2)
    head_cols = [slice(g * HEAD_DIM, (g + 1) * HEAD_DIM) for g in range(group)]
    extra = slice(HEAD_DIM, 2 * HEAD_DIM)

    @pl.when(step == 0)
    def _per_group_setup():
        ones_row = (lax.broadcasted_iota(jnp.int32, (BF16_SUBLANES, pair), 0) == 0).astype(BF16)
        lane = lax.broadcasted_iota(jnp.int32, (pair, LANES), 1)
        row = lax.broadcasted_iota(jnp.int32, (pair, LANES), 0)

        def body(a, carry):
            off = pl.multiple_of(a * pair, pair)
            if moba:
                block_of_row = 2 * a + (row >= blk).astype(jnp.int32)
                feats = jnp.where(lane == block_of_row, 1.0, 0.0).astype(BF16)
            else:
                feats = e_ref[pl.ds(off, pair), :]
            for g, cols in enumerate(head_cols):
                vt_scr[g, a, :HEAD_DIM, :] = v_ref[pl.ds(off, pair), cols].T
                vt_scr[g, a, HEAD_DIM:, :] = ones_row
                kb = k_ref[pl.ds(off, pair), cols]
                kaug_scr[g, pl.ds(off, pair), :HEAD_DIM] = kb
                kaug_scr[g, pl.ds(off, pair), extra] = feats
                if moba:
                    kf = kb.astype(F32)
                    kbar_scr[g, pl.ds(2 * a, 1), :] = jnp.mean(kf[:blk], axis=0, keepdims=True)
                    kbar_scr[g, pl.ds(2 * a + 1, 1), :] = jnp.mean(kf[blk:], axis=0,
                                                                   keepdims=True)
            return carry
        lax.fori_loop(0, n_blocks // 2, body, 0)

    def process_tile(own, rows):
        def widen_queries(g):
            q = q_ref[rows, head_cols[g]]
            qaug_scr[g, :, :HEAD_DIM] = q
            if moba:
                kbar = kbar_scr[g]
                hi = kbar.astype(BF16)
                mid = (kbar - hi.astype(F32)).astype(BF16)
                lo = (kbar - hi.astype(F32) - mid.astype(F32)).astype(BF16)
                gs3 = lax.dot_general(jnp.concatenate([hi, mid, lo], axis=0), q, NT_DIMS,
                                      preferred_element_type=F32)
                gs = gs3[:n_blocks] + gs3[n_blocks:2 * n_blocks] + gs3[2 * n_blocks:]
                blk_id = lax.broadcasted_iota(jnp.int32, gs.shape, 0)
                q_blk = 2 * own + (lax.broadcasted_iota(jnp.int32, gs.shape, 1) >= blk).astype(
                    jnp.int32)
                gs = jnp.where(blk_id < q_blk, gs, NEG_INF)
                bias = jnp.where(blk_id == q_blk, 0.0, MASKED)
                for _ in range(MOBA_TOPK):
                    mx = jnp.max(gs, axis=0, keepdims=True)
                    first = jnp.min(jnp.where(gs == mx, blk_id, n_blocks), axis=0,
                                    keepdims=True)
                    pick = jnp.logical_and(blk_id == first, mx > NEG_INF)
                    bias = jnp.where(pick, 0.0, bias)
                    gs = jnp.where(pick, NEG_INF, gs)
                bias = jnp.concatenate([bias, jnp.zeros((LANES - n_blocks, pair), F32)], axis=0)
                qaug_scr[g, :, extra] = bias.T.astype(BF16)
            else:
                lane = lax.broadcasted_iota(jnp.int32, (pair, LANES), 1)
                head = hg * group + g
                mine = functools.reduce(
                    jnp.logical_or, [lane == head + j * n_heads for j in range(FOX_TERMS)])
                qaug_scr[g, :, extra] = jnp.where(mine, -1.0, 0.0).astype(BF16)

        def scores(g, a):
            off = pl.multiple_of(a * pair, pair)
            return lax.dot_general(kaug_scr[g, pl.ds(off, pair), :], qaug_scr[g], NT_DIMS,
                                   preferred_element_type=F32)

        def stage_scores(slot, g, s):
            s_scr[slot, g] = s
            smax_scr[slot, g] = jnp.max(s, axis=0, keepdims=True)

        def fold_head(slot, g, a):
            m_old = m_scr[g]
            m_new = jnp.maximum(m_old, smax_scr[slot, g])
            p = jnp.exp2(s_scr[slot, g] - m_new).astype(BF16)
            pv = jnp.dot(vt_scr[g, a], p, preferred_element_type=F32)
            acc_scr[g] = jnp.exp2(m_old - m_new) * acc_scr[g] + pv
            m_scr[g] = m_new

        own_off = pl.multiple_of(own * pair, pair)
        visible_lo = (lax.broadcasted_iota(jnp.int32, (blk, blk), 0)
                      <= lax.broadcasted_iota(jnp.int32, (blk, blk), 1))
        visible_hi = (lax.broadcasted_iota(jnp.int32, (pair, blk), 0)
                      <= lax.broadcasted_iota(jnp.int32, (pair, blk), 1) + blk)
        for g in range(group):
            widen_queries(g)
        for g in range(group):
            s_lo = lax.dot_general(kaug_scr[g, pl.ds(own_off, blk), :], qaug_scr[g, :blk, :],
                                   NT_DIMS, preferred_element_type=F32)
            s_hi = lax.dot_general(kaug_scr[g, pl.ds(own_off, pair), :], qaug_scr[g, blk:, :],
                                   NT_DIMS, preferred_element_type=F32)
            s_scr[1, g, :blk, :blk] = jnp.where(visible_lo, s_lo, NEG_INF)
            s_scr[1, g, :, blk:] = jnp.where(visible_hi, s_hi, NEG_INF)
        for g in range(group):
            stage_scores(0, g, scores(g, 0))
            s_lo = s_scr[1, g, :blk, :blk]
            s_hi = s_scr[1, g, :, blk:]
            m_lo = jnp.max(s_lo, axis=0, keepdims=True)
            m_hi = jnp.max(s_hi, axis=0, keepdims=True)
            p_lo = jnp.exp2(s_lo - m_lo).astype(BF16)
            p_hi = jnp.exp2(s_hi - m_hi).astype(BF16)
            acc_scr[g, :, :blk] = jnp.dot(vt_scr[g, own, :, :blk], p_lo,
                                          preferred_element_type=F32)
            acc_scr[g, :, blk:] = jnp.dot(vt_scr[g, own], p_hi, preferred_element_type=F32)
            m_scr[g, :, :blk] = m_lo
            m_scr[g, :, blk:] = m_hi

        def pipelined(slot_next, a_next, slot_cur, a_cur):
            for g in range(group):
                stage_scores(slot_next, g, scores(g, a_next))
                fold_head(slot_cur, g, a_cur)

        def two_pairs(t, carry):
            a0 = 2 * t
            pipelined(1, a0 + 1, 0, a0)
            pipelined(0, a0 + 2, 1, a0 + 1)
            return carry

        is_odd = jnp.bitwise_and(own, 1) == 1
        n_full = jnp.where(is_odd, own - 1, jnp.maximum(own - 2, 0))
        lax.fori_loop(0, lax.shift_right_logical(n_full, 1), two_pairs, 0)

        @pl.when(is_odd)
        def _last_pair():
            for g in range(group):
                fold_head(0, g, own - 1)

        @pl.when(jnp.logical_and(jnp.logical_not(is_odd), own >= 2))
        def _last_two_pairs():
            for g in range(group):
                stage_scores(1, g, scores(g, own - 1))
                fold_head(0, g, own - 2)
            for g in range(group):
                fold_head(1, g, own - 1)

        for g, cols in enumerate(head_cols):
            acc = acc_scr[g]
            o = acc[:HEAD_DIM] * (1.0 / acc[HEAD_DIM:HEAD_DIM + 1])
            o_ref[rows, cols] = o.T.astype(o_ref.dtype)

    for t in range(tiles_per_step):
        process_tile(step * tiles_per_step + t, slice(t * pair, (t + 1) * pair))


def _attention(proj, e, *, moba, batch, seq, heads, q_col, k_col, v_col):
    group = ATTN_GROUP
    tiles_per_step = ATTN_TILES_PER_STEP
    tq = KEY_PAIR
    rows_per_step = tiles_per_step * tq
    nq = seq // rows_per_step
    n_blocks = seq // MOBA_BLOCK
    width = group * HEAD_DIM
    assert heads % group == 0 and seq % rows_per_step == 0 and n_blocks <= LANES
    assert q_col % group == 0 and k_col % group == 0 and v_col % group == 0
    in_specs = [
        pl.BlockSpec((rows_per_step, width), lambda b, h, i: (b * nq + i, q_col // group + h)),
        pl.BlockSpec((seq, width), lambda b, h, i: (b, k_col // group + h)),
        pl.BlockSpec((seq, width), lambda b, h, i: (b, v_col // group + h)),
    ]
    args = [proj, proj, proj]
    if not moba:
        in_specs.append(pl.BlockSpec((seq, LANES), lambda b, h, i: (b, 0)))
        args.append(e)
    scratch = [
        pltpu.VMEM((group, seq // KEY_PAIR, VT_ROWS, KEY_PAIR), BF16),
        pltpu.VMEM((group, seq, 2 * HEAD_DIM), BF16),
        pltpu.VMEM((group, tq, 2 * HEAD_DIM), BF16),
        pltpu.VMEM((2, group, KEY_PAIR, tq), F32),
        pltpu.VMEM((2, group, 1, tq), F32),
        pltpu.VMEM((group, 1, tq), F32),
        pltpu.VMEM((group, VT_ROWS, tq), F32),
    ]
    if moba:
        scratch.append(pltpu.VMEM((group, n_blocks, HEAD_DIM), F32))
    return pl.pallas_call(
        functools.partial(_attn_kernel, moba=moba, n_blocks=n_blocks, group=group,
                          n_heads=heads, tiles_per_step=tiles_per_step),
        grid=(batch, heads // group, nq),
        in_specs=in_specs,
        out_specs=pl.BlockSpec((rows_per_step, width), lambda b, h, i: (b * nq + i, h)),
        out_shape=jax.ShapeDtypeStruct((batch * seq, heads * HEAD_DIM), BF16),
        scratch_shapes=scratch,
        compiler_params=_compiler_params(3),
        name="moba_attention" if moba else "fox_attention",
    )(*args)


def _rmsnorm(x, g):
    ms = jnp.mean(x * x, axis=-1, keepdims=True)
    return x * lax.rsqrt(ms + RMS_EPS) * g


def _out_merge_kernel(oa_ref, ob_ref, za_ref, zb_ref, ga_ref, gb_ref, x_ref, p_ref,
                      wa_ref, wb_ref, wo_ref, wg_ref, wu_ref, gple_ref, gfin_ref, out_ref,
                      *, final):
    def branch(o_ref, z_ref, w_ref):
        z = z_ref[...].astype(F32)
        a = o_ref[...].astype(F32) * (z * jax.nn.sigmoid(z))
        return jnp.dot(a.astype(BF16), w_ref[...], preferred_element_type=F32)

    p_in = p_ref[...].astype(BF16)
    d_half = out_ref.shape[1] // 2
    ya = branch(oa_ref, za_ref, wa_ref)
    yb = branch(ob_ref, zb_ref, wb_ref)
    up_lo = jnp.dot(p_in, wu_ref[:, :d_half], preferred_element_type=F32)
    mixed = (jax.nn.sigmoid(ga_ref[...].astype(F32)) * ya
             + jax.nn.sigmoid(gb_ref[...].astype(F32)) * yb)
    x1 = x_ref[...] + jnp.dot(mixed.astype(BF16), wo_ref[...], preferred_element_type=F32)
    up_hi = jnp.dot(p_in, wu_ref[:, d_half:], preferred_element_type=F32)
    hn = _rmsnorm(x1, gple_ref[...]).astype(BF16)
    pg = jax.nn.sigmoid(jnp.dot(hn, wg_ref[...], preferred_element_type=F32))
    x2 = x1 + jnp.concatenate([up_lo, up_hi], axis=1) * pg
    if final:
        x2 = _rmsnorm(x2, gfin_ref[...])
    out_ref[...] = x2


def _out_merge(oa, ob, proj, x2, p2, wa, wb, wo, wg, wu, g_ple, g_final, *, tm, final,
               za_col, zb_col, ga_col, gb_col):
    t, d = x2.shape
    wa_w = oa.shape[1]
    wb_w = ob.shape[1]
    ple = p2.shape[1]
    row = lambda i: (i, 0)
    const = lambda i: (0, 0)
    return pl.pallas_call(
        functools.partial(_out_merge_kernel, final=final),
        grid=(t // tm,),
        in_specs=[
            pl.BlockSpec((tm, wa_w), row),
            pl.BlockSpec((tm, wb_w), row),
            pl.BlockSpec((tm, wa_w), lambda i: (i, za_col)),
            pl.BlockSpec((tm, wb_w), lambda i: (i, zb_col)),
            pl.BlockSpec((tm, d), lambda i: (i, ga_col)),
            pl.BlockSpec((tm, d), lambda i: (i, gb_col)),
            pl.BlockSpec((tm, d), row),
            pl.BlockSpec((tm, ple), row),
            _resident((wa_w, d), const),
            _resident((wb_w, d), const),
            _resident((d, d), const),
            _resident((d, d), const),
            _resident((ple, d), const),
            _resident((1, d), const),
            _resident((1, d), const),
        ],
        out_specs=pl.BlockSpec((tm, d), row),
        out_shape=jax.ShapeDtypeStruct((t, d), F32),
        compiler_params=_compiler_params(1),
        name="out_merge",
    )(oa, ob, proj, proj, proj, proj, x2, p2, wa, wb, wo, wg, wu, g_ple, g_final)


def kernel(x, p, positions, g_norm, w_in, b_f, w_branch_a, w_branch_b, w_out,
           g_ple, w_ple_gate, w_ple_up, g_final):
    batch, seq, d = x.shape
    depth = w_in.shape[0]
    t = batch * seq
    wa_w = w_branch_a.shape[1]
    wb_w = w_branch_b.shape[1]
    heads_a = wa_w // HEAD_DIM
    heads_b = wb_w // HEAD_DIM
    n_f = b_f.shape[1]
    assert seq % MOBA_BLOCK == 0 and wa_w == wb_w and 2 * wa_w == d
    f_start = 4 * wa_w + 3 * wb_w
    assert w_in.shape[2] == f_start + n_f + wb_w + 2 * d

    inv = ROPE_THETA ** (-jnp.arange(0, HEAD_DIM, 2, dtype=F32) / HEAD_DIM)
    inv_full = jnp.concatenate([inv, inv]).reshape(1, HEAD_DIM)
    pos = positions.reshape(t, 1)
    x2 = x.reshape(t, d)
    hb = HEAD_DIM

    for layer in range(depth):
        w_t = jnp.swapaxes(w_in[layer], 0, 1)
        w_main, w_f = _w_in_prep(w_t, f_start=f_start, n_f=n_f, tn=wa_w)
        b_pad = jnp.pad(b_f[layer], (0, LANES - n_f)).reshape(1, LANES)

        proj, f = _in_proj(x2, pos, g_norm[layer].reshape(1, d), inv_full, w_main, w_f,
                           tm=IN_PROJ_ROWS, tn=IN_PROJ_SECTIONS * wa_w, section=wa_w)
        e = _fox_gate(f, b_pad, batch=batch, seq=seq, n_heads=heads_b)
        oa = _attention(proj, None, moba=True, batch=batch, seq=seq, heads=heads_a,
                        q_col=0, k_col=wa_w // hb, v_col=2 * wa_w // hb)
        ob = _attention(proj, e, moba=False, batch=batch, seq=seq, heads=heads_b,
                        q_col=4 * wa_w // hb, k_col=(4 * wa_w + wb_w) // hb,
                        v_col=(4 * wa_w + 2 * wb_w) // hb)
        x2 = _out_merge(
            oa, ob, proj, x2, p[layer].reshape(t, -1),
            w_branch_a[layer].astype(BF16), w_branch_b[layer].astype(BF16),
            w_out[layer].astype(BF16), w_ple_gate[layer].astype(BF16),
            w_ple_up[layer].astype(BF16), g_ple[layer].reshape(1, d), g_final.reshape(1, d),
            tm=OUT_MERGE_ROWS, final=(layer == depth - 1),
            za_col=3 * wa_w // wa_w, zb_col=(4 * wa_w + 3 * wb_w) // wb_w,
            ga_col=(4 * wa_w + 4 * wb_w) // d, gb_col=(4 * wa_w + 4 * wb_w) // d + 1)
    return x2.reshape(batch, seq, d)
```

```python
import functools
import math

import jax
import jax.numpy as jnp
from jax import lax
from jax.experimental import pallas as pl
from jax.experimental.pallas import tpu as pltpu

HEAD_DIM = 128
MOBA_BLOCK = 256
MOBA_TOPK = 3
ROPE_THETA = 10000.0
RMS_EPS = 1e-6

V7X_VMEM_BYTES = 64 * 1024 * 1024
VMEM_LIMIT_BYTES = V7X_VMEM_BYTES * 7 // 8
LANES = 128
BF16_SUBLANES = 16

IN_PROJ_ROWS = 1024
IN_PROJ_ROW_PIECE = 512
IN_PROJ_SECTIONS = 2
OUT_MERGE_ROWS = 256

F32 = jnp.float32
BF16 = jnp.bfloat16
NEG_INF = float("-inf")
MASKED = -1e30
LOG2E = math.log2(math.e)
NT_DIMS = (((1,), (1,)), ((), ()))


def _compiler_params(n_grid_dims):
    return pltpu.CompilerParams(
        dimension_semantics=("arbitrary",) * n_grid_dims,
        vmem_limit_bytes=VMEM_LIMIT_BYTES,
    )


def _resident(block_shape, index_map):
    return pl.BlockSpec(block_shape, index_map, pipeline_mode=pl.Buffered(1))


F32_SUBLANES = 8
PREP_COLS = 512


def _w_in_prep_kernel(wt_hbm, out_ref, wf_ref, buf, fbuf, sem, fsem, *, n_lo, n_f):
    n = pl.program_id(0)
    tn = out_ref.shape[1]

    def fetch(j, slot):
        first_row = pl.multiple_of(j * tn + jnp.where(j >= n_lo, n_f, 0), F32_SUBLANES)
        return pltpu.make_async_copy(wt_hbm.at[pl.ds(first_row, tn), :], buf.at[slot],
                                     sem.at[slot])

    slot = jnp.bitwise_and(n, 1)

    @pl.when(n == 0)
    def _first():
        fetch(0, 0).start()
        gate_rows = pltpu.make_async_copy(wt_hbm.at[pl.ds(n_lo * tn, n_f), :], fbuf, fsem)
        gate_rows.start()
        gate_rows.wait()
        padded = jnp.concatenate(
            [fbuf[...], jnp.zeros((LANES - n_f, fbuf.shape[1]), F32)], axis=0)
        wf_ref[...] = padded.T.astype(BF16)

    @pl.when(n + 1 < pl.num_programs(0))
    def _ahead():
        fetch(n + 1, 1 - slot).start()

    fetch(n, slot).wait()
    for c0 in range(0, buf.shape[2], PREP_COLS):
        out_ref[c0:c0 + PREP_COLS, :] = buf[slot, :, c0:c0 + PREP_COLS].T.astype(BF16)


def _w_in_prep(w_t, *, f_start, n_f, tn):
    n_in, d = w_t.shape
    n_lo = f_start // tn
    n_tiles = (n_in - n_f) // tn
    assert f_start == n_lo * tn and n_in == n_tiles * tn + n_f
    assert n_f % F32_SUBLANES == 0 and n_f <= LANES and d % PREP_COLS == 0
    return pl.pallas_call(
        functools.partial(_w_in_prep_kernel, n_lo=n_lo, n_f=n_f),
        grid=(n_tiles,),
        in_specs=[pl.BlockSpec(memory_space=pl.ANY)],
        out_specs=[pl.BlockSpec((d, tn), lambda n: (0, n)),
                   pl.BlockSpec((d, LANES), lambda n: (0, 0))],
        out_shape=[jax.ShapeDtypeStruct((d, n_tiles * tn), BF16),
                   jax.ShapeDtypeStruct((d, LANES), BF16)],
        scratch_shapes=[pltpu.VMEM((2, tn, d), F32), pltpu.VMEM((n_f, d), F32),
                        pltpu.SemaphoreType.DMA((2,)), pltpu.SemaphoreType.DMA(())],
        compiler_params=_compiler_params(1),
        name="w_in_prep",
    )(w_t)


MXU_COLS = 256


def _in_proj_kernel(x_ref, pos_ref, g_ref, inv_ref, w_ref, wf_ref, proj_ref, f_ref,
                    h_scr, cos_scr, sin_scr, *, section, rope_sections, scaled_sections,
                    q_scale, row_tile):
    n = pl.program_id(1)
    half = HEAD_DIM // 2

    @pl.when(n == 0)
    def _normalize_and_tables():
        x = x_ref[...]
        ms = jnp.mean(x * x, axis=-1, keepdims=True)
        h = (x * lax.rsqrt(ms + RMS_EPS) * g_ref[...]).astype(BF16)
        h_scr[...] = h
        f_ref[...] = jnp.dot(h, wf_ref[...], preferred_element_type=F32)
        lane = lax.broadcasted_iota(jnp.int32, (x.shape[0], HEAD_DIM), 1)
        first = lane < half
        ang = pos_ref[...].astype(F32) * inv_ref[...]
        cs = jnp.cos(jnp.where(first, ang, ang - 0.5 * math.pi))
        sc = pltpu.roll(cs, half, 1)
        cos_scr[...] = jnp.where(first, cs, sc)
        sin_scr[...] = jnp.where(first, -sc, cs)

    tn = proj_ref.shape[1]
    coef_c, coef_s = [], []
    for k in range(tn // section):
        sec = n * (tn // section) + k
        is_rope = functools.reduce(jnp.logical_or, [sec == s for s in rope_sections])
        scaled = functools.reduce(jnp.logical_or, [sec == s for s in scaled_sections])
        scale = jnp.where(scaled, q_scale, 1.0).astype(F32)
        coef_c.append(jnp.where(is_rope, cos_scr[...], 1.0) * scale)
        coef_s.append(jnp.where(is_rope, sin_scr[...], 0.0) * scale)
    tm = h_scr.shape[0]
    for r0 in range(0, tm, row_tile):
        rows = slice(r0, r0 + row_tile)
        h = h_scr[rows, :]
        for c0 in range(0, tn, MXU_COLS):
            acc = jnp.dot(h, w_ref[:, c0:c0 + MXU_COLS],
                          preferred_element_type=F32)
            cc, cs = coef_c[c0 // section][rows], coef_s[c0 // section][rows]
            for j in range(0, MXU_COLS, HEAD_DIM):
                a = acc[:, j:j + HEAD_DIM]
                r = a * cc + pltpu.roll(a, half, 1) * cs
                proj_ref[rows, c0 + j:c0 + j + HEAD_DIM] = r.astype(BF16)


def _in_proj(x2, pos, g, inv_full, w_main, w_f, *, tm, tn, section):
    t, d = x2.shape
    n_main = w_main.shape[1]
    assert tn % section == 0 and n_main % tn == 0 and section % MXU_COLS == 0
    kern = functools.partial(
        _in_proj_kernel, section=section, rope_sections=(0, 1), scaled_sections=(0, 4),
        q_scale=LOG2E / math.sqrt(HEAD_DIM), row_tile=min(IN_PROJ_ROW_PIECE, tm))
    return pl.pallas_call(
        kern,
        grid=(t // tm, n_main // tn),
        in_specs=[
            pl.BlockSpec((tm, d), lambda i, n: (i, 0)),
            pl.BlockSpec((tm, 1), lambda i, n: (i, 0)),
            _resident((1, d), lambda i, n: (0, 0)),
            _resident((1, LANES), lambda i, n: (0, 0)),
            pl.BlockSpec((d, tn), lambda i, n: (0, n)),
            _resident((d, LANES), lambda i, n: (0, 0)),
        ],
        out_specs=[
            pl.BlockSpec((tm, tn), lambda i, n: (i, n)),
            pl.BlockSpec((tm, LANES), lambda i, n: (i, 0)),
        ],
        out_shape=[
            jax.ShapeDtypeStruct((t, n_main), BF16),
            jax.ShapeDtypeStruct((t, LANES), F32),
        ],
        scratch_shapes=[
            pltpu.VMEM((tm, d), BF16),
            pltpu.VMEM((tm, LANES), F32),
            pltpu.VMEM((tm, LANES), F32),
        ],
        compiler_params=_compiler_params(2),
        name="in_proj",
    )(x2, pos, g, inv_full, w_main, w_f)


FOX_TERMS = 3


def _fox_gate_kernel(f_ref, b_ref, e_ref, *, chunk, n_heads):
    n_chunks = f_ref.shape[0] // chunk
    r = lax.broadcasted_iota(jnp.int32, (chunk, chunk), 0)
    c = lax.broadcasted_iota(jnp.int32, (chunk, chunk), 1)
    tri = (c <= r).astype(BF16)
    z = f_ref[...] + b_ref[...]
    logf = (jnp.minimum(z, 0.0) - jnp.log1p(jnp.exp(-jnp.abs(z)))) * LOG2E
    wide = jnp.concatenate([logf[j * chunk:(j + 1) * chunk] for j in range(n_chunks)], axis=1)
    local = jnp.zeros_like(wide)
    for _ in range(FOX_TERMS):
        term = wide.astype(BF16)
        local = local + jnp.dot(tri, term, preferred_element_type=F32)
        wide = wide - term.astype(F32)
    lane = lax.broadcasted_iota(jnp.int32, (chunk, LANES), 1)
    offset = jnp.zeros((1, LANES), F32)
    for j in range(n_chunks):
        csum = local[:, j * LANES:(j + 1) * LANES] + offset
        offset = csum[chunk - 1:chunk, :]
        hi = csum.astype(BF16).astype(F32)
        mid = (csum - hi).astype(BF16).astype(F32)
        lo = csum - hi - mid
        feats = jnp.where(
            lane < n_heads, hi,
            jnp.where(lane < 2 * n_heads, pltpu.roll(mid, n_heads, 1),
                      jnp.where(lane < 3 * n_heads, pltpu.roll(lo, 2 * n_heads, 1), 0.0)))
        e_ref[j * chunk:(j + 1) * chunk, :] = feats.astype(BF16)


def _fox_gate(f, b_pad, *, batch, seq, n_heads):
    assert FOX_TERMS * n_heads <= LANES
    return pl.pallas_call(
        functools.partial(_fox_gate_kernel, chunk=MOBA_BLOCK, n_heads=n_heads),
        grid=(batch,),
        in_specs=[
            pl.BlockSpec((seq, LANES), lambda b: (b, 0)),
            _resident((1, LANES), lambda b: (0, 0)),
        ],
        out_specs=pl.BlockSpec((seq, LANES), lambda b: (b, 0)),
        out_shape=jax.ShapeDtypeStruct(f.shape, BF16),
        compiler_params=_compiler_params(1),
        name="fox_gate",
    )(f, b_pad)


ATTN_GROUP = 4
VT_ROWS = HEAD_DIM + BF16_SUBLANES
KEY_PAIR = 2 * MOBA_BLOCK
ATTN_TILES_PER_STEP = 2


def _attn_kernel(*refs, moba, n_blocks, group, n_heads, tiles_per_step):
    if moba:
        (q_ref, k_ref, v_ref, o_ref,
         vt_scr, kaug_scr, qaug_scr, s_scr, smax_scr, m_scr, acc_scr, kbar_scr) = refs
    else:
        (q_ref, k_ref, v_ref, e_ref, o_ref,
         vt_scr, kaug_scr, qaug_scr, s_scr, smax_scr, m_scr, acc_scr) = refs
    blk = MOBA_BLOCK
    pair = KEY_PAIR
    hg = pl.program_id(1)
    step = pl.program_id(2)
    head_cols = [slice(g * HEAD_DIM, (g + 1) * HEAD_DIM) for g in range(group)]
    extra = slice(HEAD_DIM, 2 * HEAD_DIM)

    @pl.when(step == 0)
    def _per_group_setup():
        ones_row = (lax.broadcasted_iota(jnp.int32, (BF16_SUBLANES, pair), 0) == 0).astype(BF16)
        lane = lax.broadcasted_iota(jnp.int32, (pair, LANES), 1)
        row = lax.broadcasted_iota(jnp.int32, (pair, LANES), 0)

        def body(a, carry):
            off = pl.multiple_of(a * pair, pair)
            if moba:
                block_of_row = 2 * a + (row >= blk).astype(jnp.int32)
                feats = jnp.where(lane == block_of_row, 1.0, 0.0).astype(BF16)
            else:
                feats = e_ref[pl.ds(off, pair), :]
            for g, cols in enumerate(head_cols):
                vt_scr[g, a, :HEAD_DIM, :] = v_ref[pl.ds(off, pair), cols].T
                vt_scr[g, a, HEAD_DIM:, :] = ones_row
                kb = k_ref[pl.ds(off, pair), cols]
                kaug_scr[g, pl.ds(off, pair), :HEAD_DIM] = kb
                kaug_scr[g, pl.ds(off, pair), extra] = feats
                if moba:
                    kf = kb.astype(F32)
                    kbar_scr[g, pl.ds(2 * a, 1), :] = jnp.mean(kf[:blk], axis=0, keepdims=True)
                    kbar_scr[g, pl.ds(2 * a + 1, 1), :] = jnp.mean(kf[blk:], axis=0,
                                                                   keepdims=True)
            return carry
        lax.fori_loop(0, n_blocks // 2, body, 0)

    def process_tile(own, own_is_odd, rows):
        def widen_queries(g):
            q = q_ref[rows, head_cols[g]]
            qaug_scr[g, :, :HEAD_DIM] = q
            if moba:
                kbar = kbar_scr[g]
                hi = kbar.astype(BF16)
                mid = (kbar - hi.astype(F32)).astype(BF16)
                lo = (kbar - hi.astype(F32) - mid.astype(F32)).astype(BF16)
                gs3 = lax.dot_general(jnp.concatenate([hi, mid, lo], axis=0), q, NT_DIMS,
                                      preferred_element_type=F32)
                gs = gs3[:n_blocks] + gs3[n_blocks:2 * n_blocks] + gs3[2 * n_blocks:]
                blk_id = lax.broadcasted_iota(jnp.int32, gs.shape, 0)
                q_blk = 2 * own + (lax.broadcasted_iota(jnp.int32, gs.shape, 1) >= blk).astype(
                    jnp.int32)
                gs = jnp.where(blk_id < q_blk, gs, NEG_INF)
                bias = jnp.where(blk_id == q_blk, 0.0, MASKED)
                for _ in range(MOBA_TOPK):
                    mx = jnp.max(gs, axis=0, keepdims=True)
                    first = jnp.min(jnp.where(gs == mx, blk_id, n_blocks), axis=0,
                                    keepdims=True)
                    pick = jnp.logical_and(blk_id == first, mx > NEG_INF)
                    bias = jnp.where(pick, 0.0, bias)
                    gs = jnp.where(pick, NEG_INF, gs)
                bias = jnp.concatenate([bias, jnp.zeros((LANES - n_blocks, pair), F32)], axis=0)
                qaug_scr[g, :, extra] = bias.T.astype(BF16)
            else:
                lane = lax.broadcasted_iota(jnp.int32, (pair, LANES), 1)
                head = hg * group + g
                mine = functools.reduce(
                    jnp.logical_or, [lane == head + j * n_heads for j in range(FOX_TERMS)])
                qaug_scr[g, :, extra] = jnp.where(mine, -1.0, 0.0).astype(BF16)

        def scores(g, a):
            off = pl.multiple_of(a * pair, pair)
            return lax.dot_general(kaug_scr[g, pl.ds(off, pair), :], qaug_scr[g], NT_DIMS,
                                   preferred_element_type=F32)

        def stage_scores(slot, g, s):
            s_scr[slot, g] = s
            smax_scr[slot, g] = jnp.max(s, axis=0, keepdims=True)

        def fold_head(slot, g, a):
            m_old = m_scr[g]
            m_new = jnp.maximum(m_old, smax_scr[slot, g])
            p = jnp.exp2(s_scr[slot, g] - m_new).astype(BF16)
            pv = jnp.dot(vt_scr[g, a], p, preferred_element_type=F32)
            acc_scr[g] = jnp.exp2(m_old - m_new) * acc_scr[g] + pv
            m_scr[g] = m_new

        own_off = pl.multiple_of(own * pair, pair)
        visible_lo = (lax.broadcasted_iota(jnp.int32, (blk, blk), 0)
                      <= lax.broadcasted_iota(jnp.int32, (blk, blk), 1))
        visible_hi = (lax.broadcasted_iota(jnp.int32, (pair, blk), 0)
                      <= lax.broadcasted_iota(jnp.int32, (pair, blk), 1) + blk)
        for g in range(group):
            widen_queries(g)
        for g in range(group):
            s_lo = lax.dot_general(kaug_scr[g, pl.ds(own_off, blk), :], qaug_scr[g, :blk, :],
                                   NT_DIMS, preferred_element_type=F32)
            s_hi = lax.dot_general(kaug_scr[g, pl.ds(own_off, pair), :], qaug_scr[g, blk:, :],
                                   NT_DIMS, preferred_element_type=F32)
            s_scr[1, g, :blk, :blk] = jnp.where(visible_lo, s_lo, NEG_INF)
            s_scr[1, g, :, blk:] = jnp.where(visible_hi, s_hi, NEG_INF)
        for g in range(group):
            stage_scores(0, g, scores(g, 0))
            s_lo = s_scr[1, g, :blk, :blk]
            s_hi = s_scr[1, g, :, blk:]
            m_lo = jnp.max(s_lo, axis=0, keepdims=True)
            m_hi = jnp.max(s_hi, axis=0, keepdims=True)
            p_lo = jnp.exp2(s_lo - m_lo).astype(BF16)
            p_hi = jnp.exp2(s_hi - m_hi).astype(BF16)
            acc_scr[g, :, :blk] = jnp.dot(vt_scr[g, own, :, :blk], p_lo,
                                          preferred_element_type=F32)
            acc_scr[g, :, blk:] = jnp.dot(vt_scr[g, own], p_hi, preferred_element_type=F32)
            m_scr[g, :, :blk] = m_lo
            m_scr[g, :, blk:] = m_hi

        def pipelined(slot_next, a_next, slot_cur, a_cur):
            for g in range(group):
                stage_scores(slot_next, g, scores(g, a_next))
                fold_head(slot_cur, g, a_cur)

        def two_pairs(t, carry):
            a0 = 2 * t
            pipelined(1, a0 + 1, 0, a0)
            pipelined(0, a0 + 2, 1, a0 + 1)
            return carry

        if own_is_odd:
            lax.fori_loop(0, lax.shift_right_logical(own - 1, 1), two_pairs, 0)
            for g in range(group):
                fold_head(0, g, own - 1)
        else:
            lax.fori_loop(0, lax.shift_right_logical(jnp.maximum(own - 2, 0), 1), two_pairs, 0)

            @pl.when(own >= 2)
            def _last_two_pairs():
                for g in range(group):
                    stage_scores(1, g, scores(g, own - 1))
                    fold_head(0, g, own - 2)
                for g in range(group):
                    fold_head(1, g, own - 1)

        for g, cols in enumerate(head_cols):
            acc = acc_scr[g]
            o = acc[:HEAD_DIM] * (1.0 / acc[HEAD_DIM:HEAD_DIM + 1])
            o_ref[rows, cols] = o.T.astype(o_ref.dtype)

    for t in range(tiles_per_step):
        process_tile(step * tiles_per_step + t, t % 2 == 1, slice(t * pair, (t + 1) * pair))


def _attention(proj, e, *, moba, batch, seq, heads, q_col, k_col, v_col):
    group = ATTN_GROUP
    tiles_per_step = ATTN_TILES_PER_STEP
    tq = KEY_PAIR
    rows_per_step = tiles_per_step * tq
    nq = seq // rows_per_step
    n_blocks = seq // MOBA_BLOCK
    width = group * HEAD_DIM
    assert heads % group == 0 and seq % rows_per_step == 0 and n_blocks <= LANES
    assert tiles_per_step % 2 == 0, "the kernel takes the parity of a tile index from its slot"
    assert q_col % group == 0 and k_col % group == 0 and v_col % group == 0
    in_specs = [
        pl.BlockSpec((rows_per_step, width), lambda b, h, i: (b * nq + i, q_col // group + h)),
        pl.BlockSpec((seq, width), lambda b, h, i: (b, k_col // group + h)),
        pl.BlockSpec((seq, width), lambda b, h, i: (b, v_col // group + h)),
    ]
    args = [proj, proj, proj]
    if not moba:
        in_specs.append(pl.BlockSpec((seq, LANES), lambda b, h, i: (b, 0)))
        args.append(e)
    scratch = [
        pltpu.VMEM((group, seq // KEY_PAIR, VT_ROWS, KEY_PAIR), BF16),
        pltpu.VMEM((group, seq, 2 * HEAD_DIM), BF16),
        pltpu.VMEM((group, tq, 2 * HEAD_DIM), BF16),
        pltpu.VMEM((2, group, KEY_PAIR, tq), F32),
        pltpu.VMEM((2, group, 1, tq), F32),
        pltpu.VMEM((group, 1, tq), F32),
        pltpu.VMEM((group, VT_ROWS, tq), F32),
    ]
    if moba:
        scratch.append(pltpu.VMEM((group, n_blocks, HEAD_DIM), F32))
    return pl.pallas_call(
        functools.partial(_attn_kernel, moba=moba, n_blocks=n_blocks, group=group,
                          n_heads=heads, tiles_per_step=tiles_per_step),
        grid=(batch, heads // group, nq),
        in_specs=in_specs,
        out_specs=pl.BlockSpec((rows_per_step, width), lambda b, h, i: (b * nq + i, h)),
        out_shape=jax.ShapeDtypeStruct((batch * seq, heads * HEAD_DIM), BF16),
        scratch_shapes=scratch,
        compiler_params=_compiler_params(3),
        name="moba_attention" if moba else "fox_attention",
    )(*args)


def _rmsnorm(x, g):
    ms = jnp.mean(x * x, axis=-1, keepdims=True)
    return x * lax.rsqrt(ms + RMS_EPS) * g


def _out_merge_kernel(oa_ref, ob_ref, za_ref, zb_ref, ga_ref, gb_ref, x_ref, p_ref,
                      wa_ref, wb_ref, wo_ref, wg_ref, wu_ref, gple_ref, gfin_ref, out_ref,
                      *, final):
    def branch(o_ref, z_ref, w_ref):
        z = z_ref[...].astype(F32)
        a = o_ref[...].astype(F32) * (z * jax.nn.sigmoid(z))
        return jnp.dot(a.astype(BF16), w_ref[...], preferred_element_type=F32)

    p_in = p_ref[...].astype(BF16)
    d_half = out_ref.shape[1] // 2
    ya = branch(oa_ref, za_ref, wa_ref)
    yb = branch(ob_ref, zb_ref, wb_ref)
    up_lo = jnp.dot(p_in, wu_ref[:, :d_half], preferred_element_type=F32)
    mixed = (jax.nn.sigmoid(ga_ref[...].astype(F32)) * ya
             + jax.nn.sigmoid(gb_ref[...].astype(F32)) * yb)
    x1 = x_ref[...] + jnp.dot(mixed.astype(BF16), wo_ref[...], preferred_element_type=F32)
    up_hi = jnp.dot(p_in, wu_ref[:, d_half:], preferred_element_type=F32)
    hn = _rmsnorm(x1, gple_ref[...]).astype(BF16)
    pg = jax.nn.sigmoid(jnp.dot(hn, wg_ref[...], preferred_element_type=F32))
    x2 = x1 + jnp.concatenate([up_lo, up_hi], axis=1) * pg
    if final:
        x2 = _rmsnorm(x2, gfin_ref[...])
    out_ref[...] = x2


def _out_merge(oa, ob, proj, x2, p2, wa, wb, wo, wg, wu, g_ple, g_final, *, tm, final,
               za_col, zb_col, ga_col, gb_col):
    t, d = x2.shape
    wa_w = oa.shape[1]
    wb_w = ob.shape[1]
    ple = p2.shape[1]
    row = lambda i: (i, 0)
    const = lambda i: (0, 0)
    return pl.pallas_call(
        functools.partial(_out_merge_kernel, final=final),
        grid=(t // tm,),
        in_specs=[
            pl.BlockSpec((tm, wa_w), row),
            pl.BlockSpec((tm, wb_w), row),
            pl.BlockSpec((tm, wa_w), lambda i: (i, za_col)),
            pl.BlockSpec((tm, wb_w), lambda i: (i, zb_col)),
            pl.BlockSpec((tm, d), lambda i: (i, ga_col)),
            pl.BlockSpec((tm, d), lambda i: (i, gb_col)),
            pl.BlockSpec((tm, d), row),
            pl.BlockSpec((tm, ple), row),
            _resident((wa_w, d), const),
            _resident((wb_w, d), const),
            _resident((d, d), const),
            _resident((d, d), const),
            _resident((ple, d), const),
            _resident((1, d), const),
            _resident((1, d), const),
        ],
        out_specs=pl.BlockSpec((tm, d), row),
        out_shape=jax.ShapeDtypeStruct((t, d), F32),
        compiler_params=_compiler_params(1),
        name="out_merge",
    )(oa, ob, proj, proj, proj, proj, x2, p2, wa, wb, wo, wg, wu, g_ple, g_final)


def kernel(x, p, positions, g_norm, w_in, b_f, w_branch_a, w_branch_b, w_out,
           g_ple, w_ple_gate, w_ple_up, g_final):
    batch, seq, d = x.shape
    depth = w_in.shape[0]
    t = batch * seq
    wa_w = w_branch_a.shape[1]
    wb_w = w_branch_b.shape[1]
    heads_a = wa_w // HEAD_DIM
    heads_b = wb_w // HEAD_DIM
    n_f = b_f.shape[1]
    assert seq % MOBA_BLOCK == 0 and wa_w == wb_w and 2 * wa_w == d
    f_start = 4 * wa_w + 3 * wb_w
    assert w_in.shape[2] == f_start + n_f + wb_w + 2 * d

    inv = ROPE_THETA ** (-jnp.arange(0, HEAD_DIM, 2, dtype=F32) / HEAD_DIM)
    inv_full = jnp.concatenate([inv, inv]).reshape(1, HEAD_DIM)
    pos = positions.reshape(t, 1)
    x2 = x.reshape(t, d)
    hb = HEAD_DIM

    for layer in range(depth):
        w_t = jnp.swapaxes(w_in[layer], 0, 1)
        w_main, w_f = _w_in_prep(w_t, f_start=f_start, n_f=n_f, tn=wa_w)
        b_pad = jnp.pad(b_f[layer], (0, LANES - n_f)).reshape(1, LANES)

        proj, f = _in_proj(x2, pos, g_norm[layer].reshape(1, d), inv_full, w_main, w_f,
                           tm=IN_PROJ_ROWS, tn=IN_PROJ_SECTIONS * wa_w, section=wa_w)
        e = _fox_gate(f, b_pad, batch=batch, seq=seq, n_heads=heads_b)
        oa = _attention(proj, None, moba=True, batch=batch, seq=seq, heads=heads_a,
                        q_col=0, k_col=wa_w // hb, v_col=2 * wa_w // hb)
        ob = _attention(proj, e, moba=False, batch=batch, seq=seq, heads=heads_b,
                        q_col=4 * wa_w // hb, k_col=(4 * wa_w + wb_w) // hb,
                        v_col=(4 * wa_w + 2 * wb_w) // hb)
        x2 = _out_merge(
            oa, ob, proj, x2, p[layer].reshape(t, -1),
            w_branch_a[layer].astype(BF16), w_branch_b[layer].astype(BF16),
            w_out[layer].astype(BF16), w_ple_gate[layer].astype(BF16),
            w_ple_up[layer].astype(BF16), g_ple[layer].reshape(1, d), g_final.reshape(1, d),
            tm=OUT_MERGE_ROWS, final=(layer == depth - 1),
            za_col=3 * wa_w // wa_w, zb_col=(4 * wa_w + 3 * wb_w) // wb_w,
            ga_col=(4 * wa_w + 4 * wb_w) // d, gb_col=(4 * wa_w + 4 * wb_w) // d + 1)
    return x2.reshape(batch, seq, d)
```

```python
import functools
import math

import jax
import jax.numpy as jnp
from jax import lax
from jax.experimental import pallas as pl
from jax.experimental.pallas import tpu as pltpu

HEAD_DIM = 128
MOBA_BLOCK = 256
MOBA_TOPK = 3
ROPE_THETA = 10000.0
RMS_EPS = 1e-6

V7X_VMEM_BYTES = 64 * 1024 * 1024
VMEM_LIMIT_BYTES = V7X_VMEM_BYTES * 7 // 8
LANES = 128
BF16_SUBLANES = 16

IN_PROJ_ROWS = 1024
IN_PROJ_ROW_PIECE = 512
IN_PROJ_SECTIONS = 2
OUT_MERGE_ROWS = 256

F32 = jnp.float32
BF16 = jnp.bfloat16
NEG_INF = float("-inf")
MASKED = -1e30
LOG2E = math.log2(math.e)
NT_DIMS = (((1,), (1,)), ((), ()))


def _compiler_params(n_grid_dims):
    return pltpu.CompilerParams(
        dimension_semantics=("arbitrary",) * n_grid_dims,
        vmem_limit_bytes=VMEM_LIMIT_BYTES,
    )


def _resident(block_shape, index_map):
    return pl.BlockSpec(block_shape, index_map, pipeline_mode=pl.Buffered(1))


F32_SUBLANES = 8
PREP_COLS = 512


def _w_in_prep_kernel(wt_hbm, out_ref, wf_ref, buf, fbuf, sem, fsem, *, n_lo, n_f):
    n = pl.program_id(0)
    tn = out_ref.shape[1]

    def fetch(j, slot):
        first_row = pl.multiple_of(j * tn + jnp.where(j >= n_lo, n_f, 0), F32_SUBLANES)
        return pltpu.make_async_copy(wt_hbm.at[pl.ds(first_row, tn), :], buf.at[slot],
                                     sem.at[slot])

    slot = jnp.bitwise_and(n, 1)

    @pl.when(n == 0)
    def _first():
        fetch(0, 0).start()
        gate_rows = pltpu.make_async_copy(wt_hbm.at[pl.ds(n_lo * tn, n_f), :], fbuf, fsem)
        gate_rows.start()
        gate_rows.wait()
        padded = jnp.concatenate(
            [fbuf[...], jnp.zeros((LANES - n_f, fbuf.shape[1]), F32)], axis=0)
        wf_ref[...] = padded.T.astype(BF16)

    @pl.when(n + 1 < pl.num_programs(0))
    def _ahead():
        fetch(n + 1, 1 - slot).start()

    fetch(n, slot).wait()
    for c0 in range(0, buf.shape[2], PREP_COLS):
        out_ref[c0:c0 + PREP_COLS, :] = buf[slot, :, c0:c0 + PREP_COLS].T.astype(BF16)


def _w_in_prep(w_t, *, f_start, n_f, tn):
    n_in, d = w_t.shape
    n_lo = f_start // tn
    n_tiles = (n_in - n_f) // tn
    assert f_start == n_lo * tn and n_in == n_tiles * tn + n_f
    assert n_f % F32_SUBLANES == 0 and n_f <= LANES and d % PREP_COLS == 0
    return pl.pallas_call(
        functools.partial(_w_in_prep_kernel, n_lo=n_lo, n_f=n_f),
        grid=(n_tiles,),
        in_specs=[pl.BlockSpec(memory_space=pl.ANY)],
        out_specs=[pl.BlockSpec((d, tn), lambda n: (0, n)),
                   pl.BlockSpec((d, LANES), lambda n: (0, 0))],
        out_shape=[jax.ShapeDtypeStruct((d, n_tiles * tn), BF16),
                   jax.ShapeDtypeStruct((d, LANES), BF16)],
        scratch_shapes=[pltpu.VMEM((2, tn, d), F32), pltpu.VMEM((n_f, d), F32),
                        pltpu.SemaphoreType.DMA((2,)), pltpu.SemaphoreType.DMA(())],
        compiler_params=_compiler_params(1),
        name="w_in_prep",
    )(w_t)


MXU_COLS = 256


def _in_proj_kernel(x_ref, pos_ref, g_ref, inv_ref, w_ref, wf_ref, proj_ref, f_ref,
                    h_scr, cos_scr, sin_scr, *, section, rope_sections, scaled_sections,
                    q_scale, row_tile):
    n = pl.program_id(1)
    half = HEAD_DIM // 2

    @pl.when(n == 0)
    def _normalize_and_tables():
        x = x_ref[...]
        ms = jnp.mean(x * x, axis=-1, keepdims=True)
        h = (x * lax.rsqrt(ms + RMS_EPS) * g_ref[...]).astype(BF16)
        h_scr[...] = h
        f_ref[...] = jnp.dot(h, wf_ref[...], preferred_element_type=F32)
        lane = lax.broadcasted_iota(jnp.int32, (x.shape[0], HEAD_DIM), 1)
        first = lane < half
        ang = pos_ref[...].astype(F32) * inv_ref[...]
        cs = jnp.cos(jnp.where(first, ang, ang - 0.5 * math.pi))
        sc = pltpu.roll(cs, half, 1)
        cos_scr[...] = jnp.where(first, cs, sc)
        sin_scr[...] = jnp.where(first, -sc, cs)

    tn = proj_ref.shape[1]
    coef_c, coef_s = [], []
    for k in range(tn // section):
        sec = n * (tn // section) + k
        is_rope = functools.reduce(jnp.logical_or, [sec == s for s in rope_sections])
        scaled = functools.reduce(jnp.logical_or, [sec == s for s in scaled_sections])
        scale = jnp.where(scaled, q_scale, 1.0).astype(F32)
        coef_c.append(jnp.where(is_rope, cos_scr[...], 1.0) * scale)
        coef_s.append(jnp.where(is_rope, sin_scr[...], 0.0) * scale)
    tm = h_scr.shape[0]
    for r0 in range(0, tm, row_tile):
        rows = slice(r0, r0 + row_tile)
        h = h_scr[rows, :]
        for c0 in range(0, tn, MXU_COLS):
            acc = jnp.dot(h, w_ref[:, c0:c0 + MXU_COLS],
                          preferred_element_type=F32)
            cc, cs = coef_c[c0 // section][rows], coef_s[c0 // section][rows]
            for j in range(0, MXU_COLS, HEAD_DIM):
                a = acc[:, j:j + HEAD_DIM]
                r = a * cc + pltpu.roll(a, half, 1) * cs
                proj_ref[rows, c0 + j:c0 + j + HEAD_DIM] = r.astype(BF16)


def _in_proj(x2, pos, g, inv_full, w_main, w_f, *, tm, tn, section):
    t, d = x2.shape
    n_main = w_main.shape[1]
    assert tn % section == 0 and n_main % tn == 0 and section % MXU_COLS == 0
    kern = functools.partial(
        _in_proj_kernel, section=section, rope_sections=(0, 1), scaled_sections=(0, 4),
        q_scale=LOG2E / math.sqrt(HEAD_DIM), row_tile=min(IN_PROJ_ROW_PIECE, tm))
    return pl.pallas_call(
        kern,
        grid=(t // tm, n_main // tn),
        in_specs=[
            pl.BlockSpec((tm, d), lambda i, n: (i, 0)),
            pl.BlockSpec((tm, 1), lambda i, n: (i, 0)),
            _resident((1, d), lambda i, n: (0, 0)),
            _resident((1, LANES), lambda i, n: (0, 0)),
            pl.BlockSpec((d, tn), lambda i, n: (0, n)),
            _resident((d, LANES), lambda i, n: (0, 0)),
        ],
        out_specs=[
            pl.BlockSpec((tm, tn), lambda i, n: (i, n)),
            pl.BlockSpec((tm, LANES), lambda i, n: (i, 0)),
        ],
        out_shape=[
            jax.ShapeDtypeStruct((t, n_main), BF16),
            jax.ShapeDtypeStruct((t, LANES), F32),
        ],
        scratch_shapes=[
            pltpu.VMEM((tm, d), BF16),
            pltpu.VMEM((tm, LANES), F32),
            pltpu.VMEM((tm, LANES), F32),
        ],
        compiler_params=_compiler_params(2),
        name="in_proj",
    )(x2, pos, g, inv_full, w_main, w_f)


FOX_TERMS = 3


def _fox_gate_kernel(f_ref, b_ref, e_ref, *, chunk, n_heads):
    n_chunks = f_ref.shape[0] // chunk
    r = lax.broadcasted_iota(jnp.int32, (chunk, chunk), 0)
    c = lax.broadcasted_iota(jnp.int32, (chunk, chunk), 1)
    tri = (c <= r).astype(BF16)
    z = f_ref[...] + b_ref[...]
    logf = (jnp.minimum(z, 0.0) - jnp.log1p(jnp.exp(-jnp.abs(z)))) * LOG2E
    wide = jnp.concatenate([logf[j * chunk:(j + 1) * chunk] for j in range(n_chunks)], axis=1)
    local = jnp.zeros_like(wide)
    for _ in range(FOX_TERMS):
        term = wide.astype(BF16)
        local = local + jnp.dot(tri, term, preferred_element_type=F32)
        wide = wide - term.astype(F32)
    lane = lax.broadcasted_iota(jnp.int32, (chunk, LANES), 1)
    offset = jnp.zeros((1, LANES), F32)
    for j in range(n_chunks):
        csum = local[:, j * LANES:(j + 1) * LANES] + offset
        offset = csum[chunk - 1:chunk, :]
        hi = csum.astype(BF16).astype(F32)
        mid = (csum - hi).astype(BF16).astype(F32)
        lo = csum - hi - mid
        feats = jnp.where(
            lane < n_heads, hi,
            jnp.where(lane < 2 * n_heads, pltpu.roll(mid, n_heads, 1),
                      jnp.where(lane < 3 * n_heads, pltpu.roll(lo, 2 * n_heads, 1), 0.0)))
        e_ref[j * chunk:(j + 1) * chunk, :] = feats.astype(BF16)


def _fox_gate(f, b_pad, *, batch, seq, n_heads):
    assert FOX_TERMS * n_heads <= LANES
    return pl.pallas_call(
        functools.partial(_fox_gate_kernel, chunk=MOBA_BLOCK, n_heads=n_heads),
        grid=(batch,),
        in_specs=[
            pl.BlockSpec((seq, LANES), lambda b: (b, 0)),
            _resident((1, LANES), lambda b: (0, 0)),
        ],
        out_specs=pl.BlockSpec((seq, LANES), lambda b: (b, 0)),
        out_shape=jax.ShapeDtypeStruct(f.shape, BF16),
        compiler_params=_compiler_params(1),
        name="fox_gate",
    )(f, b_pad)


ATTN_GROUP = 4
VT_ROWS = HEAD_DIM + BF16_SUBLANES
KEY_PAIR = 2 * MOBA_BLOCK
ATTN_TILES_PER_STEP = 2


def _attn_kernel(*refs, moba, n_blocks, group, n_heads, tiles_per_step):
    if moba:
        (q_ref, k_ref, v_ref, o_ref,
         vt_scr, kaug_scr, qaug_scr, s_scr, smax_scr, m_scr, acc_scr, kbar_scr) = refs
    else:
        (q_ref, k_ref, v_ref, e_ref, o_ref,
         vt_scr, kaug_scr, qaug_scr, s_scr, smax_scr, m_scr, acc_scr) = refs
    blk = MOBA_BLOCK
    pair = KEY_PAIR
    hg = pl.program_id(1)
    step = pl.program_id(2)
    head_cols = [slice(g * HEAD_DIM, (g + 1) * HEAD_DIM) for g in range(group)]
    extra = slice(HEAD_DIM, 2 * HEAD_DIM)

    @pl.when(step == 0)
    def _per_group_setup():
        ones_row = (lax.broadcasted_iota(jnp.int32, (BF16_SUBLANES, pair), 0) == 0).astype(BF16)
        lane = lax.broadcasted_iota(jnp.int32, (pair, LANES), 1)
        row = lax.broadcasted_iota(jnp.int32, (pair, LANES), 0)

        def body(a, carry):
            off = pl.multiple_of(a * pair, pair)
            if moba:
                block_of_row = 2 * a + (row >= blk).astype(jnp.int32)
                feats = jnp.where(lane == block_of_row, 1.0, 0.0).astype(BF16)
            else:
                feats = e_ref[pl.ds(off, pair), :]
            for g, cols in enumerate(head_cols):
                vt_scr[g, a, :HEAD_DIM, :] = v_ref[pl.ds(off, pair), cols].T
                vt_scr[g, a, HEAD_DIM:, :] = ones_row
                kb = k_ref[pl.ds(off, pair), cols]
                kaug_scr[g, pl.ds(off, pair), :HEAD_DIM] = kb
                kaug_scr[g, pl.ds(off, pair), extra] = feats
                if moba:
                    kf = kb.astype(F32)
                    kbar_scr[g, pl.ds(2 * a, 1), :] = jnp.mean(kf[:blk], axis=0, keepdims=True)
                    kbar_scr[g, pl.ds(2 * a + 1, 1), :] = jnp.mean(kf[blk:], axis=0,
                                                                   keepdims=True)
            return carry
        lax.fori_loop(0, n_blocks // 2, body, 0)

    def process_tile(own, own_is_odd, rows):
        def widen_queries(g):
            q = q_ref[rows, head_cols[g]]
            qaug_scr[g, :, :HEAD_DIM] = q
            if moba:
                kbar = kbar_scr[g]
                hi = kbar.astype(BF16)
                mid = (kbar - hi.astype(F32)).astype(BF16)
                lo = (kbar - hi.astype(F32) - mid.astype(F32)).astype(BF16)
                gs3 = lax.dot_general(jnp.concatenate([hi, mid, lo], axis=0), q, NT_DIMS,
                                      preferred_element_type=F32)
                gs = gs3[:n_blocks] + gs3[n_blocks:2 * n_blocks] + gs3[2 * n_blocks:]
                blk_id = lax.broadcasted_iota(jnp.int32, gs.shape, 0)
                q_blk = 2 * own + (lax.broadcasted_iota(jnp.int32, gs.shape, 1) >= blk).astype(
                    jnp.int32)
                gs = jnp.where(blk_id < q_blk, gs, NEG_INF)
                bias = jnp.where(blk_id == q_blk, 0.0, MASKED)
                for _ in range(MOBA_TOPK):
                    mx = jnp.max(gs, axis=0, keepdims=True)
                    first = jnp.min(jnp.where(gs == mx, blk_id, n_blocks), axis=0,
                                    keepdims=True)
                    pick = jnp.logical_and(blk_id == first, mx > NEG_INF)
                    bias = jnp.where(pick, 0.0, bias)
                    gs = jnp.where(pick, NEG_INF, gs)
                bias = jnp.concatenate([bias, jnp.zeros((LANES - n_blocks, pair), F32)], axis=0)
                qaug_scr[g, :, extra] = bias.T.astype(BF16)
            else:
                lane = lax.broadcasted_iota(jnp.int32, (pair, LANES), 1)
                head = hg * group + g
                mine = functools.reduce(
                    jnp.logical_or, [lane == head + j * n_heads for j in range(FOX_TERMS)])
                qaug_scr[g, :, extra] = jnp.where(mine, -1.0, 0.0).astype(BF16)

        def scores(g, a):
            off = pl.multiple_of(a * pair, pair)
            return lax.dot_general(kaug_scr[g, pl.ds(off, pair), :], qaug_scr[g], NT_DIMS,
                                   preferred_element_type=F32)

        def stage_scores(slot, g, s):
            s_scr[slot, g] = s
            smax_scr[slot, g] = jnp.max(s, axis=0, keepdims=True)

        def fold_head(slot, g, a):
            m_old = m_scr[g]
            m_new = jnp.maximum(m_old, smax_scr[slot, g])
            p = jnp.exp2(s_scr[slot, g] - m_new).astype(BF16)
            pv = jnp.dot(vt_scr[g, a], p, preferred_element_type=F32)
            acc_scr[g] = jnp.exp2(m_old - m_new) * acc_scr[g] + pv
            m_scr[g] = m_new

        own_off = pl.multiple_of(own * pair, pair)
        visible_lo = (lax.broadcasted_iota(jnp.int32, (blk, blk), 0)
                      <= lax.broadcasted_iota(jnp.int32, (blk, blk), 1))
        visible_hi = (lax.broadcasted_iota(jnp.int32, (pair, blk), 0)
                      <= lax.broadcasted_iota(jnp.int32, (pair, blk), 1) + blk)
        for g in range(group):
            widen_queries(g)
        for g in range(group):
            s_lo = lax.dot_general(kaug_scr[g, pl.ds(own_off, blk), :], qaug_scr[g, :blk, :],
                                   NT_DIMS, preferred_element_type=F32)
            s_hi = lax.dot_general(kaug_scr[g, pl.ds(own_off, pair), :], qaug_scr[g, blk:, :],
                                   NT_DIMS, preferred_element_type=F32)
            s_scr[1, g, :blk, :blk] = jnp.where(visible_lo, s_lo, NEG_INF)
            s_scr[1, g, :, blk:] = jnp.where(visible_hi, s_hi, NEG_INF)
        for g in range(group):
            stage_scores(0, g, scores(g, 0))
            s_lo = s_scr[1, g, :blk, :blk]
            s_hi = s_scr[1, g, :, blk:]
            m_lo = jnp.max(s_lo, axis=0, keepdims=True)
            m_hi = jnp.max(s_hi, axis=0, keepdims=True)
            p_lo = jnp.exp2(s_lo - m_lo).astype(BF16)
            p_hi = jnp.exp2(s_hi - m_hi).astype(BF16)
            acc_scr[g, :, :blk] = jnp.dot(vt_scr[g, own, :, :blk], p_lo,
                                          preferred_element_type=F32)
            acc_scr[g, :, blk:] = jnp.dot(vt_scr[g, own], p_hi, preferred_element_type=F32)
            m_scr[g, :, :blk] = m_lo
            m_scr[g, :, blk:] = m_hi

        def pipelined(slot_next, a_next, slot_cur, a_cur):
            for g in range(group):
                stage_scores(slot_next, g, scores(g, a_next))
                fold_head(slot_cur, g, a_cur)

        def two_pairs(t, carry):
            a0 = 2 * t
            pipelined(1, a0 + 1, 0, a0)
            pipelined(0, a0 + 2, 1, a0 + 1)
            return carry

        if own_is_odd:
            lax.fori_loop(0, lax.shift_right_logical(own - 1, 1), two_pairs, 0)
            for g in range(group):
                fold_head(0, g, own - 1)
        else:
            lax.fori_loop(0, lax.shift_right_logical(jnp.maximum(own - 2, 0), 1), two_pairs, 0)

            @pl.when(own >= 2)
            def _last_two_pairs():
                for g in range(group):
                    stage_scores(1, g, scores(g, own - 1))
                    fold_head(0, g, own - 2)
                for g in range(group):
                    fold_head(1, g, own - 1)

        for g, cols in enumerate(head_cols):
            acc = acc_scr[g]
            o = acc[:HEAD_DIM] * (1.0 / acc[HEAD_DIM:HEAD_DIM + 1])
            o_ref[rows, cols] = o.T.astype(o_ref.dtype)

    for t in range(tiles_per_step):
        process_tile(step * tiles_per_step + t, t % 2 == 1, slice(t * pair, (t + 1) * pair))


CAST_ROWS = 256


def _attn_and_cast_kernel(*refs, n_attn_in, n_cast, n_attn_scratch, chunk_table, **attn_kw):
    attn_in = refs[:n_attn_in]
    cast_in = refs[n_attn_in:n_attn_in + n_cast]
    o_ref = refs[n_attn_in + n_cast]
    cast_out = refs[n_attn_in + n_cast + 1:n_attn_in + 2 * n_cast + 1]
    rest = refs[n_attn_in + 2 * n_cast + 1:]
    attn_scratch, (ibuf, obuf, isem, osem) = rest[:n_attn_scratch], rest[n_attn_scratch:]
    lin = ((pl.program_id(0) * pl.num_programs(1) + pl.program_id(1)) * pl.num_programs(2)
           + pl.program_id(2))

    def fetch(k, j):
        rows = pl.ds(pl.multiple_of(j * CAST_ROWS, CAST_ROWS), CAST_ROWS)
        return pltpu.make_async_copy(cast_in[k].at[rows, :], ibuf, isem)

    def write_back(k, j):
        rows = pl.ds(pl.multiple_of(j * CAST_ROWS, CAST_ROWS), CAST_ROWS)
        return pltpu.make_async_copy(obuf, cast_out[k].at[rows, :], osem)

    def for_slab_of_step(s, fn):
        for k, (first, n) in enumerate(chunk_table):
            @pl.when(jnp.logical_and(s >= first, s < first + n))
            def _(k=k, first=first):
                fn(k, s - first)

    for_slab_of_step(lin, lambda k, j: fetch(k, j).start())
    _attn_kernel(*attn_in, o_ref, *attn_scratch, **attn_kw)
    for_slab_of_step(lin, lambda k, j: fetch(k, j).wait())
    for_slab_of_step(lin - 1, lambda k, j: write_back(k, j).wait())

    def cast_and_send(k, j):
        obuf[...] = ibuf[...].astype(BF16)
        write_back(k, j).start()

    for_slab_of_step(lin, cast_and_send)


def _attention(proj, e, *, moba, batch, seq, heads, q_col, k_col, v_col, cast_weights=()):
    group = ATTN_GROUP
    tiles_per_step = ATTN_TILES_PER_STEP
    tq = KEY_PAIR
    rows_per_step = tiles_per_step * tq
    nq = seq // rows_per_step
    n_blocks = seq // MOBA_BLOCK
    width = group * HEAD_DIM
    assert heads % group == 0 and seq % rows_per_step == 0 and n_blocks <= LANES
    assert tiles_per_step % 2 == 0, "the kernel takes the parity of a tile index from its slot"
    assert q_col % group == 0 and k_col % group == 0 and v_col % group == 0
    in_specs = [
        pl.BlockSpec((rows_per_step, width), lambda b, h, i: (b * nq + i, q_col // group + h)),
        pl.BlockSpec((seq, width), lambda b, h, i: (b, k_col // group + h)),
        pl.BlockSpec((seq, width), lambda b, h, i: (b, v_col // group + h)),
    ]
    args = [proj, proj, proj]
    if not moba:
        in_specs.append(pl.BlockSpec((seq, LANES), lambda b, h, i: (b, 0)))
        args.append(e)
    scratch = [
        pltpu.VMEM((group, seq // KEY_PAIR, VT_ROWS, KEY_PAIR), BF16),
        pltpu.VMEM((group, seq, 2 * HEAD_DIM), BF16),
        pltpu.VMEM((group, tq, 2 * HEAD_DIM), BF16),
        pltpu.VMEM((2, group, KEY_PAIR, tq), F32),
        pltpu.VMEM((2, group, 1, tq), F32),
        pltpu.VMEM((group, 1, tq), F32),
        pltpu.VMEM((group, VT_ROWS, tq), F32),
    ]
    if moba:
        scratch.append(pltpu.VMEM((group, n_blocks, HEAD_DIM), F32))
    attn_kw = dict(moba=moba, n_blocks=n_blocks, group=group, n_heads=heads,
                   tiles_per_step=tiles_per_step)
    grid = (batch, heads // group, nq)
    o_spec = pl.BlockSpec((rows_per_step, width), lambda b, h, i: (b * nq + i, h))
    o_shape = jax.ShapeDtypeStruct((batch * seq, heads * HEAD_DIM), BF16)
    name = "moba_attention" if moba else "fox_attention"
    if not cast_weights:
        return pl.pallas_call(
            functools.partial(_attn_kernel, **attn_kw),
            grid=grid, in_specs=in_specs, out_specs=o_spec, out_shape=o_shape,
            scratch_shapes=scratch, compiler_params=_compiler_params(3), name=name,
        )(*args)

    cols = cast_weights[0].shape[1]
    chunk_table, first = [], 0
    for w in cast_weights:
        assert w.shape[1] == cols and w.shape[0] % CAST_ROWS == 0 and w.dtype == F32
        chunk_table.append((first, w.shape[0] // CAST_ROWS))
        first += w.shape[0] // CAST_ROWS
    assert first < math.prod(grid), "the last slab's write is waited one step after its cast"
    any_spec = pl.BlockSpec(memory_space=pl.ANY)
    kern = functools.partial(
        _attn_and_cast_kernel, n_attn_in=len(args), n_cast=len(cast_weights),
        n_attn_scratch=len(scratch), chunk_table=tuple(chunk_table), **attn_kw)
    return pl.pallas_call(
        kern,
        grid=grid,
        in_specs=in_specs + [any_spec] * len(cast_weights),
        out_specs=[o_spec] + [any_spec] * len(cast_weights),
        out_shape=[o_shape] + [jax.ShapeDtypeStruct(w.shape, BF16) for w in cast_weights],
        scratch_shapes=scratch + [pltpu.VMEM((CAST_ROWS, cols), F32),
                                  pltpu.VMEM((CAST_ROWS, cols), BF16),
                                  pltpu.SemaphoreType.DMA(()), pltpu.SemaphoreType.DMA(())],
        compiler_params=_compiler_params(3),
        name=name,
    )(*args, *cast_weights)


def _rmsnorm(x, g):
    ms = jnp.mean(x * x, axis=-1, keepdims=True)
    return x * lax.rsqrt(ms + RMS_EPS) * g


def _out_merge_kernel(oa_ref, ob_ref, za_ref, zb_ref, ga_ref, gb_ref, x_ref, p_ref,
                      wa_ref, wb_ref, wo_ref, wg_ref, wu_ref, gple_ref, gfin_ref, out_ref,
                      *, final):
    def branch(o_ref, z_ref, w_ref):
        z = z_ref[...].astype(F32)
        a = o_ref[...].astype(F32) * (z * jax.nn.sigmoid(z))
        return jnp.dot(a.astype(BF16), w_ref[...], preferred_element_type=F32)

    p_in = p_ref[...].astype(BF16)
    d_half = out_ref.shape[1] // 2
    ya = branch(oa_ref, za_ref, wa_ref)
    yb = branch(ob_ref, zb_ref, wb_ref)
    up_lo = jnp.dot(p_in, wu_ref[:, :d_half], preferred_element_type=F32)
    mixed = (jax.nn.sigmoid(ga_ref[...].astype(F32)) * ya
             + jax.nn.sigmoid(gb_ref[...].astype(F32)) * yb)
    x1 = x_ref[...] + jnp.dot(mixed.astype(BF16), wo_ref[...], preferred_element_type=F32)
    up_hi = jnp.dot(p_in, wu_ref[:, d_half:], preferred_element_type=F32)
    hn = _rmsnorm(x1, gple_ref[...]).astype(BF16)
    pg = jax.nn.sigmoid(jnp.dot(hn, wg_ref[...], preferred_element_type=F32))
    x2 = x1 + jnp.concatenate([up_lo, up_hi], axis=1) * pg
    if final:
        x2 = _rmsnorm(x2, gfin_ref[...])
    out_ref[...] = x2


def _out_merge(oa, ob, proj, x2, p2, wa, wb, wo, wg, wu, g_ple, g_final, *, tm, final,
               za_col, zb_col, ga_col, gb_col):
    t, d = x2.shape
    wa_w = oa.shape[1]
    wb_w = ob.shape[1]
    ple = p2.shape[1]
    row = lambda i: (i, 0)
    const = lambda i: (0, 0)
    return pl.pallas_call(
        functools.partial(_out_merge_kernel, final=final),
        grid=(t // tm,),
        in_specs=[
            pl.BlockSpec((tm, wa_w), row),
            pl.BlockSpec((tm, wb_w), row),
            pl.BlockSpec((tm, wa_w), lambda i: (i, za_col)),
            pl.BlockSpec((tm, wb_w), lambda i: (i, zb_col)),
            pl.BlockSpec((tm, d), lambda i: (i, ga_col)),
            pl.BlockSpec((tm, d), lambda i: (i, gb_col)),
            pl.BlockSpec((tm, d), row),
            pl.BlockSpec((tm, ple), row),
            _resident((wa_w, d), const),
            _resident((wb_w, d), const),
            _resident((d, d), const),
            _resident((d, d), const),
            _resident((ple, d), const),
            _resident((1, d), const),
            _resident((1, d), const),
        ],
        out_specs=pl.BlockSpec((tm, d), row),
        out_shape=jax.ShapeDtypeStruct((t, d), F32),
        compiler_params=_compiler_params(1),
        name="out_merge",
    )(oa, ob, proj, proj, proj, proj, x2, p2, wa, wb, wo, wg, wu, g_ple, g_final)


def kernel(x, p, positions, g_norm, w_in, b_f, w_branch_a, w_branch_b, w_out,
           g_ple, w_ple_gate, w_ple_up, g_final):
    batch, seq, d = x.shape
    depth = w_in.shape[0]
    t = batch * seq
    wa_w = w_branch_a.shape[1]
    wb_w = w_branch_b.shape[1]
    heads_a = wa_w // HEAD_DIM
    heads_b = wb_w // HEAD_DIM
    n_f = b_f.shape[1]
    assert seq % MOBA_BLOCK == 0 and wa_w == wb_w and 2 * wa_w == d
    f_start = 4 * wa_w + 3 * wb_w
    assert w_in.shape[2] == f_start + n_f + wb_w + 2 * d

    inv = ROPE_THETA ** (-jnp.arange(0, HEAD_DIM, 2, dtype=F32) / HEAD_DIM)
    inv_full = jnp.concatenate([inv, inv]).reshape(1, HEAD_DIM)
    pos = positions.reshape(t, 1)
    x2 = x.reshape(t, d)
    hb = HEAD_DIM

    for layer in range(depth):
        w_t = jnp.swapaxes(w_in[layer], 0, 1)
        w_main, w_f = _w_in_prep(w_t, f_start=f_start, n_f=n_f, tn=wa_w)
        b_pad = jnp.pad(b_f[layer], (0, LANES - n_f)).reshape(1, LANES)

        proj, f = _in_proj(x2, pos, g_norm[layer].reshape(1, d), inv_full, w_main, w_f,
                           tm=IN_PROJ_ROWS, tn=IN_PROJ_SECTIONS * wa_w, section=wa_w)
        e = _fox_gate(f, b_pad, batch=batch, seq=seq, n_heads=heads_b)
        oa = _attention(proj, None, moba=True, batch=batch, seq=seq, heads=heads_a,
                        q_col=0, k_col=wa_w // hb, v_col=2 * wa_w // hb)
        ob, wa, wb, wo, wg, wu = _attention(
            proj, e, moba=False, batch=batch, seq=seq, heads=heads_b,
            q_col=4 * wa_w // hb, k_col=(4 * wa_w + wb_w) // hb,
            v_col=(4 * wa_w + 2 * wb_w) // hb,
            cast_weights=(w_branch_a[layer], w_branch_b[layer], w_out[layer],
                          w_ple_gate[layer], w_ple_up[layer]))
        x2 = _out_merge(
            oa, ob, proj, x2, p[layer].reshape(t, -1), wa, wb, wo, wg, wu,
            g_ple[layer].reshape(1, d), g_final.reshape(1, d),
            tm=OUT_MERGE_ROWS, final=(layer == depth - 1),
            za_col=3 * wa_w // wa_w, zb_col=(4 * wa_w + 3 * wb_w) // wb_w,
            ga_col=(4 * wa_w + 4 * wb_w) // d, gb_col=(4 * wa_w + 4 * wb_w) // d + 1)
    return x2.reshape(batch, seq, d)
```

```python
import functools
import math

import jax
import jax.numpy as jnp
from jax import lax
from jax.experimental import pallas as pl
from jax.experimental.pallas import tpu as pltpu

HEAD_DIM = 128
MOBA_BLOCK = 256
MOBA_TOPK = 3
ROPE_THETA = 10000.0
RMS_EPS = 1e-6

V7X_VMEM_BYTES = 64 * 1024 * 1024
VMEM_LIMIT_BYTES = V7X_VMEM_BYTES * 7 // 8
LANES = 128
BF16_SUBLANES = 16

IN_PROJ_ROWS = 1024
IN_PROJ_ROW_PIECE = 512
IN_PROJ_SECTIONS = 2
OUT_MERGE_ROWS = 256

F32 = jnp.float32
BF16 = jnp.bfloat16
NEG_INF = float("-inf")
MASKED = -1e30
LOG2E = math.log2(math.e)
NT_DIMS = (((1,), (1,)), ((), ()))


def _compiler_params(n_grid_dims):
    return pltpu.CompilerParams(
        dimension_semantics=("arbitrary",) * n_grid_dims,
        vmem_limit_bytes=VMEM_LIMIT_BYTES,
    )


def _resident(block_shape, index_map):
    return pl.BlockSpec(block_shape, index_map, pipeline_mode=pl.Buffered(1))


F32_SUBLANES = 8
PREP_COLS = 512


def _w_in_prep_kernel(wt_hbm, out_ref, wf_ref, buf, fbuf, sem, fsem, *, n_lo, n_f):
    n = pl.program_id(0)
    tn = out_ref.shape[1]

    def fetch(j, slot):
        first_row = pl.multiple_of(j * tn + jnp.where(j >= n_lo, n_f, 0), F32_SUBLANES)
        return pltpu.make_async_copy(wt_hbm.at[pl.ds(first_row, tn), :], buf.at[slot],
                                     sem.at[slot])

    slot = jnp.bitwise_and(n, 1)

    @pl.when(n == 0)
    def _first():
        fetch(0, 0).start()
        gate_rows = pltpu.make_async_copy(wt_hbm.at[pl.ds(n_lo * tn, n_f), :], fbuf, fsem)
        gate_rows.start()
        gate_rows.wait()
        padded = jnp.concatenate(
            [fbuf[...], jnp.zeros((LANES - n_f, fbuf.shape[1]), F32)], axis=0)
        wf_ref[...] = padded.T.astype(BF16)

    @pl.when(n + 1 < pl.num_programs(0))
    def _ahead():
        fetch(n + 1, 1 - slot).start()

    fetch(n, slot).wait()
    for c0 in range(0, buf.shape[2], PREP_COLS):
        out_ref[c0:c0 + PREP_COLS, :] = buf[slot, :, c0:c0 + PREP_COLS].T.astype(BF16)


def _w_in_prep(w_t, *, f_start, n_f, tn):
    n_in, d = w_t.shape
    n_lo = f_start // tn
    n_tiles = (n_in - n_f) // tn
    assert f_start == n_lo * tn and n_in == n_tiles * tn + n_f
    assert n_f % F32_SUBLANES == 0 and n_f <= LANES and d % PREP_COLS == 0
    return pl.pallas_call(
        functools.partial(_w_in_prep_kernel, n_lo=n_lo, n_f=n_f),
        grid=(n_tiles,),
        in_specs=[pl.BlockSpec(memory_space=pl.ANY)],
        out_specs=[pl.BlockSpec((d, tn), lambda n: (0, n)),
                   pl.BlockSpec((d, LANES), lambda n: (0, 0))],
        out_shape=[jax.ShapeDtypeStruct((d, n_tiles * tn), BF16),
                   jax.ShapeDtypeStruct((d, LANES), BF16)],
        scratch_shapes=[pltpu.VMEM((2, tn, d), F32), pltpu.VMEM((n_f, d), F32),
                        pltpu.SemaphoreType.DMA((2,)), pltpu.SemaphoreType.DMA(())],
        compiler_params=_compiler_params(1),
        name="w_in_prep",
    )(w_t)


MXU_COLS = 256


def _in_proj_kernel(x_ref, pos_ref, g_ref, inv_ref, w_ref, wf_ref, proj_ref, f_ref,
                    h_scr, cos_scr, sin_scr, *, section, rope_sections, scaled_sections,
                    q_scale, row_tile):
    n = pl.program_id(1)
    half = HEAD_DIM // 2

    @pl.when(n == 0)
    def _normalize_and_tables():
        x = x_ref[...]
        ms = jnp.mean(x * x, axis=-1, keepdims=True)
        h = (x * lax.rsqrt(ms + RMS_EPS) * g_ref[...]).astype(BF16)
        h_scr[...] = h
        f_ref[...] = jnp.dot(h, wf_ref[...], preferred_element_type=F32)
        lane = lax.broadcasted_iota(jnp.int32, (x.shape[0], HEAD_DIM), 1)
        first = lane < half
        ang = pos_ref[...].astype(F32) * inv_ref[...]
        cs = jnp.cos(jnp.where(first, ang, ang - 0.5 * math.pi))
        sc = pltpu.roll(cs, half, 1)
        cos_scr[...] = jnp.where(first, cs, sc)
        sin_scr[...] = jnp.where(first, -sc, cs)

    tn = proj_ref.shape[1]
    coef_c, coef_s = [], []
    for k in range(tn // section):
        sec = n * (tn // section) + k
        is_rope = functools.reduce(jnp.logical_or, [sec == s for s in rope_sections])
        scaled = functools.reduce(jnp.logical_or, [sec == s for s in scaled_sections])
        scale = jnp.where(scaled, q_scale, 1.0).astype(F32)
        coef_c.append(jnp.where(is_rope, cos_scr[...], 1.0) * scale)
        coef_s.append(jnp.where(is_rope, sin_scr[...], 0.0) * scale)
    tm = h_scr.shape[0]
    for r0 in range(0, tm, row_tile):
        rows = slice(r0, r0 + row_tile)
        h = h_scr[rows, :]
        for c0 in range(0, tn, MXU_COLS):
            acc = jnp.dot(h, w_ref[:, c0:c0 + MXU_COLS],
                          preferred_element_type=F32)
            cc, cs = coef_c[c0 // section][rows], coef_s[c0 // section][rows]
            for j in range(0, MXU_COLS, HEAD_DIM):
                a = acc[:, j:j + HEAD_DIM]
                r = a * cc + pltpu.roll(a, half, 1) * cs
                proj_ref[rows, c0 + j:c0 + j + HEAD_DIM] = r.astype(BF16)


def _in_proj(x2, pos, g, inv_full, w_main, w_f, *, tm, tn, section):
    t, d = x2.shape
    n_main = w_main.shape[1]
    assert tn % section == 0 and n_main % tn == 0 and section % MXU_COLS == 0
    kern = functools.partial(
        _in_proj_kernel, section=section, rope_sections=(0, 1), scaled_sections=(0, 4),
        q_scale=LOG2E / math.sqrt(HEAD_DIM), row_tile=min(IN_PROJ_ROW_PIECE, tm))
    return pl.pallas_call(
        kern,
        grid=(t // tm, n_main // tn),
        in_specs=[
            pl.BlockSpec((tm, d), lambda i, n: (i, 0)),
            pl.BlockSpec((tm, 1), lambda i, n: (i, 0)),
            _resident((1, d), lambda i, n: (0, 0)),
            _resident((1, LANES), lambda i, n: (0, 0)),
            pl.BlockSpec((d, tn), lambda i, n: (0, n)),
            _resident((d, LANES), lambda i, n: (0, 0)),
        ],
        out_specs=[
            pl.BlockSpec((tm, tn), lambda i, n: (i, n)),
            pl.BlockSpec((tm, LANES), lambda i, n: (i, 0)),
        ],
        out_shape=[
            jax.ShapeDtypeStruct((t, n_main), BF16),
            jax.ShapeDtypeStruct((t, LANES), F32),
        ],
        scratch_shapes=[
            pltpu.VMEM((tm, d), BF16),
            pltpu.VMEM((tm, LANES), F32),
            pltpu.VMEM((tm, LANES), F32),
        ],
        compiler_params=_compiler_params(2),
        name="in_proj",
    )(x2, pos, g, inv_full, w_main, w_f)


FOX_TERMS = 3


def _fox_gate_kernel(f_ref, b_ref, e_ref, *, chunk, n_heads):
    n_chunks = f_ref.shape[0] // chunk
    r = lax.broadcasted_iota(jnp.int32, (chunk, chunk), 0)
    c = lax.broadcasted_iota(jnp.int32, (chunk, chunk), 1)
    tri = (c <= r).astype(BF16)
    z = f_ref[...] + b_ref[...]
    logf = (jnp.minimum(z, 0.0) - jnp.log1p(jnp.exp(-jnp.abs(z)))) * LOG2E
    wide = jnp.concatenate([logf[j * chunk:(j + 1) * chunk] for j in range(n_chunks)], axis=1)
    local = jnp.zeros_like(wide)
    for _ in range(FOX_TERMS):
        term = wide.astype(BF16)
        local = local + jnp.dot(tri, term, preferred_element_type=F32)
        wide = wide - term.astype(F32)
    lane = lax.broadcasted_iota(jnp.int32, (chunk, LANES), 1)
    offset = jnp.zeros((1, LANES), F32)
    for j in range(n_chunks):
        csum = local[:, j * LANES:(j + 1) * LANES] + offset
        offset = csum[chunk - 1:chunk, :]
        hi = csum.astype(BF16).astype(F32)
        mid = (csum - hi).astype(BF16).astype(F32)
        lo = csum - hi - mid
        feats = jnp.where(
            lane < n_heads, hi,
            jnp.where(lane < 2 * n_heads, pltpu.roll(mid, n_heads, 1),
                      jnp.where(lane < 3 * n_heads, pltpu.roll(lo, 2 * n_heads, 1), 0.0)))
        e_ref[j * chunk:(j + 1) * chunk, :] = feats.astype(BF16)


def _fox_gate(f, b_pad, *, batch, seq, n_heads):
    assert FOX_TERMS * n_heads <= LANES
    return pl.pallas_call(
        functools.partial(_fox_gate_kernel, chunk=MOBA_BLOCK, n_heads=n_heads),
        grid=(batch,),
        in_specs=[
            pl.BlockSpec((seq, LANES), lambda b: (b, 0)),
            _resident((1, LANES), lambda b: (0, 0)),
        ],
        out_specs=pl.BlockSpec((seq, LANES), lambda b: (b, 0)),
        out_shape=jax.ShapeDtypeStruct(f.shape, BF16),
        compiler_params=_compiler_params(1),
        name="fox_gate",
    )(f, b_pad)


ATTN_GROUP = 4
VT_ROWS = HEAD_DIM + BF16_SUBLANES
KEY_PAIR = 2 * MOBA_BLOCK
ATTN_TILES_PER_STEP = 2


def _attn_kernel(*refs, moba, n_blocks, group, n_heads, tiles_per_step):
    if moba:
        (q_ref, k_ref, v_ref, o_ref,
         vt_scr, kaug_scr, qaug_scr, s_scr, smax_scr, m_scr, acc_scr, kbar_scr) = refs
    else:
        (q_ref, k_ref, v_ref, e_ref, o_ref,
         vt_scr, kaug_scr, qaug_scr, s_scr, smax_scr, m_scr, acc_scr) = refs
    blk = MOBA_BLOCK
    pair = KEY_PAIR
    hg = pl.program_id(1)
    step = pl.program_id(2)
    head_cols = [slice(g * HEAD_DIM, (g + 1) * HEAD_DIM) for g in range(group)]
    extra = slice(HEAD_DIM, 2 * HEAD_DIM)

    @pl.when(step == 0)
    def _per_group_setup():
        ones_row = (lax.broadcasted_iota(jnp.int32, (BF16_SUBLANES, pair), 0) == 0).astype(BF16)
        lane = lax.broadcasted_iota(jnp.int32, (pair, LANES), 1)
        row = lax.broadcasted_iota(jnp.int32, (pair, LANES), 0)

        def body(a, carry):
            off = pl.multiple_of(a * pair, pair)
            if moba:
                block_of_row = 2 * a + (row >= blk).astype(jnp.int32)
                feats = jnp.where(lane == block_of_row, 1.0, 0.0).astype(BF16)
            else:
                feats = e_ref[pl.ds(off, pair), :]
            for g, cols in enumerate(head_cols):
                vt_scr[g, a, :HEAD_DIM, :] = v_ref[pl.ds(off, pair), cols].T
                vt_scr[g, a, HEAD_DIM:, :] = ones_row
                kb = k_ref[pl.ds(off, pair), cols]
                kaug_scr[g, pl.ds(off, pair), :HEAD_DIM] = kb
                kaug_scr[g, pl.ds(off, pair), extra] = feats
                if moba:
                    kf = kb.astype(F32)
                    kbar_scr[g, pl.ds(2 * a, 1), :] = jnp.mean(kf[:blk], axis=0, keepdims=True)
                    kbar_scr[g, pl.ds(2 * a + 1, 1), :] = jnp.mean(kf[blk:], axis=0,
                                                                   keepdims=True)
            return carry
        lax.fori_loop(0, n_blocks // 2, body, 0)

    def process_tile(own, own_is_odd, rows):
        def widen_queries(g):
            q = q_ref[rows, head_cols[g]]
            qaug_scr[g, :, :HEAD_DIM] = q
            if moba:
                kbar = kbar_scr[g]
                hi = kbar.astype(BF16)
                mid = (kbar - hi.astype(F32)).astype(BF16)
                lo = (kbar - hi.astype(F32) - mid.astype(F32)).astype(BF16)
                gs3 = lax.dot_general(jnp.concatenate([hi, mid, lo], axis=0), q, NT_DIMS,
                                      preferred_element_type=F32)
                gs = gs3[:n_blocks] + gs3[n_blocks:2 * n_blocks] + gs3[2 * n_blocks:]
                blk_id = lax.broadcasted_iota(jnp.int32, gs.shape, 0)
                q_blk = 2 * own + (lax.broadcasted_iota(jnp.int32, gs.shape, 1) >= blk).astype(
                    jnp.int32)
                gs = jnp.where(blk_id < q_blk, gs, NEG_INF)
                bias = jnp.where(blk_id == q_blk, 0.0, MASKED)
                for _ in range(MOBA_TOPK):
                    mx = jnp.max(gs, axis=0, keepdims=True)
                    first = jnp.min(jnp.where(gs == mx, blk_id, n_blocks), axis=0,
                                    keepdims=True)
                    pick = jnp.logical_and(blk_id == first, mx > NEG_INF)
                    bias = jnp.where(pick, 0.0, bias)
                    gs = jnp.where(pick, NEG_INF, gs)
                bias = jnp.concatenate([bias, jnp.zeros((LANES - n_blocks, pair), F32)], axis=0)
                qaug_scr[g, :, extra] = bias.T.astype(BF16)
            else:
                lane = lax.broadcasted_iota(jnp.int32, (pair, LANES), 1)
                head = hg * group + g
                mine = functools.reduce(
                    jnp.logical_or, [lane == head + j * n_heads for j in range(FOX_TERMS)])
                qaug_scr[g, :, extra] = jnp.where(mine, -1.0, 0.0).astype(BF16)

        def scores(g, a):
            off = pl.multiple_of(a * pair, pair)
            return lax.dot_general(kaug_scr[g, pl.ds(off, pair), :], qaug_scr[g], NT_DIMS,
                                   preferred_element_type=F32)

        def stage_scores(slot, g, s):
            s_scr[slot, g] = s
            smax_scr[slot, g] = jnp.max(s, axis=0, keepdims=True)

        def fold_head(slot, g, a):
            m_old = m_scr[g]
            m_new = jnp.maximum(m_old, smax_scr[slot, g])
            p = jnp.exp2(s_scr[slot, g] - m_new).astype(BF16)
            pv = jnp.dot(vt_scr[g, a], p, preferred_element_type=F32)
            acc_scr[g] = jnp.exp2(m_old - m_new) * acc_scr[g] + pv
            m_scr[g] = m_new

        own_off = pl.multiple_of(own * pair, pair)
        visible_lo = (lax.broadcasted_iota(jnp.int32, (blk, blk), 0)
                      <= lax.broadcasted_iota(jnp.int32, (blk, blk), 1))
        visible_hi = (lax.broadcasted_iota(jnp.int32, (pair, blk), 0)
                      <= lax.broadcasted_iota(jnp.int32, (pair, blk), 1) + blk)
        for g in range(group):
            widen_queries(g)
        for g in range(group):
            s_lo = lax.dot_general(kaug_scr[g, pl.ds(own_off, blk), :], qaug_scr[g, :blk, :],
                                   NT_DIMS, preferred_element_type=F32)
            s_hi = lax.dot_general(kaug_scr[g, pl.ds(own_off, pair), :], qaug_scr[g, blk:, :],
                                   NT_DIMS, preferred_element_type=F32)
            s_scr[1, g, :blk, :blk] = jnp.where(visible_lo, s_lo, NEG_INF)
            s_scr[1, g, :, blk:] = jnp.where(visible_hi, s_hi, NEG_INF)
        def fold_own_pair(prefetch_first_past_pair):
            for g in range(group):
                if prefetch_first_past_pair:
                    stage_scores(0, g, scores(g, 0))
                s_lo = s_scr[1, g, :blk, :blk]
                s_hi = s_scr[1, g, :, blk:]
                m_lo = jnp.max(s_lo, axis=0, keepdims=True)
                m_hi = jnp.max(s_hi, axis=0, keepdims=True)
                p_lo = jnp.exp2(s_lo - m_lo).astype(BF16)
                p_hi = jnp.exp2(s_hi - m_hi).astype(BF16)
                acc_scr[g, :, :blk] = jnp.dot(vt_scr[g, own, :, :blk], p_lo,
                                              preferred_element_type=F32)
                acc_scr[g, :, blk:] = jnp.dot(vt_scr[g, own], p_hi,
                                              preferred_element_type=F32)
                m_scr[g, :, :blk] = m_lo
                m_scr[g, :, blk:] = m_hi

        if own_is_odd:
            fold_own_pair(True)
        else:
            pl.when(own == 0)(lambda: fold_own_pair(False))
            pl.when(own > 0)(lambda: fold_own_pair(True))

        def pipelined(slot_next, a_next, slot_cur, a_cur):
            for g in range(group):
                stage_scores(slot_next, g, scores(g, a_next))
                fold_head(slot_cur, g, a_cur)

        def two_pairs(t, carry):
            a0 = 2 * t
            pipelined(1, a0 + 1, 0, a0)
            pipelined(0, a0 + 2, 1, a0 + 1)
            return carry

        if own_is_odd:
            lax.fori_loop(0, lax.shift_right_logical(own - 1, 1), two_pairs, 0)
            for g in range(group):
                fold_head(0, g, own - 1)
        else:
            lax.fori_loop(0, lax.shift_right_logical(jnp.maximum(own - 2, 0), 1), two_pairs, 0)

            @pl.when(own >= 2)
            def _last_two_pairs():
                for g in range(group):
                    stage_scores(1, g, scores(g, own - 1))
                    fold_head(0, g, own - 2)
                for g in range(group):
                    fold_head(1, g, own - 1)

        for g, cols in enumerate(head_cols):
            acc = acc_scr[g]
            o = acc[:HEAD_DIM] * (1.0 / acc[HEAD_DIM:HEAD_DIM + 1])
            o_ref[rows, cols] = o.T.astype(o_ref.dtype)

    for t in range(tiles_per_step):
        process_tile(step * tiles_per_step + t, t % 2 == 1, slice(t * pair, (t + 1) * pair))


CAST_ROWS = 256


def _attn_and_cast_kernel(*refs, n_attn_in, n_cast, n_attn_scratch, chunk_table, **attn_kw):
    attn_in = refs[:n_attn_in]
    cast_in = refs[n_attn_in:n_attn_in + n_cast]
    o_ref = refs[n_attn_in + n_cast]
    cast_out = refs[n_attn_in + n_cast + 1:n_attn_in + 2 * n_cast + 1]
    rest = refs[n_attn_in + 2 * n_cast + 1:]
    attn_scratch, (ibuf, obuf, isem, osem) = rest[:n_attn_scratch], rest[n_attn_scratch:]
    lin = ((pl.program_id(0) * pl.num_programs(1) + pl.program_id(1)) * pl.num_programs(2)
           + pl.program_id(2))

    def fetch(k, j):
        rows = pl.ds(pl.multiple_of(j * CAST_ROWS, CAST_ROWS), CAST_ROWS)
        return pltpu.make_async_copy(cast_in[k].at[rows, :], ibuf, isem)

    def write_back(k, j):
        rows = pl.ds(pl.multiple_of(j * CAST_ROWS, CAST_ROWS), CAST_ROWS)
        return pltpu.make_async_copy(obuf, cast_out[k].at[rows, :], osem)

    def for_slab_of_step(s, fn):
        for k, (first, n) in enumerate(chunk_table):
            @pl.when(jnp.logical_and(s >= first, s < first + n))
            def _(k=k, first=first):
                fn(k, s - first)

    for_slab_of_step(lin, lambda k, j: fetch(k, j).start())
    _attn_kernel(*attn_in, o_ref, *attn_scratch, **attn_kw)
    for_slab_of_step(lin, lambda k, j: fetch(k, j).wait())
    for_slab_of_step(lin - 1, lambda k, j: write_back(k, j).wait())

    def cast_and_send(k, j):
        obuf[...] = ibuf[...].astype(BF16)
        write_back(k, j).start()

    for_slab_of_step(lin, cast_and_send)


def _attention(proj, e, *, moba, batch, seq, heads, q_col, k_col, v_col, cast_weights=()):
    group = ATTN_GROUP
    tiles_per_step = ATTN_TILES_PER_STEP
    tq = KEY_PAIR
    rows_per_step = tiles_per_step * tq
    nq = seq // rows_per_step
    n_blocks = seq // MOBA_BLOCK
    width = group * HEAD_DIM
    assert heads % group == 0 and seq % rows_per_step == 0 and n_blocks <= LANES
    assert tiles_per_step % 2 == 0, "the kernel takes the parity of a tile index from its slot"
    assert q_col % group == 0 and k_col % group == 0 and v_col % group == 0
    in_specs = [
        pl.BlockSpec((rows_per_step, width), lambda b, h, i: (b * nq + i, q_col // group + h)),
        pl.BlockSpec((seq, width), lambda b, h, i: (b, k_col // group + h)),
        pl.BlockSpec((seq, width), lambda b, h, i: (b, v_col // group + h)),
    ]
    args = [proj, proj, proj]
    if not moba:
        in_specs.append(pl.BlockSpec((seq, LANES), lambda b, h, i: (b, 0)))
        args.append(e)
    scratch = [
        pltpu.VMEM((group, seq // KEY_PAIR, VT_ROWS, KEY_PAIR), BF16),
        pltpu.VMEM((group, seq, 2 * HEAD_DIM), BF16),
        pltpu.VMEM((group, tq, 2 * HEAD_DIM), BF16),
        pltpu.VMEM((2, group, KEY_PAIR, tq), F32),
        pltpu.VMEM((2, group, 1, tq), F32),
        pltpu.VMEM((group, 1, tq), F32),
        pltpu.VMEM((group, VT_ROWS, tq), F32),
    ]
    if moba:
        scratch.append(pltpu.VMEM((group, n_blocks, HEAD_DIM), F32))
    attn_kw = dict(moba=moba, n_blocks=n_blocks, group=group, n_heads=heads,
                   tiles_per_step=tiles_per_step)
    grid = (batch, heads // group, nq)
    o_spec = pl.BlockSpec((rows_per_step, width), lambda b, h, i: (b * nq + i, h))
    o_shape = jax.ShapeDtypeStruct((batch * seq, heads * HEAD_DIM), BF16)
    name = "moba_attention" if moba else "fox_attention"
    if not cast_weights:
        return pl.pallas_call(
            functools.partial(_attn_kernel, **attn_kw),
            grid=grid, in_specs=in_specs, out_specs=o_spec, out_shape=o_shape,
            scratch_shapes=scratch, compiler_params=_compiler_params(3), name=name,
        )(*args)

    cols = cast_weights[0].shape[1]
    chunk_table, first = [], 0
    for w in cast_weights:
        assert w.shape[1] == cols and w.shape[0] % CAST_ROWS == 0 and w.dtype == F32
        chunk_table.append((first, w.shape[0] // CAST_ROWS))
        first += w.shape[0] // CAST_ROWS
    assert first < math.prod(grid), "the last slab's write is waited one step after its cast"
    any_spec = pl.BlockSpec(memory_space=pl.ANY)
    kern = functools.partial(
        _attn_and_cast_kernel, n_attn_in=len(args), n_cast=len(cast_weights),
        n_attn_scratch=len(scratch), chunk_table=tuple(chunk_table), **attn_kw)
    return pl.pallas_call(
        kern,
        grid=grid,
        in_specs=in_specs + [any_spec] * len(cast_weights),
        out_specs=[o_spec] + [any_spec] * len(cast_weights),
        out_shape=[o_shape] + [jax.ShapeDtypeStruct(w.shape, BF16) for w in cast_weights],
        scratch_shapes=scratch + [pltpu.VMEM((CAST_ROWS, cols), F32),
                                  pltpu.VMEM((CAST_ROWS, cols), BF16),
                                  pltpu.SemaphoreType.DMA(()), pltpu.SemaphoreType.DMA(())],
        compiler_params=_compiler_params(3),
        name=name,
    )(*args, *cast_weights)


def _rmsnorm(x, g):
    ms = jnp.mean(x * x, axis=-1, keepdims=True)
    return x * lax.rsqrt(ms + RMS_EPS) * g


def _out_merge_kernel(oa_ref, ob_ref, za_ref, zb_ref, ga_ref, gb_ref, x_ref, p_ref,
                      wa_ref, wb_ref, wo_ref, wg_ref, wu_ref, gple_ref, gfin_ref, out_ref,
                      *, final):
    def branch(o_ref, z_ref, w_ref):
        z = z_ref[...].astype(F32)
        a = o_ref[...].astype(F32) * (z * jax.nn.sigmoid(z))
        return jnp.dot(a.astype(BF16), w_ref[...], preferred_element_type=F32)

    p_in = p_ref[...].astype(BF16)
    d_half = out_ref.shape[1] // 2
    ya = branch(oa_ref, za_ref, wa_ref)
    yb = branch(ob_ref, zb_ref, wb_ref)
    up_lo = jnp.dot(p_in, wu_ref[:, :d_half], preferred_element_type=F32)
    mixed = (jax.nn.sigmoid(ga_ref[...].astype(F32)) * ya
             + jax.nn.sigmoid(gb_ref[...].astype(F32)) * yb)
    x1 = x_ref[...] + jnp.dot(mixed.astype(BF16), wo_ref[...], preferred_element_type=F32)
    up_hi = jnp.dot(p_in, wu_ref[:, d_half:], preferred_element_type=F32)
    hn = _rmsnorm(x1, gple_ref[...]).astype(BF16)
    pg = jax.nn.sigmoid(jnp.dot(hn, wg_ref[...], preferred_element_type=F32))
    x2 = x1 + jnp.concatenate([up_lo, up_hi], axis=1) * pg
    if final:
        x2 = _rmsnorm(x2, gfin_ref[...])
    out_ref[...] = x2


def _out_merge(oa, ob, proj, x2, p2, wa, wb, wo, wg, wu, g_ple, g_final, *, tm, final,
               za_col, zb_col, ga_col, gb_col):
    t, d = x2.shape
    wa_w = oa.shape[1]
    wb_w = ob.shape[1]
    ple = p2.shape[1]
    row = lambda i: (i, 0)
    const = lambda i: (0, 0)
    return pl.pallas_call(
        functools.partial(_out_merge_kernel, final=final),
        grid=(t // tm,),
        in_specs=[
            pl.BlockSpec((tm, wa_w), row),
            pl.BlockSpec((tm, wb_w), row),
            pl.BlockSpec((tm, wa_w), lambda i: (i, za_col)),
            pl.BlockSpec((tm, wb_w), lambda i: (i, zb_col)),
            pl.BlockSpec((tm, d), lambda i: (i, ga_col)),
            pl.BlockSpec((tm, d), lambda i: (i, gb_col)),
            pl.BlockSpec((tm, d), row),
            pl.BlockSpec((tm, ple), row),
            _resident((wa_w, d), const),
            _resident((wb_w, d), const),
            _resident((d, d), const),
            _resident((d, d), const),
            _resident((ple, d), const),
            _resident((1, d), const),
            _resident((1, d), const),
        ],
        out_specs=pl.BlockSpec((tm, d), row),
        out_shape=jax.ShapeDtypeStruct((t, d), F32),
        compiler_params=_compiler_params(1),
        name="out_merge",
    )(oa, ob, proj, proj, proj, proj, x2, p2, wa, wb, wo, wg, wu, g_ple, g_final)


def kernel(x, p, positions, g_norm, w_in, b_f, w_branch_a, w_branch_b, w_out,
           g_ple, w_ple_gate, w_ple_up, g_final):
    batch, seq, d = x.shape
    depth = w_in.shape[0]
    t = batch * seq
    wa_w = w_branch_a.shape[1]
    wb_w = w_branch_b.shape[1]
    heads_a = wa_w // HEAD_DIM
    heads_b = wb_w // HEAD_DIM
    n_f = b_f.shape[1]
    assert seq % MOBA_BLOCK == 0 and wa_w == wb_w and 2 * wa_w == d
    f_start = 4 * wa_w + 3 * wb_w
    assert w_in.shape[2] == f_start + n_f + wb_w + 2 * d

    inv = ROPE_THETA ** (-jnp.arange(0, HEAD_DIM, 2, dtype=F32) / HEAD_DIM)
    inv_full = jnp.concatenate([inv, inv]).reshape(1, HEAD_DIM)
    pos = positions.reshape(t, 1)
    x2 = x.reshape(t, d)
    hb = HEAD_DIM

    for layer in range(depth):
        w_t = jnp.swapaxes(w_in[layer], 0, 1)
        w_main, w_f = _w_in_prep(w_t, f_start=f_start, n_f=n_f, tn=wa_w)
        b_pad = jnp.pad(b_f[layer], (0, LANES - n_f)).reshape(1, LANES)

        proj, f = _in_proj(x2, pos, g_norm[layer].reshape(1, d), inv_full, w_main, w_f,
                           tm=IN_PROJ_ROWS, tn=IN_PROJ_SECTIONS * wa_w, section=wa_w)
        e = _fox_gate(f, b_pad, batch=batch, seq=seq, n_heads=heads_b)
        oa = _attention(proj, None, moba=True, batch=batch, seq=seq, heads=heads_a,
                        q_col=0, k_col=wa_w // hb, v_col=2 * wa_w // hb)
        ob, wa, wb, wo, wg, wu = _attention(
            proj, e, moba=False, batch=batch, seq=seq, heads=heads_b,
            q_col=4 * wa_w // hb, k_col=(4 * wa_w + wb_w) // hb,
            v_col=(4 * wa_w + 2 * wb_w) // hb,
            cast_weights=(w_branch_a[layer], w_branch_b[layer], w_out[layer],
                          w_ple_gate[layer], w_ple_up[layer]))
        x2 = _out_merge(
            oa, ob, proj, x2, p[layer].reshape(t, -1), wa, wb, wo, wg, wu,
            g_ple[layer].reshape(1, d), g_final.reshape(1, d),
            tm=OUT_MERGE_ROWS, final=(layer == depth - 1),
            za_col=3 * wa_w // wa_w, zb_col=(4 * wa_w + 3 * wb_w) // wb_w,
            ga_col=(4 * wa_w + 4 * wb_w) // d, gb_col=(4 * wa_w + 4 * wb_w) // d + 1)
    return x2.reshape(batch, seq, d)
```

```python
import functools
import math

import jax
import jax.numpy as jnp
from jax import lax
from jax.experimental import pallas as pl
from jax.experimental.pallas import tpu as pltpu

HEAD_DIM = 128
MOBA_BLOCK = 256
MOBA_TOPK = 3
ROPE_THETA = 10000.0
RMS_EPS = 1e-6

V7X_VMEM_BYTES = 64 * 1024 * 1024
VMEM_LIMIT_BYTES = V7X_VMEM_BYTES * 7 // 8
LANES = 128
BF16_SUBLANES = 16

IN_PROJ_ROWS = 1024
IN_PROJ_ROW_PIECE = 512
IN_PROJ_SECTIONS = 2
OUT_MERGE_ROWS = 256

F32 = jnp.float32
BF16 = jnp.bfloat16
NEG_INF = float("-inf")
MASKED = -1e30
LOG2E = math.log2(math.e)
NT_DIMS = (((1,), (1,)), ((), ()))


def _compiler_params(n_grid_dims):
    return pltpu.CompilerParams(
        dimension_semantics=("arbitrary",) * n_grid_dims,
        vmem_limit_bytes=VMEM_LIMIT_BYTES,
    )


def _resident(block_shape, index_map):
    return pl.BlockSpec(block_shape, index_map, pipeline_mode=pl.Buffered(1))


F32_SUBLANES = 8
PREP_COLS = 512


def _w_in_prep_kernel(wt_hbm, out_ref, wf_ref, buf, fbuf, sem, fsem, *, n_lo, n_f):
    n = pl.program_id(0)
    tn = out_ref.shape[1]

    def fetch(j, slot):
        first_row = pl.multiple_of(j * tn + jnp.where(j >= n_lo, n_f, 0), F32_SUBLANES)
        return pltpu.make_async_copy(wt_hbm.at[pl.ds(first_row, tn), :], buf.at[slot],
                                     sem.at[slot])

    slot = jnp.bitwise_and(n, 1)

    @pl.when(n == 0)
    def _first():
        fetch(0, 0).start()
        gate_rows = pltpu.make_async_copy(wt_hbm.at[pl.ds(n_lo * tn, n_f), :], fbuf, fsem)
        gate_rows.start()
        gate_rows.wait()
        padded = jnp.concatenate(
            [fbuf[...], jnp.zeros((LANES - n_f, fbuf.shape[1]), F32)], axis=0)
        wf_ref[...] = padded.T.astype(BF16)

    @pl.when(n + 1 < pl.num_programs(0))
    def _ahead():
        fetch(n + 1, 1 - slot).start()

    fetch(n, slot).wait()
    for c0 in range(0, buf.shape[2], PREP_COLS):
        out_ref[c0:c0 + PREP_COLS, :] = buf[slot, :, c0:c0 + PREP_COLS].T.astype(BF16)


def _w_in_prep(w_t, *, f_start, n_f, tn):
    n_in, d = w_t.shape
    n_lo = f_start // tn
    n_tiles = (n_in - n_f) // tn
    assert f_start == n_lo * tn and n_in == n_tiles * tn + n_f
    assert n_f % F32_SUBLANES == 0 and n_f <= LANES and d % PREP_COLS == 0
    return pl.pallas_call(
        functools.partial(_w_in_prep_kernel, n_lo=n_lo, n_f=n_f),
        grid=(n_tiles,),
        in_specs=[pl.BlockSpec(memory_space=pl.ANY)],
        out_specs=[pl.BlockSpec((d, tn), lambda n: (0, n)),
                   pl.BlockSpec((d, LANES), lambda n: (0, 0))],
        out_shape=[jax.ShapeDtypeStruct((d, n_tiles * tn), BF16),
                   jax.ShapeDtypeStruct((d, LANES), BF16)],
        scratch_shapes=[pltpu.VMEM((2, tn, d), F32), pltpu.VMEM((n_f, d), F32),
                        pltpu.SemaphoreType.DMA((2,)), pltpu.SemaphoreType.DMA(())],
        compiler_params=_compiler_params(1),
        name="w_in_prep",
    )(w_t)


MXU_COLS = 256


def _in_proj_kernel(x_ref, pos_ref, g_ref, inv_ref, w_ref, wf_ref, proj_ref, f_ref,
                    h_scr, cos_scr, sin_scr, *, section, rope_sections, scaled_sections,
                    q_scale, row_tile):
    n = pl.program_id(1)
    half = HEAD_DIM // 2

    @pl.when(n == 0)
    def _normalize_and_tables():
        x = x_ref[...]
        ms = jnp.mean(x * x, axis=-1, keepdims=True)
        h = (x * lax.rsqrt(ms + RMS_EPS) * g_ref[...]).astype(BF16)
        h_scr[...] = h
        f_ref[...] = jnp.dot(h, wf_ref[...], preferred_element_type=F32)
        lane = lax.broadcasted_iota(jnp.int32, (x.shape[0], HEAD_DIM), 1)
        first = lane < half
        ang = pos_ref[...].astype(F32) * inv_ref[...]
        cs = jnp.cos(jnp.where(first, ang, ang - 0.5 * math.pi))
        sc = pltpu.roll(cs, half, 1)
        cos_scr[...] = jnp.where(first, cs, sc)
        sin_scr[...] = jnp.where(first, -sc, cs)

    tn = proj_ref.shape[1]
    coef_c, coef_s = [], []
    for k in range(tn // section):
        sec = n * (tn // section) + k
        is_rope = functools.reduce(jnp.logical_or, [sec == s for s in rope_sections])
        scaled = functools.reduce(jnp.logical_or, [sec == s for s in scaled_sections])
        scale = jnp.where(scaled, q_scale, 1.0).astype(F32)
        coef_c.append(jnp.where(is_rope, cos_scr[...], 1.0) * scale)
        coef_s.append(jnp.where(is_rope, sin_scr[...], 0.0) * scale)
    tm = h_scr.shape[0]
    for r0 in range(0, tm, row_tile):
        rows = slice(r0, r0 + row_tile)
        h = h_scr[rows, :]
        for c0 in range(0, tn, MXU_COLS):
            acc = jnp.dot(h, w_ref[:, c0:c0 + MXU_COLS],
                          preferred_element_type=F32)
            cc, cs = coef_c[c0 // section][rows], coef_s[c0 // section][rows]
            for j in range(0, MXU_COLS, HEAD_DIM):
                a = acc[:, j:j + HEAD_DIM]
                r = a * cc + pltpu.roll(a, half, 1) * cs
                proj_ref[rows, c0 + j:c0 + j + HEAD_DIM] = r.astype(BF16)


def _in_proj(x2, pos, g, inv_full, w_main, w_f, *, tm, tn, section):
    t, d = x2.shape
    n_main = w_main.shape[1]
    assert tn % section == 0 and n_main % tn == 0 and section % MXU_COLS == 0
    kern = functools.partial(
        _in_proj_kernel, section=section, rope_sections=(0, 1), scaled_sections=(0, 4),
        q_scale=LOG2E / math.sqrt(HEAD_DIM), row_tile=min(IN_PROJ_ROW_PIECE, tm))
    return pl.pallas_call(
        kern,
        grid=(t // tm, n_main // tn),
        in_specs=[
            pl.BlockSpec((tm, d), lambda i, n: (i, 0)),
            pl.BlockSpec((tm, 1), lambda i, n: (i, 0)),
            _resident((1, d), lambda i, n: (0, 0)),
            _resident((1, LANES), lambda i, n: (0, 0)),
            pl.BlockSpec((d, tn), lambda i, n: (0, n)),
            _resident((d, LANES), lambda i, n: (0, 0)),
        ],
        out_specs=[
            pl.BlockSpec((tm, tn), lambda i, n: (i, n)),
            pl.BlockSpec((tm, LANES), lambda i, n: (i, 0)),
        ],
        out_shape=[
            jax.ShapeDtypeStruct((t, n_main), BF16),
            jax.ShapeDtypeStruct((t, LANES), F32),
        ],
        scratch_shapes=[
            pltpu.VMEM((tm, d), BF16),
            pltpu.VMEM((tm, LANES), F32),
            pltpu.VMEM((tm, LANES), F32),
        ],
        compiler_params=_compiler_params(2),
        name="in_proj",
    )(x2, pos, g, inv_full, w_main, w_f)


FOX_TERMS = 3


def _fox_gate_kernel(f_ref, b_ref, e_ref, *, chunk, n_heads):
    n_chunks = f_ref.shape[0] // chunk
    r = lax.broadcasted_iota(jnp.int32, (chunk, chunk), 0)
    c = lax.broadcasted_iota(jnp.int32, (chunk, chunk), 1)
    tri = (c <= r).astype(BF16)
    z = f_ref[...] + b_ref[...]
    logf = (jnp.minimum(z, 0.0) - jnp.log1p(jnp.exp(-jnp.abs(z)))) * LOG2E
    wide = jnp.concatenate([logf[j * chunk:(j + 1) * chunk] for j in range(n_chunks)], axis=1)
    local = jnp.zeros_like(wide)
    for _ in range(FOX_TERMS):
        term = wide.astype(BF16)
        local = local + jnp.dot(tri, term, preferred_element_type=F32)
        wide = wide - term.astype(F32)
    lane = lax.broadcasted_iota(jnp.int32, (chunk, LANES), 1)
    offset = jnp.zeros((1, LANES), F32)
    for j in range(n_chunks):
        csum = local[:, j * LANES:(j + 1) * LANES] + offset
        offset = csum[chunk - 1:chunk, :]
        hi = csum.astype(BF16).astype(F32)
        mid = (csum - hi).astype(BF16).astype(F32)
        lo = csum - hi - mid
        feats = jnp.where(
            lane < n_heads, hi,
            jnp.where(lane < 2 * n_heads, pltpu.roll(mid, n_heads, 1),
                      jnp.where(lane < 3 * n_heads, pltpu.roll(lo, 2 * n_heads, 1), 0.0)))
        e_ref[j * chunk:(j + 1) * chunk, :] = feats.astype(BF16)


def _fox_gate(f, b_pad, *, batch, seq, n_heads):
    assert FOX_TERMS * n_heads <= LANES
    return pl.pallas_call(
        functools.partial(_fox_gate_kernel, chunk=MOBA_BLOCK, n_heads=n_heads),
        grid=(batch,),
        in_specs=[
            pl.BlockSpec((seq, LANES), lambda b: (b, 0)),
            _resident((1, LANES), lambda b: (0, 0)),
        ],
        out_specs=pl.BlockSpec((seq, LANES), lambda b: (b, 0)),
        out_shape=jax.ShapeDtypeStruct(f.shape, BF16),
        compiler_params=_compiler_params(1),
        name="fox_gate",
    )(f, b_pad)


ATTN_GROUP = 4
VT_ROWS = HEAD_DIM + BF16_SUBLANES
KEY_PAIR = 2 * MOBA_BLOCK
ATTN_TILES_PER_STEP = 2


def _attn_kernel(*refs, moba, n_blocks, group, n_heads, tiles_per_step):
    if moba:
        (q_ref, k_ref, v_ref, o_ref,
         vt_scr, kaug_scr, qaug_scr, s_scr, smax_scr, m_scr, acc_scr, kbar_scr) = refs
    else:
        (q_ref, k_ref, v_ref, e_ref, o_ref,
         vt_scr, kaug_scr, qaug_scr, s_scr, smax_scr, m_scr, acc_scr) = refs
    blk = MOBA_BLOCK
    pair = KEY_PAIR
    hg = pl.program_id(1)
    step = pl.program_id(2)
    head_cols = [slice(g * HEAD_DIM, (g + 1) * HEAD_DIM) for g in range(group)]
    extra = slice(HEAD_DIM, 2 * HEAD_DIM)

    @pl.when(step == 0)
    def _per_group_setup():
        ones_row = (lax.broadcasted_iota(jnp.int32, (BF16_SUBLANES, pair), 0) == 0).astype(BF16)
        lane = lax.broadcasted_iota(jnp.int32, (pair, LANES), 1)
        row = lax.broadcasted_iota(jnp.int32, (pair, LANES), 0)

        def body(a, carry):
            off = pl.multiple_of(a * pair, pair)
            if moba:
                block_of_row = 2 * a + (row >= blk).astype(jnp.int32)
                feats = jnp.where(lane == block_of_row, 1.0, 0.0).astype(BF16)
            else:
                feats = e_ref[pl.ds(off, pair), :]
            for g, cols in enumerate(head_cols):
                vt_scr[g, a, :HEAD_DIM, :] = v_ref[pl.ds(off, pair), cols].T
                vt_scr[g, a, HEAD_DIM:, :] = ones_row
                kb = k_ref[pl.ds(off, pair), cols]
                kaug_scr[g, pl.ds(off, pair), :HEAD_DIM] = kb
                kaug_scr[g, pl.ds(off, pair), extra] = feats
                if moba:
                    kf = kb.astype(F32)
                    kbar_scr[g, pl.ds(2 * a, 1), :] = jnp.mean(kf[:blk], axis=0, keepdims=True)
                    kbar_scr[g, pl.ds(2 * a + 1, 1), :] = jnp.mean(kf[blk:], axis=0,
                                                                   keepdims=True)
            return carry
        lax.fori_loop(0, n_blocks // 2, body, 0)

    def process_tile(own, own_is_odd, rows):
        def widen_queries(g):
            q = q_ref[rows, head_cols[g]]
            qaug_scr[g, :, :HEAD_DIM] = q
            if moba:
                kbar = kbar_scr[g]
                hi = kbar.astype(BF16)
                mid = (kbar - hi.astype(F32)).astype(BF16)
                lo = (kbar - hi.astype(F32) - mid.astype(F32)).astype(BF16)
                gs3 = lax.dot_general(jnp.concatenate([hi, mid, lo], axis=0), q, NT_DIMS,
                                      preferred_element_type=F32)
                gs = gs3[:n_blocks] + gs3[n_blocks:2 * n_blocks] + gs3[2 * n_blocks:]
                blk_id = lax.broadcasted_iota(jnp.int32, gs.shape, 0)
                q_blk = 2 * own + (lax.broadcasted_iota(jnp.int32, gs.shape, 1) >= blk).astype(
                    jnp.int32)
                gs = jnp.where(blk_id < q_blk, gs, NEG_INF)
                bias = jnp.where(blk_id == q_blk, 0.0, MASKED)
                for _ in range(MOBA_TOPK):
                    mx = jnp.max(gs, axis=0, keepdims=True)
                    first = jnp.min(jnp.where(gs == mx, blk_id, n_blocks), axis=0,
                                    keepdims=True)
                    pick = jnp.logical_and(blk_id == first, mx > NEG_INF)
                    bias = jnp.where(pick, 0.0, bias)
                    gs = jnp.where(pick, NEG_INF, gs)
                bias = jnp.concatenate([bias, jnp.zeros((LANES - n_blocks, pair), F32)], axis=0)
                qaug_scr[g, :, extra] = bias.T.astype(BF16)
            else:
                lane = lax.broadcasted_iota(jnp.int32, (pair, LANES), 1)
                head = hg * group + g
                mine = functools.reduce(
                    jnp.logical_or, [lane == head + j * n_heads for j in range(FOX_TERMS)])
                qaug_scr[g, :, extra] = jnp.where(mine, -1.0, 0.0).astype(BF16)

        def scores(g, a):
            off = pl.multiple_of(a * pair, pair)
            return lax.dot_general(kaug_scr[g, pl.ds(off, pair), :], qaug_scr[g], NT_DIMS,
                                   preferred_element_type=F32)

        def stage_scores(slot, g, s):
            s_scr[slot, g] = s
            smax_scr[slot, g] = jnp.max(s, axis=0, keepdims=True)

        def fold_head(slot, g, a):
            m_old = m_scr[g]
            m_new = jnp.maximum(m_old, smax_scr[slot, g])
            p = jnp.exp2(s_scr[slot, g] - m_new).astype(BF16)
            pv = jnp.dot(vt_scr[g, a], p, preferred_element_type=F32)
            acc_scr[g] = jnp.exp2(m_old - m_new) * acc_scr[g] + pv
            m_scr[g] = m_new

        own_off = pl.multiple_of(own * pair, pair)
        visible_lo = (lax.broadcasted_iota(jnp.int32, (blk, blk), 0)
                      <= lax.broadcasted_iota(jnp.int32, (blk, blk), 1))
        visible_hi = (lax.broadcasted_iota(jnp.int32, (pair, blk), 0)
                      <= lax.broadcasted_iota(jnp.int32, (pair, blk), 1) + blk)
        for g in range(group):
            widen_queries(g)
        for g in range(group):
            s_lo = lax.dot_general(kaug_scr[g, pl.ds(own_off, blk), :], qaug_scr[g, :blk, :],
                                   NT_DIMS, preferred_element_type=F32)
            s_hi = lax.dot_general(kaug_scr[g, pl.ds(own_off, pair), :], qaug_scr[g, blk:, :],
                                   NT_DIMS, preferred_element_type=F32)
            s_scr[1, g, :blk, :blk] = jnp.where(visible_lo, s_lo, NEG_INF)
            s_scr[1, g, :, blk:] = jnp.where(visible_hi, s_hi, NEG_INF)
        for g in range(group):
            stage_scores(0, g, scores(g, 0))
            s_lo = s_scr[1, g, :blk, :blk]
            s_hi = s_scr[1, g, :, blk:]
            m_lo = jnp.max(s_lo, axis=0, keepdims=True)
            m_hi = jnp.max(s_hi, axis=0, keepdims=True)
            p_lo = jnp.exp2(s_lo - m_lo).astype(BF16)
            p_hi = jnp.exp2(s_hi - m_hi).astype(BF16)
            acc_scr[g, :, :blk] = jnp.dot(vt_scr[g, own, :, :blk], p_lo,
                                          preferred_element_type=F32)
            acc_scr[g, :, blk:] = jnp.dot(vt_scr[g, own], p_hi, preferred_element_type=F32)
            m_scr[g, :, :blk] = m_lo
            m_scr[g, :, blk:] = m_hi

        def pipelined(slot_next, a_next, slot_cur, a_cur):
            for g in range(group):
                stage_scores(slot_next, g, scores(g, a_next))
                fold_head(slot_cur, g, a_cur)

        def two_pairs(t, carry):
            a0 = 2 * t
            pipelined(1, a0 + 1, 0, a0)
            pipelined(0, a0 + 2, 1, a0 + 1)
            return carry

        if own_is_odd:
            lax.fori_loop(0, lax.shift_right_logical(own - 1, 1), two_pairs, 0)
            for g in range(group):
                fold_head(0, g, own - 1)
        else:
            lax.fori_loop(0, lax.shift_right_logical(jnp.maximum(own - 2, 0), 1), two_pairs, 0)

            @pl.when(own >= 2)
            def _last_two_pairs():
                for g in range(group):
                    stage_scores(1, g, scores(g, own - 1))
                    fold_head(0, g, own - 2)
                for g in range(group):
                    fold_head(1, g, own - 1)

        for g, cols in enumerate(head_cols):
            acc = acc_scr[g]
            o = acc[:HEAD_DIM] * (1.0 / acc[HEAD_DIM:HEAD_DIM + 1])
            o_ref[rows, cols] = o.T.astype(o_ref.dtype)

    for t in range(tiles_per_step):
        process_tile(step * tiles_per_step + t, t % 2 == 1, slice(t * pair, (t + 1) * pair))


CAST_ROWS = 256


def _attn_and_cast_kernel(*refs, n_attn_in, n_cast, n_attn_scratch, chunk_table, **attn_kw):
    attn_in = refs[:n_attn_in]
    cast_in = refs[n_attn_in:n_attn_in + n_cast]
    o_ref = refs[n_attn_in + n_cast]
    cast_out = refs[n_attn_in + n_cast + 1:n_attn_in + 2 * n_cast + 1]
    rest = refs[n_attn_in + 2 * n_cast + 1:]
    attn_scratch, (ibuf, obuf, isem, osem) = rest[:n_attn_scratch], rest[n_attn_scratch:]
    lin = ((pl.program_id(0) * pl.num_programs(1) + pl.program_id(1)) * pl.num_programs(2)
           + pl.program_id(2))

    def fetch(k, j):
        rows = pl.ds(pl.multiple_of(j * CAST_ROWS, CAST_ROWS), CAST_ROWS)
        return pltpu.make_async_copy(cast_in[k].at[rows, :], ibuf, isem)

    def write_back(k, j):
        rows = pl.ds(pl.multiple_of(j * CAST_ROWS, CAST_ROWS), CAST_ROWS)
        return pltpu.make_async_copy(obuf, cast_out[k].at[rows, :], osem)

    def for_slab_of_step(s, fn):
        for k, (first, n) in enumerate(chunk_table):
            @pl.when(jnp.logical_and(s >= first, s < first + n))
            def _(k=k, first=first):
                fn(k, s - first)

    for_slab_of_step(lin, lambda k, j: fetch(k, j).start(priority=1))
    _attn_kernel(*attn_in, o_ref, *attn_scratch, **attn_kw)
    for_slab_of_step(lin, lambda k, j: fetch(k, j).wait())
    for_slab_of_step(lin - 1, lambda k, j: write_back(k, j).wait())

    def cast_and_send(k, j):
        obuf[...] = ibuf[...].astype(BF16)
        write_back(k, j).start(priority=1)

    for_slab_of_step(lin, cast_and_send)


def _attention(proj, e, *, moba, batch, seq, heads, q_col, k_col, v_col, cast_weights=()):
    group = ATTN_GROUP
    tiles_per_step = ATTN_TILES_PER_STEP
    tq = KEY_PAIR
    rows_per_step = tiles_per_step * tq
    nq = seq // rows_per_step
    n_blocks = seq // MOBA_BLOCK
    width = group * HEAD_DIM
    assert heads % group == 0 and seq % rows_per_step == 0 and n_blocks <= LANES
    assert tiles_per_step % 2 == 0, "the kernel takes the parity of a tile index from its slot"
    assert q_col % group == 0 and k_col % group == 0 and v_col % group == 0
    in_specs = [
        pl.BlockSpec((rows_per_step, width), lambda b, h, i: (b * nq + i, q_col // group + h)),
        pl.BlockSpec((seq, width), lambda b, h, i: (b, k_col // group + h)),
        pl.BlockSpec((seq, width), lambda b, h, i: (b, v_col // group + h)),
    ]
    args = [proj, proj, proj]
    if not moba:
        in_specs.append(pl.BlockSpec((seq, LANES), lambda b, h, i: (b, 0)))
        args.append(e)
    scratch = [
        pltpu.VMEM((group, seq // KEY_PAIR, VT_ROWS, KEY_PAIR), BF16),
        pltpu.VMEM((group, seq, 2 * HEAD_DIM), BF16),
        pltpu.VMEM((group, tq, 2 * HEAD_DIM), BF16),
        pltpu.VMEM((2, group, KEY_PAIR, tq), F32),
        pltpu.VMEM((2, group, 1, tq), F32),
        pltpu.VMEM((group, 1, tq), F32),
        pltpu.VMEM((group, VT_ROWS, tq), F32),
    ]
    if moba:
        scratch.append(pltpu.VMEM((group, n_blocks, HEAD_DIM), F32))
    attn_kw = dict(moba=moba, n_blocks=n_blocks, group=group, n_heads=heads,
                   tiles_per_step=tiles_per_step)
    grid = (batch, heads // group, nq)
    o_spec = pl.BlockSpec((rows_per_step, width), lambda b, h, i: (b * nq + i, h))
    o_shape = jax.ShapeDtypeStruct((batch * seq, heads * HEAD_DIM), BF16)
    name = "moba_attention" if moba else "fox_attention"
    if not cast_weights:
        return pl.pallas_call(
            functools.partial(_attn_kernel, **attn_kw),
            grid=grid, in_specs=in_specs, out_specs=o_spec, out_shape=o_shape,
            scratch_shapes=scratch, compiler_params=_compiler_params(3), name=name,
        )(*args)

    cols = cast_weights[0].shape[1]
    chunk_table, first = [], 0
    for w in cast_weights:
        assert w.shape[1] == cols and w.shape[0] % CAST_ROWS == 0 and w.dtype == F32
        chunk_table.append((first, w.shape[0] // CAST_ROWS))
        first += w.shape[0] // CAST_ROWS
    assert first < math.prod(grid), "the last slab's write is waited one step after its cast"
    any_spec = pl.BlockSpec(memory_space=pl.ANY)
    kern = functools.partial(
        _attn_and_cast_kernel, n_attn_in=len(args), n_cast=len(cast_weights),
        n_attn_scratch=len(scratch), chunk_table=tuple(chunk_table), **attn_kw)
    return pl.pallas_call(
        kern,
        grid=grid,
        in_specs=in_specs + [any_spec] * len(cast_weights),
        out_specs=[o_spec] + [any_spec] * len(cast_weights),
        out_shape=[o_shape] + [jax.ShapeDtypeStruct(w.shape, BF16) for w in cast_weights],
        scratch_shapes=scratch + [pltpu.VMEM((CAST_ROWS, cols), F32),
                                  pltpu.VMEM((CAST_ROWS, cols), BF16),
                                  pltpu.SemaphoreType.DMA(()), pltpu.SemaphoreType.DMA(())],
        compiler_params=_compiler_params(3),
        name=name,
    )(*args, *cast_weights)


def _rmsnorm(x, g):
    ms = jnp.mean(x * x, axis=-1, keepdims=True)
    return x * lax.rsqrt(ms + RMS_EPS) * g


def _out_merge_kernel(oa_ref, ob_ref, za_ref, zb_ref, ga_ref, gb_ref, x_ref, p_ref,
                      wa_ref, wb_ref, wo_ref, wg_ref, wu_ref, gple_ref, gfin_ref, out_ref,
                      *, final):
    def branch(o_ref, z_ref, w_ref):
        z = z_ref[...].astype(F32)
        a = o_ref[...].astype(F32) * (z * jax.nn.sigmoid(z))
        return jnp.dot(a.astype(BF16), w_ref[...], preferred_element_type=F32)

    p_in = p_ref[...].astype(BF16)
    d_half = out_ref.shape[1] // 2
    ya = branch(oa_ref, za_ref, wa_ref)
    yb = branch(ob_ref, zb_ref, wb_ref)
    up_lo = jnp.dot(p_in, wu_ref[:, :d_half], preferred_element_type=F32)
    mixed = (jax.nn.sigmoid(ga_ref[...].astype(F32)) * ya
             + jax.nn.sigmoid(gb_ref[...].astype(F32)) * yb)
    x1 = x_ref[...] + jnp.dot(mixed.astype(BF16), wo_ref[...], preferred_element_type=F32)
    up_hi = jnp.dot(p_in, wu_ref[:, d_half:], preferred_element_type=F32)
    hn = _rmsnorm(x1, gple_ref[...]).astype(BF16)
    pg = jax.nn.sigmoid(jnp.dot(hn, wg_ref[...], preferred_element_type=F32))
    x2 = x1 + jnp.concatenate([up_lo, up_hi], axis=1) * pg
    if final:
        x2 = _rmsnorm(x2, gfin_ref[...])
    out_ref[...] = x2


def _out_merge(oa, ob, proj, x2, p2, wa, wb, wo, wg, wu, g_ple, g_final, *, tm, final,
               za_col, zb_col, ga_col, gb_col):
    t, d = x2.shape
    wa_w = oa.shape[1]
    wb_w = ob.shape[1]
    ple = p2.shape[1]
    row = lambda i: (i, 0)
    const = lambda i: (0, 0)
    return pl.pallas_call(
        functools.partial(_out_merge_kernel, final=final),
        grid=(t // tm,),
        in_specs=[
            pl.BlockSpec((tm, wa_w), row),
            pl.BlockSpec((tm, wb_w), row),
            pl.BlockSpec((tm, wa_w), lambda i: (i, za_col)),
            pl.BlockSpec((tm, wb_w), lambda i: (i, zb_col)),
            pl.BlockSpec((tm, d), lambda i: (i, ga_col)),
            pl.BlockSpec((tm, d), lambda i: (i, gb_col)),
            pl.BlockSpec((tm, d), row),
            pl.BlockSpec((tm, ple), row),
            _resident((wa_w, d), const),
            _resident((wb_w, d), const),
            _resident((d, d), const),
            _resident((d, d), const),
            _resident((ple, d), const),
            _resident((1, d), const),
            _resident((1, d), const),
        ],
        out_specs=pl.BlockSpec((tm, d), row),
        out_shape=jax.ShapeDtypeStruct((t, d), F32),
        compiler_params=_compiler_params(1),
        name="out_merge",
    )(oa, ob, proj, proj, proj, proj, x2, p2, wa, wb, wo, wg, wu, g_ple, g_final)


def kernel(x, p, positions, g_norm, w_in, b_f, w_branch_a, w_branch_b, w_out,
           g_ple, w_ple_gate, w_ple_up, g_final):
    batch, seq, d = x.shape
    depth = w_in.shape[0]
    t = batch * seq
    wa_w = w_branch_a.shape[1]
    wb_w = w_branch_b.shape[1]
    heads_a = wa_w // HEAD_DIM
    heads_b = wb_w // HEAD_DIM
    n_f = b_f.shape[1]
    assert seq % MOBA_BLOCK == 0 and wa_w == wb_w and 2 * wa_w == d
    f_start = 4 * wa_w + 3 * wb_w
    assert w_in.shape[2] == f_start + n_f + wb_w + 2 * d

    inv = ROPE_THETA ** (-jnp.arange(0, HEAD_DIM, 2, dtype=F32) / HEAD_DIM)
    inv_full = jnp.concatenate([inv, inv]).reshape(1, HEAD_DIM)
    pos = positions.reshape(t, 1)
    x2 = x.reshape(t, d)
    hb = HEAD_DIM

    for layer in range(depth):
        w_t = jnp.swapaxes(w_in[layer], 0, 1)
        w_main, w_f = _w_in_prep(w_t, f_start=f_start, n_f=n_f, tn=wa_w)
        b_pad = jnp.pad(b_f[layer], (0, LANES - n_f)).reshape(1, LANES)

        proj, f = _in_proj(x2, pos, g_norm[layer].reshape(1, d), inv_full, w_main, w_f,
                           tm=IN_PROJ_ROWS, tn=IN_PROJ_SECTIONS * wa_w, section=wa_w)
        e = _fox_gate(f, b_pad, batch=batch, seq=seq, n_heads=heads_b)
        oa = _attention(proj, None, moba=True, batch=batch, seq=seq, heads=heads_a,
                        q_col=0, k_col=wa_w // hb, v_col=2 * wa_w // hb)
        ob, wa, wb, wo, wg, wu = _attention(
            proj, e, moba=False, batch=batch, seq=seq, heads=heads_b,
            q_col=4 * wa_w // hb, k_col=(4 * wa_w + wb_w) // hb,
            v_col=(4 * wa_w + 2 * wb_w) // hb,
            cast_weights=(w_branch_a[layer], w_branch_b[layer], w_out[layer],
                          w_ple_gate[layer], w_ple_up[layer]))
        x2 = _out_merge(
            oa, ob, proj, x2, p[layer].reshape(t, -1), wa, wb, wo, wg, wu,
            g_ple[layer].reshape(1, d), g_final.reshape(1, d),
            tm=OUT_MERGE_ROWS, final=(layer == depth - 1),
            za_col=3 * wa_w // wa_w, zb_col=(4 * wa_w + 3 * wb_w) // wb_w,
            ga_col=(4 * wa_w + 4 * wb_w) // d, gb_col=(4 * wa_w + 4 * wb_w) // d + 1)
    return x2.reshape(batch, seq, d)
```
